```python
import math
import jax, jax.numpy as jnp
from jax import lax
import numpy as np

D_MODEL = 1024
BATCH = 8
SEQ = 2048
DEPTH = 4
DEC_BATCH = 128
DEC_SEQ = 8
PAST_LEN = 16384
PAGE_SIZE = 128

D_A = D_MODEL // 4
D_B = D_MODEL // 4
H_B = 4
DH_B = D_B // H_B
D_C = D_MODEL // 2
H_C = 4
DV_C = D_C // H_C
DK_C = DV_C // 2
GATE_RANK = 16
GATE_TAU = 16.0
CHUNK_B = 128
CHUNK_C = 64
CONV_W = 3
D_FF = 2816
LN_EPS = 1e-5
DEEPNORM_ALPHA = (2.0 * DEPTH) ** 0.25
DEEPNORM_BETA = (8.0 * DEPTH) ** -0.25
IN_SPLITS = (D_A, D_A, D_A, D_B, D_B, H_C * DK_C, H_C * DK_C, D_C, D_C, GATE_RANK)
N_IN = sum(IN_SPLITS)

kernel_name = "hymba_conv_chunkmlp_gla_deepnorm_step"


def layer_norm(x, g, b):
    xf = x.astype(jnp.float32)
    mu = jnp.mean(xf, axis=-1, keepdims=True)
    var = jnp.mean(jnp.square(xf - mu), axis=-1, keepdims=True)
    return ((xf - mu) * lax.rsqrt(var + LN_EPS)).astype(x.dtype) * g + b


def rms_norm(x, g):
    xf = x.astype(jnp.float32)
    y = xf * lax.rsqrt(jnp.mean(xf * xf, axis=-1, keepdims=True) + LN_EPS)
    return y.astype(g.dtype) * g


def causal_dwconv(x, past, w):
    T = x.shape[1]
    xp = jnp.concatenate([past.astype(x.dtype), x], axis=1)
    y = xp[:, 0:T] * w[0]
    for j in range(1, CONV_W):
        y = y + xp[:, j:j + T] * w[j]
    return y, xp[:, -(CONV_W - 1):]


def split_cols(z):
    idx, acc = [], 0
    for width in IN_SPLITS[:-1]:
        acc += width
        idx.append(acc)
    return jnp.split(z, idx, axis=-1)


def chunk_mlp_mix(v, w_s, b_s):
    B, T, H, dh = v.shape
    L = min(T, CHUNK_B)
    n = T // L
    mask = jnp.tril(jnp.ones((L, L), dtype=bool))
    w = jnp.where(mask[None], w_s[:, :L, :L], 0)
    vc = v.reshape(B, n, L, H, dh)
    mixed = jnp.einsum("hts,bnshd->bnthd", w, vc) + b_s[:, :L].T[None, None, :, :, None]
    return mixed.reshape(B, T, H, dh)


def gla_block_step(S, inp):
    q, k, v, lg = inp
    L = q.shape[1]
    b = jnp.cumsum(lg, axis=1)
    causal = jnp.tril(jnp.ones((L, L), dtype=bool))
    diff = b[:, :, None] - b[:, None, :]
    decay = jnp.exp(jnp.where(causal[None, :, :, None, None], diff, -jnp.inf))
    A = jnp.einsum("bthd,bshd,btshd->bhts", q, k, decay)
    o = jnp.einsum("bhts,bshv->bthv", A, v) + jnp.einsum("bthd,bhdv->bthv", q * jnp.exp(b), S)
    bL = b[:, -1]
    k_dec = k * jnp.exp(bL[:, None] - b)
    S_new = jnp.exp(bL)[..., None] * S + jnp.einsum("bshd,bshv->bhdv", k_dec, v)
    return S_new, o


def gla(q, k, v, lg, S0):
    B, T = q.shape[0], q.shape[1]
    L = CHUNK_C if T % CHUNK_C == 0 else T
    n = T // L

    def to_blocks(a):
        return jnp.moveaxis(a.astype(jnp.float32).reshape(B, n, L, *a.shape[2:]), 1, 0)

    S, o = lax.scan(gla_block_step, S0.astype(jnp.float32),
                    (to_blocks(q), to_blocks(k), to_blocks(v), to_blocks(lg)))
    o = jnp.moveaxis(o, 0, 1).reshape(B, T, H_C, DV_C)
    return o, S


def token_mixers(u, conv_a_past, gla_S0, l, p):
    B, T, _ = u.shape
    z = u @ p["w_in"][l]
    a_in, a_b, a_c, s_u, s_v, c_q, c_k, c_v, c_g, c_r = split_cols(z)
    a_conv, conv_a_new = causal_dwconv(a_c * a_in, conv_a_past, p["conv_a_w"][l])
    y_a = a_b * a_conv
    s_u = jax.nn.gelu(s_u)
    s_v = layer_norm(jax.nn.gelu(s_v), p["sgu_ln_g"][l], p["sgu_ln_b"][l])
    mixed = chunk_mlp_mix(s_v.reshape(B, T, H_B, DH_B), p["sgu_w"][l], p["sgu_b"][l])
    y_b = s_u * mixed.reshape(B, T, D_B)
    open_rows = T % CHUNK_B or CHUNK_B
    v_rows = s_v[:, T - open_rows:]
    lg = jax.nn.log_sigmoid((c_r @ p["gla_w_a2"][l] + p["gla_b_a"][l]).astype(jnp.float32)) / GATE_TAU
    q = c_q.reshape(B, T, H_C, DK_C) * (DK_C ** -0.5)
    k = c_k.reshape(B, T, H_C, DK_C)
    v = c_v.reshape(B, T, H_C, DV_C)
    o, S_new = gla(q, k, v, lg.reshape(B, T, H_C, DK_C), gla_S0)
    o = rms_norm(o, p["gla_norm_g"][l]).reshape(B, T, D_C).astype(u.dtype)
    y_c = jax.nn.silu(c_g) * o
    y = jnp.concatenate([y_a, y_b, y_c], axis=-1) @ p["w_o"][l]
    return y, conv_a_new, v_rows, S_new


def conv_ffn(u, past, l, p):
    up = u @ p["ffn_w_up"][l]
    up_c, new_past = causal_dwconv(up, past, p["ffn_conv_w"][l])
    a, g = jnp.split(up_c + p["ffn_conv_b"][l], 2, axis=-1)
    return (jax.nn.silu(g) * a) @ p["ffn_w_down"][l], new_past


def run_trunk(x, c, conv_a_st, gla_st, ffn_st, p):
    h = layer_norm(x, p["ln_in_g"], p["ln_in_b"])
    cs = jax.nn.silu(c)
    conv_a_out, chunk_v_out, gla_out, ffn_out = [], [], [], []
    for l in range(DEPTH):
        mod = (cs @ p["w_ada"][l] + p["b_ada"][l])[:, None, :]
        sh1, sc1, g1, sh2, sc2, g2 = jnp.split(mod, 6, axis=-1)
        y, ca, cv, S = token_mixers(h * (1 + sc1) + sh1, conv_a_st[l], gla_st[l], l, p)
        h = layer_norm(DEEPNORM_ALPHA * h + g1 * y, p["ln1_g"][l], p["ln1_b"][l])
        f, fs = conv_ffn(h * (1 + sc2) + sh2, ffn_st[l], l, p)
        h = layer_norm(DEEPNORM_ALPHA * h + g2 * f, p["ln2_g"][l], p["ln2_b"][l])
        conv_a_out.append(ca)
        chunk_v_out.append(cv)
        gla_out.append(S)
        ffn_out.append(fs)
    return h, jnp.stack(conv_a_out), jnp.stack(chunk_v_out), jnp.stack(gla_out), jnp.stack(ffn_out)


def setup_inputs(seed: int = 0) -> dict:
    key = jax.random.key(seed)
    ks = iter(jax.random.split(key, 32))
    f32 = jnp.float32

    def nrm(shape, scale):
        return jax.random.normal(next(ks), shape, f32) * scale

    def gain(shape):
        return 1.0 + nrm(shape, 0.02)

    D = D_MODEL
    return {
        "x_prompt": nrm((BATCH, SEQ, D), 1.0),
        "x_sample": nrm((DEC_BATCH, DEC_SEQ, D), 1.0),
        "c_prompt": nrm((BATCH, D), 1.0),
        "c_sample": nrm((DEC_BATCH, D), 1.0),
        "state_conv_a": nrm((DEPTH, DEC_BATCH, CONV_W - 1, D_A), 1.0),
        "state_gla": nrm((DEPTH, DEC_BATCH, H_C, DK_C, DV_C), 0.3),
        "state_ffn_conv": nrm((DEPTH, DEC_BATCH, CONV_W - 1, 2 * D_FF), 1.0),
        "ln_in_g": gain((D,)),
        "ln_in_b": nrm((D,), 0.02),
        "w_ada": nrm((DEPTH, D, 6 * D), D ** -0.5),
        "b_ada": nrm((DEPTH, 6 * D), 0.02),
        "w_in": nrm((DEPTH, D, N_IN), D ** -0.5),
        "conv_a_w": nrm((DEPTH, CONV_W, D_A), CONV_W ** -0.5),
        "sgu_ln_g": gain((DEPTH, D_B)),
        "sgu_ln_b": nrm((DEPTH, D_B), 0.02),
        "sgu_w": nrm((DEPTH, H_B, CHUNK_B, CHUNK_B), CHUNK_B ** -0.5),
        "sgu_b": gain((DEPTH, H_B, CHUNK_B)),
        "gla_w_a2": nrm((DEPTH, GATE_RANK, H_C * DK_C), GATE_RANK ** -0.5),
        "gla_b_a": nrm((DEPTH, H_C * DK_C), 0.02),
        "gla_norm_g": gain((DEPTH, DV_C)),
        "w_o": nrm((DEPTH, D, D), DEEPNORM_BETA * D ** -0.5),
        "ln1_g": gain((DEPTH, D)),
        "ln1_b": nrm((DEPTH, D), 0.02),
        "ffn_w_up": nrm((DEPTH, D, 2 * D_FF), D ** -0.5),
        "ffn_conv_w": nrm((DEPTH, CONV_W, 2 * D_FF), CONV_W ** -0.5),
        "ffn_conv_b": nrm((DEPTH, 2 * D_FF), 0.02),
        "ffn_w_down": nrm((DEPTH, D_FF, D), DEEPNORM_BETA * D_FF ** -0.5),
        "ln2_g": gain((DEPTH, D)),
        "ln2_b": nrm((DEPTH, D), 0.02),
    }


def reference(x_prompt, x_sample, c_prompt, c_sample, state_conv_a, state_gla, state_ffn_conv,
              ln_in_g, ln_in_b, w_ada, b_ada, w_in, conv_a_w, sgu_ln_g, sgu_ln_b, sgu_w, sgu_b,
              gla_w_a2, gla_b_a, gla_norm_g, w_o, ln1_g, ln1_b, ffn_w_up, ffn_conv_w, ffn_conv_b,
              ffn_w_down, ln2_g, ln2_b):
    p = {"ln_in_g": ln_in_g, "ln_in_b": ln_in_b, "w_ada": w_ada, "b_ada": b_ada, "w_in": w_in,
         "conv_a_w": conv_a_w, "sgu_ln_g": sgu_ln_g, "sgu_ln_b": sgu_ln_b, "sgu_w": sgu_w,
         "sgu_b": sgu_b, "gla_w_a2": gla_w_a2, "gla_b_a": gla_b_a, "gla_norm_g": gla_norm_g,
         "w_o": w_o, "ln1_g": ln1_g, "ln1_b": ln1_b, "ffn_w_up": ffn_w_up,
         "ffn_conv_w": ffn_conv_w, "ffn_conv_b": ffn_conv_b, "ffn_w_down": ffn_w_down,
         "ln2_g": ln2_g, "ln2_b": ln2_b}
    dt = x_prompt.dtype
    zero_conv_a = jnp.zeros((DEPTH, BATCH, CONV_W - 1, D_A), dt)
    zero_gla = jnp.zeros((DEPTH, BATCH, H_C, DK_C, DV_C), jnp.float32)
    zero_ffn = jnp.zeros((DEPTH, BATCH, CONV_W - 1, 2 * D_FF), dt)
    y_prompt, ca_p, cv_p, gla_p, ffn_p = run_trunk(x_prompt, c_prompt, zero_conv_a, zero_gla, zero_ffn, p)
    y_sample, ca_s, cv_s, gla_s, ffn_s = run_trunk(x_sample, c_sample, state_conv_a, state_gla,
                                                   state_ffn_conv, p)
    return (y_prompt, y_sample, ca_p, ca_s, cv_p, cv_s, gla_p, gla_s, ffn_p, ffn_s)
```

```python
import functools
import math

import jax
import jax.numpy as jnp
from jax import lax
from jax.experimental import pallas as pl
from jax.experimental.pallas import tpu as pltpu

D_MODEL = 1024
BATCH = 8
SEQ = 2048
DEPTH = 4
DEC_BATCH = 128
DEC_SEQ = 8
D_A = 256
D_B = 256
H_B = 4
DH_B = 64
D_C = 512
H_C = 4
DV_C = 128
DK_C = 64
GATE_RANK = 16
GATE_TAU = 16.0
CHUNK = 128
D_FF = 2816
LN_EPS = 1e-5
ALPHA = (2.0 * DEPTH) ** 0.25
N_IN = 2832
N_IN_PAD = 2944
RANK_PAD = 128

O_AIN, O_AB, O_AC, O_SU, O_SV, O_Q, O_K, O_V, O_G, O_R = (
    0, 256, 512, 768, 1024, 1280, 1536, 1792, 2304, 2816)

SUBLANES = 8
VMEM_LIMIT = 56 * 1024 * 1024

TS_MIX = 512
TS_FFN = 512
TB_S = 32
GROUP_SEQ = CHUNK // DEC_SEQ

F32 = jnp.float32
BF16 = jnp.bfloat16


def _dot(a, b):
    return jnp.dot(a, b, preferred_element_type=F32)


def _dot_nt(a, b):
    return lax.dot_general(a, b, (((1,), (1,)), ((), ())), preferred_element_type=F32)


def _dot_tn(a, b):
    return lax.dot_general(a, b, (((0,), (0,)), ((), ())), preferred_element_type=F32)


def _layer_norm(x, g, b):
    mu = jnp.mean(x, axis=-1, keepdims=True)
    xc = x - mu
    var = jnp.mean(xc * xc, axis=-1, keepdims=True)
    return xc * lax.rsqrt(var + LN_EPS) * g + b


def _gelu(x):
    c = math.sqrt(2.0 / math.pi)
    return 0.5 * x * (1.0 + jnp.tanh(c * (x + 0.044715 * (x * x * x))))


def _silu(x):
    return x * jax.nn.sigmoid(x)


def _log_sigmoid(x):
    return jnp.minimum(x, 0.0) - jnp.log1p(jnp.exp(-jnp.abs(x)))


def _split3(x):
    hi = x.astype(BF16)
    r1 = x - hi.astype(F32)
    mid = r1.astype(BF16)
    lo = (r1 - mid.astype(F32)).astype(BF16)
    return hi, mid, lo


def _dot_exact_lhs(m_bf16, x):
    hi, mid, lo = _split3(x)
    return _dot(m_bf16, hi) + _dot(m_bf16, mid) + _dot(m_bf16, lo)


def _head_masks(width, n_heads):
    lane = lax.broadcasted_iota(jnp.int32, (1, width), 1)
    per = width // n_heads
    return [(lane >= h * per) & (lane < (h + 1) * per) for h in range(n_heads)]


def _ada_kernel(c_ref, w_ref, b_ref, o_ref):
    cs = _silu(c_ref[...]).astype(BF16)
    o_ref[...] = _dot(cs, w_ref[...].astype(BF16)) + b_ref[...]


def _ada_call(c_all, w_ada, b_ada):
    n_rows = c_all.shape[0]
    tn = 1536
    return pl.pallas_call(
        _ada_kernel,
        grid=(DEPTH, 6 * D_MODEL // tn),
        in_specs=[
            pl.BlockSpec((n_rows, D_MODEL), lambda l, j: (0, 0)),
            pl.BlockSpec((None, D_MODEL, tn), lambda l, j: (l, 0, j)),
            pl.BlockSpec((None, 1, tn), lambda l, j: (l, 0, j)),
        ],
        out_specs=pl.BlockSpec((None, n_rows, tn), lambda l, j: (l, 0, j)),
        out_shape=jax.ShapeDtypeStruct((DEPTH, n_rows, 6 * D_MODEL), F32),
        compiler_params=pltpu.CompilerParams(
            dimension_semantics=("arbitrary", "arbitrary"), vmem_limit_bytes=VMEM_LIMIT),
        name="ada_mod",
    )(c_all, w_ada, b_ada.reshape(DEPTH, 1, 6 * D_MODEL))


def _ln_kernel(x_ref, g_ref, b_ref, o_ref):
    o_ref[...] = _layer_norm(x_ref[...], g_ref[...], b_ref[...])


def _ln_call(x2d, g, b):
    rows = x2d.shape[0]
    tr = 512
    return pl.pallas_call(
        _ln_kernel,
        grid=(rows // tr,),
        in_specs=[
            pl.BlockSpec((tr, D_MODEL), lambda i: (i, 0)),
            pl.BlockSpec((1, D_MODEL), lambda i: (0, 0)),
            pl.BlockSpec((1, D_MODEL), lambda i: (0, 0)),
        ],
        out_specs=pl.BlockSpec((tr, D_MODEL), lambda i: (i, 0)),
        out_shape=jax.ShapeDtypeStruct((rows, D_MODEL), F32),
        compiler_params=pltpu.CompilerParams(dimension_semantics=("arbitrary",)),
        name="ln_in",
    )(x2d, g.reshape(1, D_MODEL), b.reshape(1, D_MODEL))


def _sgu_chunk(z_ref, rows, y_ref, wcat, sbias, ln_g, ln_b, masks_b):
    su = _gelu(z_ref[rows, O_SU:O_SU + D_B])
    sv = _layer_norm(_gelu(z_ref[rows, O_SV:O_SV + D_B]), ln_g, ln_b)
    svb = sv.astype(BF16)
    zero = jnp.zeros_like(svb)
    svm = jnp.concatenate([jnp.where(m, svb, zero) for m in masks_b], axis=0)
    mixed = _dot(wcat, svm) + sbias
    y_ref[rows, D_A:D_A + D_B] = (su * mixed).astype(BF16)
    return sv


def _gla_gate(z_ref, rows, wa2, ba):
    cr = z_ref[rows, O_R:O_R + RANK_PAD].astype(BF16)
    return _log_sigmoid(_dot(cr, wa2) + ba) * (1.0 / GATE_TAU)


def _gla_head_out(z_ref, rows, y_ref, hh, o, gn):
    ms = jnp.mean(o * o, axis=-1, keepdims=True)
    on = o * lax.rsqrt(ms + LN_EPS) * gn
    g = z_ref[rows, O_G + hh * DV_C:O_G + (hh + 1) * DV_C]
    y_ref[rows, D_A + D_B + hh * DV_C:D_A + D_B + (hh + 1) * DV_C] = (_silu(g) * on).astype(BF16)


def _mixer_prompt_kernel(h_ref, mod_ref, win_ref, cw_ref, sg_ref, sb_ref, wcat_ref, sbias_ref,
                         wa2_ref, ba_ref, gn_ref, wo_ref, lng_ref, lnb_ref,
                         hout_ref, ca_ref, cv_ref, gla_ref,
                         z_ref, xa_ref, y_ref, st_ref, *, ts, n_tiles):
    j = pl.program_id(1)

    @pl.when(j == 0)
    def _():
        xa_ref[0:SUBLANES, :] = jnp.zeros((SUBLANES, D_A), F32)
        st_ref[...] = jnp.zeros_like(st_ref)

    h = h_ref[...]
    sh1, sc1, g1 = mod_ref[0:1, :], mod_ref[1:2, :], mod_ref[2:3, :]
    u = (h * (1.0 + sc1) + sh1).astype(BF16)
    z_ref[...] = _dot(u, win_ref[...])

    cw = cw_ref[...]
    wcat = wcat_ref[...].astype(BF16)
    sbias = sbias_ref[...]
    wa2 = wa2_ref[...]
    ba = ba_ref[...]
    gn = gn_ref[...]
    sg, sb = sg_ref[...], sb_ref[...]
    masks_b = _head_masks(D_B, H_B)
    masks_k = _head_masks(H_C * DK_C, H_C)
    row = lax.broadcasted_iota(jnp.int32, (CHUNK, CHUNK), 0)
    col = lax.broadcasted_iota(jnp.int32, (CHUNK, CHUNK), 1)
    causal = col <= row
    tri = causal.astype(BF16)

    def chunk_body(c, carry):
        r0 = pl.multiple_of(c * CHUNK, CHUNK)
        rows = pl.ds(r0, CHUNK)
        x = z_ref[rows, O_AC:O_AC + D_A] * z_ref[rows, O_AIN:O_AIN + D_A]
        xa_ref[pl.ds(pl.multiple_of(r0 + SUBLANES, SUBLANES), CHUNK), :] = x
        win = xa_ref[pl.ds(r0, CHUNK + SUBLANES), :]
        conv = (cw[0:1, :] * win[SUBLANES - 2:SUBLANES - 2 + CHUNK]
                + cw[1:2, :] * win[SUBLANES - 1:SUBLANES - 1 + CHUNK]
                + cw[2:3, :] * x)
        y_ref[rows, 0:D_A] = (z_ref[rows, O_AB:O_AB + D_A] * conv).astype(BF16)
        cv_ref[...] = _sgu_chunk(z_ref, rows, y_ref, wcat, sbias, sg, sb, masks_b)
        lg = _gla_gate(z_ref, rows, wa2, ba)
        b = _dot_exact_lhs(tri, lg)
        b_last = b[CHUNK - 1:CHUNK, :]
        q = z_ref[rows, O_Q:O_Q + H_C * DK_C]
        k = z_ref[rows, O_K:O_K + H_C * DK_C]
        qt = (q * (DK_C ** -0.5)) * jnp.exp(b)
        kt = (k * jnp.exp(-b)).astype(BF16)
        kh = k * jnp.exp(b_last - b)
        st = st_ref[...]
        stb = st.astype(BF16)
        st_new = st * jnp.exp(b_last)
        zq = jnp.zeros_like(qt)
        for hh in range(H_C):
            qm = jnp.where(masks_k[hh], qt, zq).astype(BF16)
            a = jnp.where(causal, _dot_nt(qm, kt), 0.0).astype(BF16)
            vb = z_ref[rows, O_V + hh * DV_C:O_V + (hh + 1) * DV_C].astype(BF16)
            o = _dot(a, vb) + _dot_nt(qm, stb)
            _gla_head_out(z_ref, rows, y_ref, hh, o, gn)
            khm = jnp.where(masks_k[hh], kh, zq).astype(BF16)
            st_new = st_new + _dot_tn(vb, khm)
        st_ref[...] = st_new
        return carry

    lax.fori_loop(0, ts // CHUNK, chunk_body, 0)

    y = _dot(y_ref[...], wo_ref[...])
    hout_ref[...] = _layer_norm(ALPHA * h + g1 * y, lng_ref[...], lnb_ref[...])
    ca_ref[...] = xa_ref[ts + SUBLANES - 2:ts + SUBLANES, :]
    xa_ref[0:SUBLANES, :] = xa_ref[ts:ts + SUBLANES, :]

    @pl.when(j == n_tiles - 1)
    def _():
        gla_ref[...] = st_ref[...].T


def _const_spec(shape, n_grid):
    zeros = (0,) * len(shape)
    if n_grid == 1:
        return pl.BlockSpec(shape, lambda i: zeros, pipeline_mode=pl.Buffered(1))
    return pl.BlockSpec(shape, lambda b, j: zeros, pipeline_mode=pl.Buffered(1))


def _layer_spec(shape, l, n_grid):
    idx = (l,) + (0,) * len(shape)
    if n_grid == 1:
        return pl.BlockSpec((None,) + shape, lambda i: idx, pipeline_mode=pl.Buffered(1))
    return pl.BlockSpec((None,) + shape, lambda b, j: idx, pipeline_mode=pl.Buffered(1))


def _mixer_prompt_call(l, h, mod, p):
    ts = TS_MIX
    n_tiles = SEQ // ts
    ls = functools.partial(_layer_spec, l=l, n_grid=2)
    kern = functools.partial(_mixer_prompt_kernel, ts=ts, n_tiles=n_tiles)
    return pl.pallas_call(
        kern,
        grid=(BATCH, n_tiles),
        in_specs=[
            pl.BlockSpec((ts, D_MODEL), lambda b, j: (b * n_tiles + j, 0)),
            pl.BlockSpec((None, None, 6, D_MODEL), lambda b, j: (l, b, 0, 0)),
            ls((D_MODEL, N_IN_PAD)),
            ls((3, D_A)),
            ls((1, D_B)),
            ls((1, D_B)),
            ls((CHUNK, H_B * CHUNK)),
            ls((CHUNK, D_B)),
            ls((RANK_PAD, H_C * DK_C)),
            ls((1, H_C * DK_C)),
            ls((1, DV_C)),
            ls((D_MODEL, D_MODEL)),
            ls((1, D_MODEL)),
            ls((1, D_MODEL)),
        ],
        out_specs=[
            pl.BlockSpec((ts, D_MODEL), lambda b, j: (b * n_tiles + j, 0)),
            pl.BlockSpec((None, 2, D_A), lambda b, j: (b, 0, 0)),
            pl.BlockSpec((None, CHUNK, D_B), lambda b, j: (b, 0, 0)),
            pl.BlockSpec((None, H_C * DK_C, DV_C), lambda b, j: (b, 0, 0)),
        ],
        out_shape=[
            jax.ShapeDtypeStruct((BATCH * SEQ, D_MODEL), F32),
            jax.ShapeDtypeStruct((BATCH, 2, D_A), F32),
            jax.ShapeDtypeStruct((BATCH, CHUNK, D_B), F32),
            jax.ShapeDtypeStruct((BATCH, H_C * DK_C, DV_C), F32),
        ],
        scratch_shapes=[
            pltpu.VMEM((ts, N_IN_PAD), F32),
            pltpu.VMEM((ts + SUBLANES, D_A), F32),
            pltpu.VMEM((ts, D_MODEL), BF16),
            pltpu.VMEM((DV_C, H_C * DK_C), F32),
        ],
        compiler_params=pltpu.CompilerParams(
            dimension_semantics=("arbitrary", "arbitrary"), vmem_limit_bytes=VMEM_LIMIT),
        name="mixer_prompt",
    )(h, mod, p["w_in"], p["conv_a_w"], p["sgu_ln_g"], p["sgu_ln_b"], p["wcat_p"], p["sbias_p"],
      p["w_a2"], p["b_a"], p["gla_norm_g"], p["w_o"], p["ln1_g"], p["ln1_b"])


def _expand_rows(x3):
    n, _, c = x3.shape
    return jnp.broadcast_to(x3, (n, DEC_SEQ, c)).reshape(n * DEC_SEQ, c)


def _sample_conv(x, past, cw):
    p0 = _expand_rows(past[:, 0:1, :])
    p1 = _expand_rows(past[:, 1:2, :])
    t_idx = lax.broadcasted_iota(jnp.int32, x.shape, 0) & (DEC_SEQ - 1)
    xm1 = jnp.where(t_idx >= 1, pltpu.roll(x, 1, 0), p1)
    xm2 = jnp.where(t_idx >= 2, pltpu.roll(x, 2, 0), jnp.where(t_idx == 1, p1, p0))
    return cw[0:1, :] * xm2 + cw[1:2, :] * xm1 + cw[2:3, :] * x


def _mixer_sample_kernel(h_ref, mod_ref, past_ref, s0_ref, win_ref, cw_ref, sg_ref, sb_ref,
                         wcat_ref, sbias_ref, wa2_ref, ba_ref, gn_ref, wo_ref, lng_ref, lnb_ref,
                         hout_ref, ca_ref, cv_ref, gla_ref,
                         z_ref, y_ref, qt_ref, kh_ref, eb_ref, oi_ref, *, tb):
    rr = tb * DEC_SEQ
    h = h_ref[...].reshape(rr, D_MODEL)
    sh1 = _expand_rows(mod_ref[:, 0:1, :])
    sc1 = _expand_rows(mod_ref[:, 1:2, :])
    g1 = _expand_rows(mod_ref[:, 2:3, :])
    u = (h * (1.0 + sc1) + sh1).astype(BF16)
    z_ref[...] = _dot(u, win_ref[...])

    x = z_ref[:, O_AC:O_AC + D_A] * z_ref[:, O_AIN:O_AIN + D_A]
    conv = _sample_conv(x, past_ref[...], cw_ref[...])
    y_ref[:, 0:D_A] = (z_ref[:, O_AB:O_AB + D_A] * conv).astype(BF16)
    ca_ref[...] = x.reshape(tb, DEC_SEQ, D_A)[:, DEC_SEQ - 2:DEC_SEQ, :]

    wcat = wcat_ref[...].astype(BF16)
    sbias = sbias_ref[...]
    wa2 = wa2_ref[...]
    ba = ba_ref[...]
    gn = gn_ref[...]
    sg, sb = sg_ref[...], sb_ref[...]
    masks_b = _head_masks(D_B, H_B)
    masks_k = _head_masks(H_C * DK_C, H_C)
    row = lax.broadcasted_iota(jnp.int32, (CHUNK, CHUNK), 0)
    col = lax.broadcasted_iota(jnp.int32, (CHUNK, CHUNK), 1)
    same_seq = (row // DEC_SEQ) == (col // DEC_SEQ)
    causal = same_seq & (col <= row)
    tri = causal.astype(BF16)
    blk = same_seq.astype(BF16)

    for gi in range(rr // CHUNK):
        rows = pl.ds(gi * CHUNK, CHUNK)
        sv = _sgu_chunk(z_ref, rows, y_ref, wcat, sbias, sg, sb, masks_b)
        cv_ref[gi * GROUP_SEQ:(gi + 1) * GROUP_SEQ] = sv.reshape(GROUP_SEQ, DEC_SEQ, D_B)
        lg = _gla_gate(z_ref, rows, wa2, ba)
        b = _dot_exact_lhs(tri, lg)
        b_tot = _dot_exact_lhs(blk, lg)
        q = z_ref[rows, O_Q:O_Q + H_C * DK_C]
        k = z_ref[rows, O_K:O_K + H_C * DK_C]
        qt = (q * (DK_C ** -0.5)) * jnp.exp(b)
        kt = (k * jnp.exp(-b)).astype(BF16)
        qt_ref[rows, :] = qt
        kh_ref[rows, :] = k * jnp.exp(b_tot - b)
        eb_ref[rows, :] = jnp.exp(b_tot)
        zq = jnp.zeros_like(qt)
        for hh in range(H_C):
            qm = jnp.where(masks_k[hh], qt, zq).astype(BF16)
            a = jnp.where(causal, _dot_nt(qm, kt), 0.0).astype(BF16)
            vb = z_ref[rows, O_V + hh * DV_C:O_V + (hh + 1) * DV_C].astype(BF16)
            oi_ref[rows, hh * DV_C:(hh + 1) * DV_C] = _dot(a, vb)

    ones16 = jnp.ones((2 * SUBLANES, DV_C), BF16)
    sub = lax.broadcasted_iota(jnp.int32, (SUBLANES, H_C * DK_C), 0)

    def seq_body(i, carry):
        r0 = pl.multiple_of(i * DEC_SEQ, DEC_SEQ)
        rows = pl.ds(r0, DEC_SEQ)
        s0 = s0_ref[i]
        qt = qt_ref[rows, :]
        kh = kh_ref[rows, :]
        zq = jnp.zeros_like(qt)
        q_exp = jnp.concatenate([jnp.where(m, qt, zq) for m in masks_k], axis=0).astype(BF16)
        k_exp = jnp.concatenate([jnp.where(m, kh, zq) for m in masks_k], axis=0).astype(BF16)
        o_int = _dot(q_exp, s0.astype(BF16))
        v = z_ref[rows, O_V:O_V + D_C]
        v_exp = jnp.concatenate([v[:, hh * DV_C:(hh + 1) * DV_C] for hh in range(H_C)],
                                axis=0).astype(BF16)
        eb = eb_ref[rows, :]
        hi, mid, lo = _split3(eb)
        e3 = jnp.where(sub == 0, hi.astype(F32),
                       jnp.where(sub == 1, mid.astype(F32),
                                 jnp.where(sub == 2, lo.astype(F32), jnp.zeros_like(eb))))
        e3 = jnp.concatenate([e3, jnp.zeros_like(eb)], axis=0).astype(BF16)
        decay = _dot_tn(e3, ones16)
        gla_ref[i] = decay * s0 + _dot_tn(k_exp, v_exp)
        for hh in range(H_C):
            oi_ref[rows, hh * DV_C:(hh + 1) * DV_C] += o_int[hh * DEC_SEQ:(hh + 1) * DEC_SEQ, :]
        return carry

    lax.fori_loop(0, tb, seq_body, 0)

    for gi in range(rr // CHUNK):
        rows = pl.ds(gi * CHUNK, CHUNK)
        for hh in range(H_C):
            _gla_head_out(z_ref, rows, y_ref, hh, oi_ref[rows, hh * DV_C:(hh + 1) * DV_C], gn)

    y = _dot(y_ref[...], wo_ref[...])
    hn = _layer_norm(ALPHA * h + g1 * y, lng_ref[...], lnb_ref[...])
    hout_ref[...] = hn.reshape(tb, DEC_SEQ, D_MODEL)


def _mixer_sample_call(l, h3, mod, past, s0, p):
    tb = TB_S
    ls = functools.partial(_layer_spec, l=l, n_grid=1)
    kern = functools.partial(_mixer_sample_kernel, tb=tb)
    rr = tb * DEC_SEQ
    return pl.pallas_call(
        kern,
        grid=(DEC_BATCH // tb,),
        in_specs=[
            pl.BlockSpec((tb, DEC_SEQ, D_MODEL), lambda i: (i, 0, 0)),
            pl.BlockSpec((None, tb, 6, D_MODEL), lambda i: (l, i, 0, 0)),
            pl.BlockSpec((None, tb, 2, D_A), lambda i: (l, i, 0, 0)),
            pl.BlockSpec((None, tb, H_C * DK_C, DV_C), lambda i: (l, i, 0, 0)),
            ls((D_MODEL, N_IN_PAD)),
            ls((3, D_A)),
            ls((1, D_B)),
            ls((1, D_B)),
            ls((CHUNK, H_B * CHUNK)),
            ls((CHUNK, D_B)),
            ls((RANK_PAD, H_C * DK_C)),
            ls((1, H_C * DK_C)),
            ls((1, DV_C)),
            ls((D_MODEL, D_MODEL)),
            ls((1, D_MODEL)),
            ls((1, D_MODEL)),
        ],
        out_specs=[
            pl.BlockSpec((tb, DEC_SEQ, D_MODEL), lambda i: (i, 0, 0)),
            pl.BlockSpec((tb, 2, D_A), lambda i: (i, 0, 0)),
            pl.BlockSpec((tb, DEC_SEQ, D_B), lambda i: (i, 0, 0)),
            pl.BlockSpec((tb, H_C * DK_C, DV_C), lambda i: (i, 0, 0)),
        ],
        out_shape=[
            jax.ShapeDtypeStruct((DEC_BATCH, DEC_SEQ, D_MODEL), F32),
            jax.ShapeDtypeStruct((DEC_BATCH, 2, D_A), F32),
            jax.ShapeDtypeStruct((DEC_BATCH, DEC_SEQ, D_B), F32),
            jax.ShapeDtypeStruct((DEC_BATCH, H_C * DK_C, DV_C), F32),
        ],
        scratch_shapes=[
            pltpu.VMEM((rr, N_IN_PAD), F32),
            pltpu.VMEM((rr, D_MODEL), BF16),
            pltpu.VMEM((rr, H_C * DK_C), F32),
            pltpu.VMEM((rr, H_C * DK_C), F32),
            pltpu.VMEM((rr, H_C * DK_C), F32),
            pltpu.VMEM((rr, D_C), F32),
        ],
        compiler_params=pltpu.CompilerParams(
            dimension_semantics=("arbitrary",), vmem_limit_bytes=VMEM_LIMIT),
        name="mixer_sample",
    )(h3, mod, past, s0, p["w_in"], p["conv_a_w"], p["sgu_ln_g"], p["sgu_ln_b"], p["wcat_s"],
      p["sbias_s"], p["w_a2"], p["b_a"], p["gla_norm_g"], p["w_o"], p["ln1_g"], p["ln1_b"])


FFN_RB = 32
FFN_CB = 256


def _ffn_tail(h, g2, act_ref, wdn_ref, lng_ref, lnb_ref):
    f = _dot(act_ref[...], wdn_ref[...])
    return _layer_norm(ALPHA * h + g2 * f, lng_ref[...], lnb_ref[...])


def _ffn_prompt_kernel(h_ref, mod_ref, wup_ref, cw_ref, cb_ref, wdn_ref, lng_ref, lnb_ref,
                       hout_ref, st_ref, xp_ref, act_ref, *, ts):
    j = pl.program_id(1)

    @pl.when(j == 0)
    def _():
        xp_ref[0:SUBLANES, :] = jnp.zeros((SUBLANES, 2 * D_FF), F32)

    h = h_ref[...]
    sh2, sc2, g2 = mod_ref[3:4, :], mod_ref[4:5, :], mod_ref[5:6, :]
    u = (h * (1.0 + sc2) + sh2).astype(BF16)
    xp_ref[SUBLANES:SUBLANES + ts, :] = _dot(u, wup_ref[...])

    def conv_cols(r0, c0):
        cols = slice(c0, c0 + FFN_CB)
        win = xp_ref[pl.ds(r0, FFN_RB + SUBLANES), cols]
        return (cw_ref[0:1, cols] * win[SUBLANES - 2:SUBLANES - 2 + FFN_RB]
                + cw_ref[1:2, cols] * win[SUBLANES - 1:SUBLANES - 1 + FFN_RB]
                + cw_ref[2:3, cols] * win[SUBLANES:SUBLANES + FFN_RB]
                + cb_ref[0:1, cols])

    def row_body(i, carry):
        r0 = pl.multiple_of(i * FFN_RB, FFN_RB)
        for cb in range(D_FF // FFN_CB):
            a = conv_cols(r0, cb * FFN_CB)
            g = conv_cols(r0, D_FF + cb * FFN_CB)
            act_ref[pl.ds(r0, FFN_RB), cb * FFN_CB:(cb + 1) * FFN_CB] = (_silu(g) * a).astype(BF16)
        return carry

    lax.fori_loop(0, ts // FFN_RB, row_body, 0)

    hout_ref[...] = _ffn_tail(h, g2, act_ref, wdn_ref, lng_ref, lnb_ref)
    st_ref[...] = xp_ref[ts + SUBLANES - 2:ts + SUBLANES, :]
    xp_ref[0:SUBLANES, :] = xp_ref[ts:ts + SUBLANES, :]


def _ffn_prompt_call(l, h, mod, p):
    ts = TS_FFN
    n_tiles = SEQ // ts
    ls = functools.partial(_layer_spec, l=l, n_grid=2)
    kern = functools.partial(_ffn_prompt_kernel, ts=ts)
    return pl.pallas_call(
        kern,
        grid=(BATCH, n_tiles),
        in_specs=[
            pl.BlockSpec((ts, D_MODEL), lambda b, j: (b * n_tiles + j, 0)),
            pl.BlockSpec((None, None, 6, D_MODEL), lambda b, j: (l, b, 0, 0)),
            ls((D_MODEL, 2 * D_FF)),
            ls((3, 2 * D_FF)),
            ls((1, 2 * D_FF)),
            ls((D_FF, D_MODEL)),
            ls((1, D_MODEL)),
            ls((1, D_MODEL)),
        ],
        out_specs=[
            pl.BlockSpec((ts, D_MODEL), lambda b, j: (b * n_tiles + j, 0)),
            pl.BlockSpec((None, 2, 2 * D_FF), lambda b, j: (b, 0, 0)),
        ],
        out_shape=[
            jax.ShapeDtypeStruct((BATCH * SEQ, D_MODEL), F32),
            jax.ShapeDtypeStruct((BATCH, 2, 2 * D_FF), F32),
        ],
        scratch_shapes=[
            pltpu.VMEM((ts + SUBLANES, 2 * D_FF), F32),
            pltpu.VMEM((ts, D_FF), BF16),
        ],
        compiler_params=pltpu.CompilerParams(
            dimension_semantics=("arbitrary", "arbitrary"), vmem_limit_bytes=VMEM_LIMIT),
        name="ffn_prompt",
    )(h, mod, p["w_up"], p["ffn_conv_w"], p["ffn_conv_b"], p["w_down"], p["ln2_g"], p["ln2_b"])


def _ffn_sample_kernel(h_ref, mod_ref, past_ref, wup_ref, cw_ref, cb_ref, wdn_ref, lng_ref, lnb_ref,
                       hout_ref, st_ref, up_ref, act_ref, *, tb):
    rr = tb * DEC_SEQ
    h = h_ref[...].reshape(rr, D_MODEL)
    sh2 = _expand_rows(mod_ref[:, 3:4, :])
    sc2 = _expand_rows(mod_ref[:, 4:5, :])
    g2 = _expand_rows(mod_ref[:, 5:6, :])
    u = (h * (1.0 + sc2) + sh2).astype(BF16)
    up_ref[...] = _dot(u, wup_ref[...])
    n_seq = FFN_RB // DEC_SEQ

    def conv_cols(i, r0, c0):
        cols = slice(c0, c0 + FFN_CB)
        seqs = pl.ds(i * n_seq, n_seq)
        x = up_ref[pl.ds(r0, FFN_RB), cols]
        st_ref[seqs, :, cols] = x.reshape(n_seq, DEC_SEQ, FFN_CB)[:, DEC_SEQ - 2:DEC_SEQ, :]
        return _sample_conv(x, past_ref[seqs, :, cols], cw_ref[:, cols]) + cb_ref[0:1, cols]

    def row_body(i, carry):
        r0 = pl.multiple_of(i * FFN_RB, FFN_RB)
        for cb in range(D_FF // FFN_CB):
            a = conv_cols(i, r0, cb * FFN_CB)
            g = conv_cols(i, r0, D_FF + cb * FFN_CB)
            act_ref[pl.ds(r0, FFN_RB), cb * FFN_CB:(cb + 1) * FFN_CB] = (_silu(g) * a).astype(BF16)
        return carry

    lax.fori_loop(0, rr // FFN_RB, row_body, 0)

    hn = _ffn_tail(h, g2, act_ref, wdn_ref, lng_ref, lnb_ref)
    hout_ref[...] = hn.reshape(tb, DEC_SEQ, D_MODEL)


def _ffn_sample_call(l, h3, mod, past, p):
    tb = TB_S
    rr = tb * DEC_SEQ
    ls = functools.partial(_layer_spec, l=l, n_grid=1)
    kern = functools.partial(_ffn_sample_kernel, tb=tb)
    return pl.pallas_call(
        kern,
        grid=(DEC_BATCH // tb,),
        in_specs=[
            pl.BlockSpec((tb, DEC_SEQ, D_MODEL), lambda i: (i, 0, 0)),
            pl.BlockSpec((None, tb, 6, D_MODEL), lambda i: (l, i, 0, 0)),
            pl.BlockSpec((None, tb, 2, 2 * D_FF), lambda i: (l, i, 0, 0)),
            ls((D_MODEL, 2 * D_FF)),
            ls((3, 2 * D_FF)),
            ls((1, 2 * D_FF)),
            ls((D_FF, D_MODEL)),
            ls((1, D_MODEL)),
            ls((1, D_MODEL)),
        ],
        out_specs=[
            pl.BlockSpec((tb, DEC_SEQ, D_MODEL), lambda i: (i, 0, 0)),
            pl.BlockSpec((tb, 2, 2 * D_FF), lambda i: (i, 0, 0)),
        ],
        out_shape=[
            jax.ShapeDtypeStruct((DEC_BATCH, DEC_SEQ, D_MODEL), F32),
            jax.ShapeDtypeStruct((DEC_BATCH, 2, 2 * D_FF), F32),
        ],
        scratch_shapes=[
            pltpu.VMEM((rr, 2 * D_FF), F32),
            pltpu.VMEM((rr, D_FF), BF16),
        ],
        compiler_params=pltpu.CompilerParams(
            dimension_semantics=("arbitrary",), vmem_limit_bytes=VMEM_LIMIT),
        name="ffn_sample",
    )(h3, mod, past, p["w_up"], p["ffn_conv_w"], p["ffn_conv_b"], p["w_down"], p["ln2_g"], p["ln2_b"])


def _prep_params(w_in, conv_a_w, sgu_ln_g, sgu_ln_b, sgu_w, sgu_b, gla_w_a2, gla_b_a, gla_norm_g,
                 w_o, ln1_g, ln1_b, ffn_w_up, ffn_conv_w, ffn_conv_b, ffn_w_down, ln2_g, ln2_b):
    tril = jnp.tril(jnp.ones((CHUNK, CHUNK), bool))
    w_p = jnp.where(tril[None, None], sgu_w, 0.0)
    wcat_p = jnp.transpose(w_p, (0, 2, 1, 3)).reshape(DEPTH, CHUNK, H_B * CHUNK)
    tril8 = jnp.tril(jnp.ones((DEC_SEQ, DEC_SEQ), bool))
    w8 = jnp.where(tril8[None, None], sgu_w[:, :, :DEC_SEQ, :DEC_SEQ], 0.0)
    eye = jnp.eye(GROUP_SEQ, dtype=F32)
    w_blk = jnp.einsum("ij,lhts->lhitjs", eye, w8).reshape(DEPTH, H_B, CHUNK, CHUNK)
    wcat_s = jnp.transpose(w_blk, (0, 2, 1, 3)).reshape(DEPTH, CHUNK, H_B * CHUNK)
    sbias_p = jnp.repeat(jnp.transpose(sgu_b, (0, 2, 1)), DH_B, axis=2)
    sbias_s = jnp.tile(sbias_p[:, :DEC_SEQ, :], (1, GROUP_SEQ, 1))
    return {
        "w_in": jnp.pad(w_in, ((0, 0), (0, 0), (0, N_IN_PAD - N_IN))).astype(BF16),
        "conv_a_w": conv_a_w,
        "sgu_ln_g": sgu_ln_g.reshape(DEPTH, 1, D_B),
        "sgu_ln_b": sgu_ln_b.reshape(DEPTH, 1, D_B),
        "wcat_p": wcat_p, "wcat_s": wcat_s, "sbias_p": sbias_p, "sbias_s": sbias_s,
        "w_a2": jnp.pad(gla_w_a2, ((0, 0), (0, RANK_PAD - GATE_RANK), (0, 0))).astype(BF16),
        "b_a": gla_b_a.reshape(DEPTH, 1, H_C * DK_C),
        "gla_norm_g": gla_norm_g.reshape(DEPTH, 1, DV_C),
        "w_o": w_o.astype(BF16),
        "ln1_g": ln1_g.reshape(DEPTH, 1, D_MODEL), "ln1_b": ln1_b.reshape(DEPTH, 1, D_MODEL),
        "w_up": ffn_w_up.astype(BF16),
        "ffn_conv_w": ffn_conv_w,
        "ffn_conv_b": ffn_conv_b.reshape(DEPTH, 1, 2 * D_FF),
        "w_down": ffn_w_down.astype(BF16),
        "ln2_g": ln2_g.reshape(DEPTH, 1, D_MODEL), "ln2_b": ln2_b.reshape(DEPTH, 1, D_MODEL),
    }


def kernel(x_prompt, x_sample, c_prompt, c_sample, state_conv_a, state_gla, state_ffn_conv,
           ln_in_g, ln_in_b, w_ada, b_ada, w_in, conv_a_w, sgu_ln_g, sgu_ln_b, sgu_w, sgu_b,
           gla_w_a2, gla_b_a, gla_norm_g, w_o, ln1_g, ln1_b, ffn_w_up, ffn_conv_w, ffn_conv_b,
           ffn_w_down, ln2_g, ln2_b):
    p = _prep_params(w_in, conv_a_w, sgu_ln_g, sgu_ln_b, sgu_w, sgu_b, gla_w_a2, gla_b_a,
                     gla_norm_g, w_o, ln1_g, ln1_b, ffn_w_up, ffn_conv_w, ffn_conv_b,
                     ffn_w_down, ln2_g, ln2_b)
    mod = _ada_call(jnp.concatenate([c_prompt, c_sample], axis=0), w_ada, b_ada)
    mod = mod.reshape(DEPTH, BATCH + DEC_BATCH, 6, D_MODEL)
    mod_p, mod_s = mod[:, :BATCH], mod[:, BATCH:]
    s0_all = state_gla.reshape(DEPTH, DEC_BATCH, H_C * DK_C, DV_C)

    h_all = _ln_call(jnp.concatenate([x_prompt.reshape(BATCH * SEQ, D_MODEL),
                                      x_sample.reshape(DEC_BATCH * DEC_SEQ, D_MODEL)], axis=0),
                     ln_in_g, ln_in_b)
    hp = h_all[:BATCH * SEQ]
    hs = h_all[BATCH * SEQ:].reshape(DEC_BATCH, DEC_SEQ, D_MODEL)

    outs_p = [[], [], [], []]
    outs_s = [[], [], [], []]
    for l in range(DEPTH):
        hp, ca, cv, gs = _mixer_prompt_call(l, hp, mod_p, p)
        hp, fs = _ffn_prompt_call(l, hp, mod_p, p)
        for acc, val in zip(outs_p, (ca, cv, gs, fs)):
            acc.append(val)
        hs, ca, cv, gs = _mixer_sample_call(l, hs, mod_s, state_conv_a, s0_all, p)
        hs, fs = _ffn_sample_call(l, hs, mod_s, state_ffn_conv, p)
        for acc, val in zip(outs_s, (ca, cv, gs, fs)):
            acc.append(val)

    ca_p, cv_p, gla_p, ffn_p = [jnp.stack(v) for v in outs_p]
    ca_s, cv_s, gla_s, ffn_s = [jnp.stack(v) for v in outs_s]
    return (hp.reshape(BATCH, SEQ, D_MODEL), hs,
            ca_p, ca_s, cv_p, cv_s,
            gla_p.reshape(DEPTH, BATCH, H_C, DK_C, DV_C),
            gla_s.reshape(DEPTH, DEC_BATCH, H_C, DK_C, DV_C),
            ffn_p, ffn_s)
```

```python
import functools
import math

import jax
import jax.numpy as jnp
from jax import lax
from jax.experimental import pallas as pl
from jax.experimental.pallas import tpu as pltpu

D_MODEL = 1024
BATCH = 8
SEQ = 2048
DEPTH = 4
DEC_BATCH = 128
DEC_SEQ = 8
D_A = 256
D_B = 256
H_B = 4
DH_B = 64
D_C = 512
H_C = 4
DV_C = 128
DK_C = 64
GATE_RANK = 16
GATE_TAU = 16.0
CHUNK = 128
D_FF = 2816
LN_EPS = 1e-5
ALPHA = (2.0 * DEPTH) ** 0.25
N_IN = 2832
N_IN_PAD = 2944
RANK_PAD = 128

O_AIN, O_AB, O_AC, O_SU, O_SV, O_Q, O_K, O_V, O_G, O_R = (
    0, 256, 512, 768, 1024, 1280, 1536, 1792, 2304, 2816)

SUBLANES = 8
VMEM_LIMIT = 56 * 1024 * 1024

TS_MIX = 512
MIX_RB = 256
TS_FFN = 512
TB_S = 32
GROUP_SEQ = CHUNK // DEC_SEQ

F32 = jnp.float32
BF16 = jnp.bfloat16


def _dot(a, b):
    return jnp.dot(a, b, preferred_element_type=F32)


def _dot_nt(a, b):
    return lax.dot_general(a, b, (((1,), (1,)), ((), ())), preferred_element_type=F32)


def _dot_tn(a, b):
    return lax.dot_general(a, b, (((0,), (0,)), ((), ())), preferred_element_type=F32)


def _layer_norm(x, g, b):
    mu = jnp.mean(x, axis=-1, keepdims=True)
    xc = x - mu
    var = jnp.mean(xc * xc, axis=-1, keepdims=True)
    return xc * lax.rsqrt(var + LN_EPS) * g + b


def _gelu(x):
    c = math.sqrt(2.0 / math.pi)
    return 0.5 * x * (1.0 + jnp.tanh(c * (x + 0.044715 * (x * x * x))))


def _silu(x):
    return x * jax.nn.sigmoid(x)


def _log_sigmoid(x):
    return jnp.minimum(x, 0.0) - jnp.log1p(jnp.exp(-jnp.abs(x)))


def _split3(x):
    hi = x.astype(BF16)
    r1 = x - hi.astype(F32)
    mid = r1.astype(BF16)
    lo = (r1 - mid.astype(F32)).astype(BF16)
    return hi, mid, lo


def _dot_exact_lhs(m_bf16, x):
    hi, mid, lo = _split3(x)
    return _dot(m_bf16, hi) + _dot(m_bf16, mid) + _dot(m_bf16, lo)


def _head_masks(width, n_heads):
    lane = lax.broadcasted_iota(jnp.int32, (1, width), 1)
    per = width // n_heads
    return [(lane >= h * per) & (lane < (h + 1) * per) for h in range(n_heads)]


def _ada_kernel(c_ref, w_ref, b_ref, o_ref):
    cs = _silu(c_ref[...]).astype(BF16)
    o_ref[...] = _dot(cs, w_ref[...].astype(BF16)) + b_ref[...]


def _ada_call(c_all, w_ada, b_ada):
    n_rows = c_all.shape[0]
    tn = 1536
    return pl.pallas_call(
        _ada_kernel,
        grid=(DEPTH, 6 * D_MODEL // tn),
        in_specs=[
            pl.BlockSpec((n_rows, D_MODEL), lambda l, j: (0, 0)),
            pl.BlockSpec((None, D_MODEL, tn), lambda l, j: (l, 0, j)),
            pl.BlockSpec((None, 1, tn), lambda l, j: (l, 0, j)),
        ],
        out_specs=pl.BlockSpec((None, n_rows, tn), lambda l, j: (l, 0, j)),
        out_shape=jax.ShapeDtypeStruct((DEPTH, n_rows, 6 * D_MODEL), F32),
        compiler_params=pltpu.CompilerParams(
            dimension_semantics=("arbitrary", "arbitrary"), vmem_limit_bytes=VMEM_LIMIT),
        name="ada_mod",
    )(c_all, w_ada, b_ada.reshape(DEPTH, 1, 6 * D_MODEL))


def _ln_kernel(x_ref, g_ref, b_ref, o_ref):
    o_ref[...] = _layer_norm(x_ref[...], g_ref[...], b_ref[...])


def _ln_call(x2d, g, b):
    rows = x2d.shape[0]
    tr = 512
    return pl.pallas_call(
        _ln_kernel,
        grid=(rows // tr,),
        in_specs=[
            pl.BlockSpec((tr, D_MODEL), lambda i: (i, 0)),
            pl.BlockSpec((1, D_MODEL), lambda i: (0, 0)),
            pl.BlockSpec((1, D_MODEL), lambda i: (0, 0)),
        ],
        out_specs=pl.BlockSpec((tr, D_MODEL), lambda i: (i, 0)),
        out_shape=jax.ShapeDtypeStruct((rows, D_MODEL), F32),
        compiler_params=pltpu.CompilerParams(dimension_semantics=("arbitrary",)),
        name="ln_in",
    )(x2d, g.reshape(1, D_MODEL), b.reshape(1, D_MODEL))


def _sgu_chunk(z_ref, rows, y_ref, wcat, sbias, ln_g, ln_b, masks_b):
    su = _gelu(z_ref[rows, O_SU:O_SU + D_B])
    sv = _layer_norm(_gelu(z_ref[rows, O_SV:O_SV + D_B]), ln_g, ln_b)
    svb = sv.astype(BF16)
    zero = jnp.zeros_like(svb)
    svm = jnp.concatenate([jnp.where(m, svb, zero) for m in masks_b], axis=0)
    mixed = _dot(wcat, svm) + sbias
    y_ref[rows, D_A:D_A + D_B] = (su * mixed).astype(BF16)
    return sv


def _gla_gate(z_ref, rows, wa2, ba):
    cr = z_ref[rows, O_R:O_R + RANK_PAD].astype(BF16)
    return _log_sigmoid(_dot(cr, wa2) + ba) * (1.0 / GATE_TAU)


def _gla_head_out(z_ref, rows, y_ref, hh, o, gn):
    ms = jnp.mean(o * o, axis=-1, keepdims=True)
    on = o * lax.rsqrt(ms + LN_EPS) * gn
    g = z_ref[rows, O_G + hh * DV_C:O_G + (hh + 1) * DV_C]
    y_ref[rows, D_A + D_B + hh * DV_C:D_A + D_B + (hh + 1) * DV_C] = (_silu(g) * on).astype(BF16)


def _mixer_prompt_kernel(h_ref, mod_ref, win_ref, cw_ref, sg_ref, sb_ref, wcat_ref, sbias_ref,
                         wa2_ref, ba_ref, gn_ref, wo_ref, lng_ref, lnb_ref,
                         hout_ref, ca_ref, cv_ref, gla_ref,
                         u_ref, z_ref, xa_ref, y_ref, st_ref, *, ts, n_tiles):
    j = pl.program_id(1)

    @pl.when(j == 0)
    def _():
        xa_ref[0:SUBLANES, :] = jnp.zeros((SUBLANES, D_A), F32)
        st_ref[...] = jnp.zeros_like(st_ref)

    sh1, sc1, g1 = mod_ref[0:1, :], mod_ref[1:2, :], mod_ref[2:3, :]
    u_ref[...] = (h_ref[...] * (1.0 + sc1) + sh1).astype(BF16)

    def in_proj(p):
        rows = slice(p * MIX_RB, (p + 1) * MIX_RB)
        z_ref[rows, :] = _dot(u_ref[rows, :], win_ref[...])

    def out_proj(p):
        rows = slice(p * MIX_RB, (p + 1) * MIX_RB)
        y = _dot(y_ref[rows, :], wo_ref[...])
        hout_ref[rows, :] = _layer_norm(ALPHA * h_ref[rows, :] + g1 * y, lng_ref[...], lnb_ref[...])

    cw = cw_ref[...]
    wcat = wcat_ref[...].astype(BF16)
    sbias = sbias_ref[...]
    wa2 = wa2_ref[...]
    ba = ba_ref[...]
    gn = gn_ref[...]
    sg, sb = sg_ref[...], sb_ref[...]
    masks_b = _head_masks(D_B, H_B)
    masks_k = _head_masks(H_C * DK_C, H_C)
    row = lax.broadcasted_iota(jnp.int32, (CHUNK, CHUNK), 0)
    col = lax.broadcasted_iota(jnp.int32, (CHUNK, CHUNK), 1)
    causal = col <= row
    tri = causal.astype(BF16)

    def mix_chunk(c):
        r0 = c * CHUNK
        rows = slice(r0, r0 + CHUNK)
        x = z_ref[rows, O_AC:O_AC + D_A] * z_ref[rows, O_AIN:O_AIN + D_A]
        xa_ref[r0 + SUBLANES:r0 + SUBLANES + CHUNK, :] = x
        win = xa_ref[r0:r0 + CHUNK + SUBLANES, :]
        conv = (cw[0:1, :] * win[SUBLANES - 2:SUBLANES - 2 + CHUNK]
                + cw[1:2, :] * win[SUBLANES - 1:SUBLANES - 1 + CHUNK]
                + cw[2:3, :] * x)
        y_ref[rows, 0:D_A] = (z_ref[rows, O_AB:O_AB + D_A] * conv).astype(BF16)
        cv_ref[...] = _sgu_chunk(z_ref, rows, y_ref, wcat, sbias, sg, sb, masks_b)
        lg = _gla_gate(z_ref, rows, wa2, ba)
        b = _dot_exact_lhs(tri, lg)
        b_last = b[CHUNK - 1:CHUNK, :]
        q = z_ref[rows, O_Q:O_Q + H_C * DK_C]
        k = z_ref[rows, O_K:O_K + H_C * DK_C]
        qt = (q * (DK_C ** -0.5)) * jnp.exp(b)
        kt = (k * jnp.exp(-b)).astype(BF16)
        kh = k * jnp.exp(b_last - b)
        st = st_ref[...]
        stb = st.astype(BF16)
        st_new = st * jnp.exp(b_last)
        zq = jnp.zeros_like(qt)
        for hh in range(H_C):
            qm = jnp.where(masks_k[hh], qt, zq).astype(BF16)
            a = jnp.where(causal, _dot_nt(qm, kt), 0.0).astype(BF16)
            vb = z_ref[rows, O_V + hh * DV_C:O_V + (hh + 1) * DV_C].astype(BF16)
            o = _dot(a, vb) + _dot_nt(qm, stb)
            _gla_head_out(z_ref, rows, y_ref, hh, o, gn)
            khm = jnp.where(masks_k[hh], kh, zq).astype(BF16)
            st_new = st_new + _dot_tn(vb, khm)
        st_ref[...] = st_new

    n_rb = ts // MIX_RB
    per = MIX_RB // CHUNK
    in_proj(0)
    for p in range(n_rb):
        for c in range(p * per, (p + 1) * per):
            mix_chunk(c)
        if p + 1 < n_rb:
            in_proj(p + 1)
        out_proj(p)

    ca_ref[...] = xa_ref[ts + SUBLANES - 2:ts + SUBLANES, :]
    xa_ref[0:SUBLANES, :] = xa_ref[ts:ts + SUBLANES, :]

    @pl.when(j == n_tiles - 1)
    def _():
        gla_ref[...] = st_ref[...].T


def _const_spec(shape, n_grid):
    zeros = (0,) * len(shape)
    if n_grid == 1:
        return pl.BlockSpec(shape, lambda i: zeros, pipeline_mode=pl.Buffered(1))
    return pl.BlockSpec(shape, lambda b, j: zeros, pipeline_mode=pl.Buffered(1))


def _layer_spec(shape, l, n_grid):
    idx = (l,) + (0,) * len(shape)
    if n_grid == 1:
        return pl.BlockSpec((None,) + shape, lambda i: idx, pipeline_mode=pl.Buffered(1))
    return pl.BlockSpec((None,) + shape, lambda b, j: idx, pipeline_mode=pl.Buffered(1))


def _mixer_prompt_call(l, h, mod, p):
    ts = TS_MIX
    n_tiles = SEQ // ts
    ls = functools.partial(_layer_spec, l=l, n_grid=2)
    kern = functools.partial(_mixer_prompt_kernel, ts=ts, n_tiles=n_tiles)
    return pl.pallas_call(
        kern,
        grid=(BATCH, n_tiles),
        in_specs=[
            pl.BlockSpec((ts, D_MODEL), lambda b, j: (b * n_tiles + j, 0)),
            pl.BlockSpec((None, None, 6, D_MODEL), lambda b, j: (l, b, 0, 0)),
            ls((D_MODEL, N_IN_PAD)),
            ls((3, D_A)),
            ls((1, D_B)),
            ls((1, D_B)),
            ls((CHUNK, H_B * CHUNK)),
            ls((CHUNK, D_B)),
            ls((RANK_PAD, H_C * DK_C)),
            ls((1, H_C * DK_C)),
            ls((1, DV_C)),
            ls((D_MODEL, D_MODEL)),
            ls((1, D_MODEL)),
            ls((1, D_MODEL)),
        ],
        out_specs=[
            pl.BlockSpec((ts, D_MODEL), lambda b, j: (b * n_tiles + j, 0)),
            pl.BlockSpec((None, 2, D_A), lambda b, j: (b, 0, 0)),
            pl.BlockSpec((None, CHUNK, D_B), lambda b, j: (b, 0, 0)),
            pl.BlockSpec((None, H_C * DK_C, DV_C), lambda b, j: (b, 0, 0)),
        ],
        out_shape=[
            jax.ShapeDtypeStruct((BATCH * SEQ, D_MODEL), F32),
            jax.ShapeDtypeStruct((BATCH, 2, D_A), F32),
            jax.ShapeDtypeStruct((BATCH, CHUNK, D_B), F32),
            jax.ShapeDtypeStruct((BATCH, H_C * DK_C, DV_C), F32),
        ],
        scratch_shapes=[
            pltpu.VMEM((ts, D_MODEL), BF16),
            pltpu.VMEM((ts, N_IN_PAD), F32),
            pltpu.VMEM((ts + SUBLANES, D_A), F32),
            pltpu.VMEM((ts, D_MODEL), BF16),
            pltpu.VMEM((DV_C, H_C * DK_C), F32),
        ],
        compiler_params=pltpu.CompilerParams(
            dimension_semantics=("arbitrary", "arbitrary"), vmem_limit_bytes=VMEM_LIMIT),
        name="mixer_prompt",
    )(h, mod, p["w_in"], p["conv_a_w"], p["sgu_ln_g"], p["sgu_ln_b"], p["wcat_p"], p["sbias_p"],
      p["w_a2"], p["b_a"], p["gla_norm_g"], p["w_o"], p["ln1_g"], p["ln1_b"])


def _expand_rows(x3):
    n, _, c = x3.shape
    return jnp.broadcast_to(x3, (n, DEC_SEQ, c)).reshape(n * DEC_SEQ, c)


def _sample_conv(x, past, cw):
    p0 = _expand_rows(past[:, 0:1, :])
    p1 = _expand_rows(past[:, 1:2, :])
    t_idx = lax.broadcasted_iota(jnp.int32, x.shape, 0) & (DEC_SEQ - 1)
    xm1 = jnp.where(t_idx >= 1, pltpu.roll(x, 1, 0), p1)
    xm2 = jnp.where(t_idx >= 2, pltpu.roll(x, 2, 0), jnp.where(t_idx == 1, p1, p0))
    return cw[0:1, :] * xm2 + cw[1:2, :] * xm1 + cw[2:3, :] * x


def _mixer_sample_kernel(h_ref, mod_ref, past_ref, s0_ref, win_ref, cw_ref, sg_ref, sb_ref,
                         wcat_ref, sbias_ref, wa2_ref, ba_ref, gn_ref, wo_ref, lng_ref, lnb_ref,
                         hout_ref, ca_ref, cv_ref, gla_ref,
                         z_ref, y_ref, qt_ref, kh_ref, eb_ref, oi_ref, *, tb):
    rr = tb * DEC_SEQ
    h = h_ref[...].reshape(rr, D_MODEL)
    sh1 = _expand_rows(mod_ref[:, 0:1, :])
    sc1 = _expand_rows(mod_ref[:, 1:2, :])
    g1 = _expand_rows(mod_ref[:, 2:3, :])
    u = (h * (1.0 + sc1) + sh1).astype(BF16)
    z_ref[...] = _dot(u, win_ref[...])

    x = z_ref[:, O_AC:O_AC + D_A] * z_ref[:, O_AIN:O_AIN + D_A]
    conv = _sample_conv(x, past_ref[...], cw_ref[...])
    y_ref[:, 0:D_A] = (z_ref[:, O_AB:O_AB + D_A] * conv).astype(BF16)
    ca_ref[...] = x.reshape(tb, DEC_SEQ, D_A)[:, DEC_SEQ - 2:DEC_SEQ, :]

    wcat = wcat_ref[...].astype(BF16)
    sbias = sbias_ref[...]
    wa2 = wa2_ref[...]
    ba = ba_ref[...]
    gn = gn_ref[...]
    sg, sb = sg_ref[...], sb_ref[...]
    masks_b = _head_masks(D_B, H_B)
    masks_k = _head_masks(H_C * DK_C, H_C)
    row = lax.broadcasted_iota(jnp.int32, (CHUNK, CHUNK), 0)
    col = lax.broadcasted_iota(jnp.int32, (CHUNK, CHUNK), 1)
    same_seq = (row // DEC_SEQ) == (col // DEC_SEQ)
    causal = same_seq & (col <= row)
    tri = causal.astype(BF16)
    blk = same_seq.astype(BF16)

    for gi in range(rr // CHUNK):
        rows = pl.ds(gi * CHUNK, CHUNK)
        sv = _sgu_chunk(z_ref, rows, y_ref, wcat, sbias, sg, sb, masks_b)
        cv_ref[gi * GROUP_SEQ:(gi + 1) * GROUP_SEQ] = sv.reshape(GROUP_SEQ, DEC_SEQ, D_B)
        lg = _gla_gate(z_ref, rows, wa2, ba)
        b = _dot_exact_lhs(tri, lg)
        b_tot = _dot_exact_lhs(blk, lg)
        q = z_ref[rows, O_Q:O_Q + H_C * DK_C]
        k = z_ref[rows, O_K:O_K + H_C * DK_C]
        qt = (q * (DK_C ** -0.5)) * jnp.exp(b)
        kt = (k * jnp.exp(-b)).astype(BF16)
        qt_ref[rows, :] = qt
        kh_ref[rows, :] = k * jnp.exp(b_tot - b)
        eb_ref[rows, :] = jnp.exp(b_tot)
        zq = jnp.zeros_like(qt)
        for hh in range(H_C):
            qm = jnp.where(masks_k[hh], qt, zq).astype(BF16)
            a = jnp.where(causal, _dot_nt(qm, kt), 0.0).astype(BF16)
            vb = z_ref[rows, O_V + hh * DV_C:O_V + (hh + 1) * DV_C].astype(BF16)
            oi_ref[rows, hh * DV_C:(hh + 1) * DV_C] = _dot(a, vb)

    ones16 = jnp.ones((2 * SUBLANES, DV_C), BF16)
    sub = lax.broadcasted_iota(jnp.int32, (SUBLANES, H_C * DK_C), 0)

    def seq_body(i, carry):
        r0 = pl.multiple_of(i * DEC_SEQ, DEC_SEQ)
        rows = pl.ds(r0, DEC_SEQ)
        s0 = s0_ref[i]
        qt = qt_ref[rows, :]
        kh = kh_ref[rows, :]
        zq = jnp.zeros_like(qt)
        q_exp = jnp.concatenate([jnp.where(m, qt, zq) for m in masks_k], axis=0).astype(BF16)
        k_exp = jnp.concatenate([jnp.where(m, kh, zq) for m in masks_k], axis=0).astype(BF16)
        o_int = _dot(q_exp, s0.astype(BF16))
        v = z_ref[rows, O_V:O_V + D_C]
        v_exp = jnp.concatenate([v[:, hh * DV_C:(hh + 1) * DV_C] for hh in range(H_C)],
                                axis=0).astype(BF16)
        eb = eb_ref[rows, :]
        hi, mid, lo = _split3(eb)
        e3 = jnp.where(sub == 0, hi.astype(F32),
                       jnp.where(sub == 1, mid.astype(F32),
                                 jnp.where(sub == 2, lo.astype(F32), jnp.zeros_like(eb))))
        e3 = jnp.concatenate([e3, jnp.zeros_like(eb)], axis=0).astype(BF16)
        decay = _dot_tn(e3, ones16)
        gla_ref[i] = decay * s0 + _dot_tn(k_exp, v_exp)
        for hh in range(H_C):
            oi_ref[rows, hh * DV_C:(hh + 1) * DV_C] += o_int[hh * DEC_SEQ:(hh + 1) * DEC_SEQ, :]
        return carry

    lax.fori_loop(0, tb, seq_body, 0)

    for gi in range(rr // CHUNK):
        rows = pl.ds(gi * CHUNK, CHUNK)
        for hh in range(H_C):
            _gla_head_out(z_ref, rows, y_ref, hh, oi_ref[rows, hh * DV_C:(hh + 1) * DV_C], gn)

    y = _dot(y_ref[...], wo_ref[...])
    hn = _layer_norm(ALPHA * h + g1 * y, lng_ref[...], lnb_ref[...])
    hout_ref[...] = hn.reshape(tb, DEC_SEQ, D_MODEL)


def _mixer_sample_call(l, h3, mod, past, s0, p):
    tb = TB_S
    ls = functools.partial(_layer_spec, l=l, n_grid=1)
    kern = functools.partial(_mixer_sample_kernel, tb=tb)
    rr = tb * DEC_SEQ
    return pl.pallas_call(
        kern,
        grid=(DEC_BATCH // tb,),
        in_specs=[
            pl.BlockSpec((tb, DEC_SEQ, D_MODEL), lambda i: (i, 0, 0)),
            pl.BlockSpec((None, tb, 6, D_MODEL), lambda i: (l, i, 0, 0)),
            pl.BlockSpec((None, tb, 2, D_A), lambda i: (l, i, 0, 0)),
            pl.BlockSpec((None, tb, H_C * DK_C, DV_C), lambda i: (l, i, 0, 0)),
            ls((D_MODEL, N_IN_PAD)),
            ls((3, D_A)),
            ls((1, D_B)),
            ls((1, D_B)),
            ls((CHUNK, H_B * CHUNK)),
            ls((CHUNK, D_B)),
            ls((RANK_PAD, H_C * DK_C)),
            ls((1, H_C * DK_C)),
            ls((1, DV_C)),
            ls((D_MODEL, D_MODEL)),
            ls((1, D_MODEL)),
            ls((1, D_MODEL)),
        ],
        out_specs=[
            pl.BlockSpec((tb, DEC_SEQ, D_MODEL), lambda i: (i, 0, 0)),
            pl.BlockSpec((tb, 2, D_A), lambda i: (i, 0, 0)),
            pl.BlockSpec((tb, DEC_SEQ, D_B), lambda i: (i, 0, 0)),
            pl.BlockSpec((tb, H_C * DK_C, DV_C), lambda i: (i, 0, 0)),
        ],
        out_shape=[
            jax.ShapeDtypeStruct((DEC_BATCH, DEC_SEQ, D_MODEL), F32),
            jax.ShapeDtypeStruct((DEC_BATCH, 2, D_A), F32),
            jax.ShapeDtypeStruct((DEC_BATCH, DEC_SEQ, D_B), F32),
            jax.ShapeDtypeStruct((DEC_BATCH, H_C * DK_C, DV_C), F32),
        ],
        scratch_shapes=[
            pltpu.VMEM((rr, N_IN_PAD), F32),
            pltpu.VMEM((rr, D_MODEL), BF16),
            pltpu.VMEM((rr, H_C * DK_C), F32),
            pltpu.VMEM((rr, H_C * DK_C), F32),
            pltpu.VMEM((rr, H_C * DK_C), F32),
            pltpu.VMEM((rr, D_C), F32),
        ],
        compiler_params=pltpu.CompilerParams(
            dimension_semantics=("arbitrary",), vmem_limit_bytes=VMEM_LIMIT),
        name="mixer_sample",
    )(h3, mod, past, s0, p["w_in"], p["conv_a_w"], p["sgu_ln_g"], p["sgu_ln_b"], p["wcat_s"],
      p["sbias_s"], p["w_a2"], p["b_a"], p["gla_norm_g"], p["w_o"], p["ln1_g"], p["ln1_b"])


FFN_RB = 32
FFN_CB = 256
N_CB = D_FF // FFN_CB


def _ffn_block_major(w_up, conv_w, conv_b, w_down):
    def pair(x):
        lead = x.shape[:-1]
        x = x.reshape(lead + (2, N_CB, FFN_CB))
        x = jnp.moveaxis(x, -2, 1)
        return x.reshape((lead[0], N_CB) + lead[1:] + (2 * FFN_CB,))
    return (pair(w_up).astype(BF16), pair(conv_w), pair(conv_b.reshape(DEPTH, 1, 2 * D_FF)),
            w_down.reshape(DEPTH, N_CB, FFN_CB, D_MODEL).astype(BF16))


def _ffn_tail(h, g2, act_ref, wdn_ref, lng_ref, lnb_ref):
    f = _dot(act_ref[...], wdn_ref[...])
    return _layer_norm(ALPHA * h + g2 * f, lng_ref[...], lnb_ref[...])


def _ffn_prompt_kernel(h_ref, mod_ref, wup_ref, cw_ref, cb_ref, wdn_ref, lng_ref, lnb_ref,
                       hout_ref, st_ref, u_ref, xp_ref, act_ref, acc_ref, *, ts):
    j = pl.program_id(1)

    @pl.when(j == 0)
    def _():
        xp_ref[:, 0:SUBLANES, :] = jnp.zeros((N_CB, SUBLANES, 2 * FFN_CB), F32)

    sh2, sc2 = mod_ref[3:4, :], mod_ref[4:5, :]
    u_ref[...] = (h_ref[...] * (1.0 + sc2) + sh2).astype(BF16)

    n_rb = ts // FFN_RB
    n_dn = D_MODEL // FFN_CB

    def up_pieces(c):
        def piece(half):
            cols = slice(half * FFN_CB, (half + 1) * FFN_CB)
            xp_ref[c, SUBLANES:SUBLANES + ts, cols] = _dot(u_ref[...], wup_ref[c, :, cols])
        return [functools.partial(piece, half) for half in range(2)]

    def down_pieces(c, first=False):
        def piece(n):
            cols = slice(n * FFN_CB, (n + 1) * FFN_CB)
            d = _dot(act_ref[c % 2], wdn_ref[c, :, cols])
            if first:
                acc_ref[:, cols] = d
            else:
                acc_ref[:, cols] += d
        return [functools.partial(piece, n) for n in range(n_dn)]

    def elementwise_pieces(c):
        def piece(r):
            cw = cw_ref[c]
            win = xp_ref[c, r:r + FFN_RB + SUBLANES, :]
            conv = (cw[0:1, :] * win[SUBLANES - 2:SUBLANES - 2 + FFN_RB]
                    + cw[1:2, :] * win[SUBLANES - 1:SUBLANES - 1 + FFN_RB]
                    + cw[2:3, :] * win[SUBLANES:SUBLANES + FFN_RB]
                    + cb_ref[c])
            act_ref[c % 2, r:r + FFN_RB, :] = (
                _silu(conv[:, FFN_CB:]) * conv[:, :FFN_CB]).astype(BF16)
        return [functools.partial(piece, rb * FFN_RB) for rb in range(n_rb)]

    def interleave(valu_pieces, mxu_pieces):
        n_v, n_m = len(valu_pieces), len(mxu_pieces)
        done = 0
        for k, mp in enumerate(mxu_pieces):
            upto = (k + 1) * n_v // (n_m + 1) if n_m else n_v
            for vp in valu_pieces[done:upto]:
                vp()
            done = upto
            mp()
        for vp in valu_pieces[done:]:
            vp()

    interleave([], up_pieces(0))
    for c in range(N_CB):
        interleave(elementwise_pieces(c), [])
        if c + 1 < N_CB:
            interleave([], up_pieces(c + 1))
        interleave([], down_pieces(c, first=(c == 0)))

    g2 = mod_ref[5:6, :]
    hout_ref[...] = _layer_norm(ALPHA * h_ref[...] + g2 * acc_ref[...], lng_ref[...], lnb_ref[...])
    for c in range(N_CB):
        st_ref[:, c * FFN_CB:(c + 1) * FFN_CB] = xp_ref[c, ts + SUBLANES - 2:ts + SUBLANES, :FFN_CB]
        st_ref[:, D_FF + c * FFN_CB:D_FF + (c + 1) * FFN_CB] = (
            xp_ref[c, ts + SUBLANES - 2:ts + SUBLANES, FFN_CB:])
    xp_ref[:, 0:SUBLANES, :] = xp_ref[:, ts:ts + SUBLANES, :]


def _ffn_prompt_call(l, h, mod, p):
    ts = TS_FFN
    n_tiles = SEQ // ts
    ls = functools.partial(_layer_spec, l=l, n_grid=2)
    kern = functools.partial(_ffn_prompt_kernel, ts=ts)
    return pl.pallas_call(
        kern,
        grid=(BATCH, n_tiles),
        in_specs=[
            pl.BlockSpec((ts, D_MODEL), lambda b, j: (b * n_tiles + j, 0)),
            pl.BlockSpec((None, None, 6, D_MODEL), lambda b, j: (l, b, 0, 0)),
            ls((N_CB, D_MODEL, 2 * FFN_CB)),
            ls((N_CB, 3, 2 * FFN_CB)),
            ls((N_CB, 1, 2 * FFN_CB)),
            ls((N_CB, FFN_CB, D_MODEL)),
            ls((1, D_MODEL)),
            ls((1, D_MODEL)),
        ],
        out_specs=[
            pl.BlockSpec((ts, D_MODEL), lambda b, j: (b * n_tiles + j, 0)),
            pl.BlockSpec((None, 2, 2 * D_FF), lambda b, j: (b, 0, 0)),
        ],
        out_shape=[
            jax.ShapeDtypeStruct((BATCH * SEQ, D_MODEL), F32),
            jax.ShapeDtypeStruct((BATCH, 2, 2 * D_FF), F32),
        ],
        scratch_shapes=[
            pltpu.VMEM((ts, D_MODEL), BF16),
            pltpu.VMEM((N_CB, ts + SUBLANES, 2 * FFN_CB), F32),
            pltpu.VMEM((2, ts, FFN_CB), BF16),
            pltpu.VMEM((ts, D_MODEL), F32),
        ],
        compiler_params=pltpu.CompilerParams(
            dimension_semantics=("arbitrary", "arbitrary"), vmem_limit_bytes=VMEM_LIMIT),
        name="ffn_prompt",
    )(h, mod, p["w_up_blk"], p["ffn_conv_w_blk"], p["ffn_conv_b_blk"], p["w_down_blk"],
      p["ln2_g"], p["ln2_b"])


def _ffn_sample_kernel(h_ref, mod_ref, past_ref, wup_ref, cw_ref, cb_ref, wdn_ref, lng_ref, lnb_ref,
                       hout_ref, st_ref, up_ref, act_ref, *, tb):
    rr = tb * DEC_SEQ
    h = h_ref[...].reshape(rr, D_MODEL)
    sh2 = _expand_rows(mod_ref[:, 3:4, :])
    sc2 = _expand_rows(mod_ref[:, 4:5, :])
    g2 = _expand_rows(mod_ref[:, 5:6, :])
    u = (h * (1.0 + sc2) + sh2).astype(BF16)
    up_ref[...] = _dot(u, wup_ref[...])
    n_seq = FFN_RB // DEC_SEQ

    def conv_cols(i, r0, c0):
        cols = slice(c0, c0 + FFN_CB)
        seqs = pl.ds(i * n_seq, n_seq)
        x = up_ref[pl.ds(r0, FFN_RB), cols]
        st_ref[seqs, :, cols] = x.reshape(n_seq, DEC_SEQ, FFN_CB)[:, DEC_SEQ - 2:DEC_SEQ, :]
        return _sample_conv(x, past_ref[seqs, :, cols], cw_ref[:, cols]) + cb_ref[0:1, cols]

    def row_body(i, carry):
        r0 = pl.multiple_of(i * FFN_RB, FFN_RB)
        for cb in range(D_FF // FFN_CB):
            a = conv_cols(i, r0, cb * FFN_CB)
            g = conv_cols(i, r0, D_FF + cb * FFN_CB)
            act_ref[pl.ds(r0, FFN_RB), cb * FFN_CB:(cb + 1) * FFN_CB] = (_silu(g) * a).astype(BF16)
        return carry

    lax.fori_loop(0, rr // FFN_RB, row_body, 0)

    hn = _ffn_tail(h, g2, act_ref, wdn_ref, lng_ref, lnb_ref)
    hout_ref[...] = hn.reshape(tb, DEC_SEQ, D_MODEL)


def _ffn_sample_call(l, h3, mod, past, p):
    tb = TB_S
    rr = tb * DEC_SEQ
    ls = functools.partial(_layer_spec, l=l, n_grid=1)
    kern = functools.partial(_ffn_sample_kernel, tb=tb)
    return pl.pallas_call(
        kern,
        grid=(DEC_BATCH // tb,),
        in_specs=[
            pl.BlockSpec((tb, DEC_SEQ, D_MODEL), lambda i: (i, 0, 0)),
            pl.BlockSpec((None, tb, 6, D_MODEL), lambda i: (l, i, 0, 0)),
            pl.BlockSpec((None, tb, 2, 2 * D_FF), lambda i: (l, i, 0, 0)),
            ls((D_MODEL, 2 * D_FF)),
            ls((3, 2 * D_FF)),
            ls((1, 2 * D_FF)),
            ls((D_FF, D_MODEL)),
            ls((1, D_MODEL)),
            ls((1, D_MODEL)),
        ],
        out_specs=[
            pl.BlockSpec((tb, DEC_SEQ, D_MODEL), lambda i: (i, 0, 0)),
            pl.BlockSpec((tb, 2, 2 * D_FF), lambda i: (i, 0, 0)),
        ],
        out_shape=[
            jax.ShapeDtypeStruct((DEC_BATCH, DEC_SEQ, D_MODEL), F32),
            jax.ShapeDtypeStruct((DEC_BATCH, 2, 2 * D_FF), F32),
        ],
        scratch_shapes=[
            pltpu.VMEM((rr, 2 * D_FF), F32),
            pltpu.VMEM((rr, D_FF), BF16),
        ],
        compiler_params=pltpu.CompilerParams(
            dimension_semantics=("arbitrary",), vmem_limit_bytes=VMEM_LIMIT),
        name="ffn_sample",
    )(h3, mod, past, p["w_up"], p["ffn_conv_w"], p["ffn_conv_b"], p["w_down"], p["ln2_g"], p["ln2_b"])


def _prep_params(w_in, conv_a_w, sgu_ln_g, sgu_ln_b, sgu_w, sgu_b, gla_w_a2, gla_b_a, gla_norm_g,
                 w_o, ln1_g, ln1_b, ffn_w_up, ffn_conv_w, ffn_conv_b, ffn_w_down, ln2_g, ln2_b):
    tril = jnp.tril(jnp.ones((CHUNK, CHUNK), bool))
    w_p = jnp.where(tril[None, None], sgu_w, 0.0)
    wcat_p = jnp.transpose(w_p, (0, 2, 1, 3)).reshape(DEPTH, CHUNK, H_B * CHUNK)
    tril8 = jnp.tril(jnp.ones((DEC_SEQ, DEC_SEQ), bool))
    w8 = jnp.where(tril8[None, None], sgu_w[:, :, :DEC_SEQ, :DEC_SEQ], 0.0)
    eye = jnp.eye(GROUP_SEQ, dtype=F32)
    w_blk = jnp.einsum("ij,lhts->lhitjs", eye, w8).reshape(DEPTH, H_B, CHUNK, CHUNK)
    wcat_s = jnp.transpose(w_blk, (0, 2, 1, 3)).reshape(DEPTH, CHUNK, H_B * CHUNK)
    sbias_p = jnp.repeat(jnp.transpose(sgu_b, (0, 2, 1)), DH_B, axis=2)
    sbias_s = jnp.tile(sbias_p[:, :DEC_SEQ, :], (1, GROUP_SEQ, 1))
    w_up_blk, conv_w_blk, conv_b_blk, w_down_blk = _ffn_block_major(
        ffn_w_up, ffn_conv_w, ffn_conv_b, ffn_w_down)
    return {
        "w_up_blk": w_up_blk, "ffn_conv_w_blk": conv_w_blk, "ffn_conv_b_blk": conv_b_blk,
        "w_down_blk": w_down_blk,
        "w_in": jnp.pad(w_in, ((0, 0), (0, 0), (0, N_IN_PAD - N_IN))).astype(BF16),
        "conv_a_w": conv_a_w,
        "sgu_ln_g": sgu_ln_g.reshape(DEPTH, 1, D_B),
        "sgu_ln_b": sgu_ln_b.reshape(DEPTH, 1, D_B),
        "wcat_p": wcat_p, "wcat_s": wcat_s, "sbias_p": sbias_p, "sbias_s": sbias_s,
        "w_a2": jnp.pad(gla_w_a2, ((0, 0), (0, RANK_PAD - GATE_RANK), (0, 0))).astype(BF16),
        "b_a": gla_b_a.reshape(DEPTH, 1, H_C * DK_C),
        "gla_norm_g": gla_norm_g.reshape(DEPTH, 1, DV_C),
        "w_o": w_o.astype(BF16),
        "ln1_g": ln1_g.reshape(DEPTH, 1, D_MODEL), "ln1_b": ln1_b.reshape(DEPTH, 1, D_MODEL),
        "w_up": ffn_w_up.astype(BF16),
        "ffn_conv_w": ffn_conv_w,
        "ffn_conv_b": ffn_conv_b.reshape(DEPTH, 1, 2 * D_FF),
        "w_down": ffn_w_down.astype(BF16),
        "ln2_g": ln2_g.reshape(DEPTH, 1, D_MODEL), "ln2_b": ln2_b.reshape(DEPTH, 1, D_MODEL),
    }


def kernel(x_prompt, x_sample, c_prompt, c_sample, state_conv_a, state_gla, state_ffn_conv,
           ln_in_g, ln_in_b, w_ada, b_ada, w_in, conv_a_w, sgu_ln_g, sgu_ln_b, sgu_w, sgu_b,
           gla_w_a2, gla_b_a, gla_norm_g, w_o, ln1_g, ln1_b, ffn_w_up, ffn_conv_w, ffn_conv_b,
           ffn_w_down, ln2_g, ln2_b):
    p = _prep_params(w_in, conv_a_w, sgu_ln_g, sgu_ln_b, sgu_w, sgu_b, gla_w_a2, gla_b_a,
                     gla_norm_g, w_o, ln1_g, ln1_b, ffn_w_up, ffn_conv_w, ffn_conv_b,
                     ffn_w_down, ln2_g, ln2_b)
    mod = _ada_call(jnp.concatenate([c_prompt, c_sample], axis=0), w_ada, b_ada)
    mod = mod.reshape(DEPTH, BATCH + DEC_BATCH, 6, D_MODEL)
    mod_p, mod_s = mod[:, :BATCH], mod[:, BATCH:]
    s0_all = state_gla.reshape(DEPTH, DEC_BATCH, H_C * DK_C, DV_C)

    hp = _ln_call(x_prompt.reshape(BATCH * SEQ, D_MODEL), ln_in_g, ln_in_b)
    hs = _ln_call(x_sample.reshape(DEC_BATCH * DEC_SEQ, D_MODEL), ln_in_g, ln_in_b)
    hs = hs.reshape(DEC_BATCH, DEC_SEQ, D_MODEL)

    outs_p = [[], [], [], []]
    outs_s = [[], [], [], []]
    for l in range(DEPTH):
        hp, ca, cv, gs = _mixer_prompt_call(l, hp, mod_p, p)
        hp, fs = _ffn_prompt_call(l, hp, mod_p, p)
        for acc, val in zip(outs_p, (ca, cv, gs, fs)):
            acc.append(val)
        hs, ca, cv, gs = _mixer_sample_call(l, hs, mod_s, state_conv_a, s0_all, p)
        hs, fs = _ffn_sample_call(l, hs, mod_s, state_ffn_conv, p)
        for acc, val in zip(outs_s, (ca, cv, gs, fs)):
            acc.append(val)

    ca_p, cv_p, gla_p, ffn_p = [jnp.stack(v) for v in outs_p]
    ca_s, cv_s, gla_s, ffn_s = [jnp.stack(v) for v in outs_s]
    return (hp.reshape(BATCH, SEQ, D_MODEL), hs,
            ca_p, ca_s, cv_p, cv_s,
            gla_p.reshape(DEPTH, BATCH, H_C, DK_C, DV_C),
            gla_s.reshape(DEPTH, DEC_BATCH, H_C, DK_C, DV_C),
            ffn_p, ffn_s)
```

```python
import functools
import math

import jax
import jax.numpy as jnp
from jax import lax
from jax.experimental import pallas as pl
from jax.experimental.pallas import tpu as pltpu

D_MODEL = 1024
BATCH = 8
SEQ = 2048
DEPTH = 4
DEC_BATCH = 128
DEC_SEQ = 8
D_A = 256
D_B = 256
H_B = 4
DH_B = 64
D_C = 512
H_C = 4
DV_C = 128
DK_C = 64
GATE_RANK = 16
GATE_TAU = 16.0
CHUNK = 128
D_FF = 2816
LN_EPS = 1e-5
ALPHA = (2.0 * DEPTH) ** 0.25
N_IN = 2832
N_IN_PAD = 2944
RANK_PAD = 128

O_AIN, O_AB, O_AC, O_SU, O_SV, O_Q, O_K, O_V, O_G, O_R = (
    0, 256, 512, 768, 1024, 1280, 1536, 1792, 2304, 2816)

SUBLANES = 8
VMEM_LIMIT = 56 * 1024 * 1024

TS_MIX = 512
MIX_RB = 256
TS_FFN = 256
TB_S = 32
GROUP_SEQ = CHUNK // DEC_SEQ

F32 = jnp.float32
BF16 = jnp.bfloat16


def _dot(a, b):
    return jnp.dot(a, b, preferred_element_type=F32)


def _dot_nt(a, b):
    return lax.dot_general(a, b, (((1,), (1,)), ((), ())), preferred_element_type=F32)


def _dot_tn(a, b):
    return lax.dot_general(a, b, (((0,), (0,)), ((), ())), preferred_element_type=F32)


def _layer_norm(x, g, b):
    mu = jnp.mean(x, axis=-1, keepdims=True)
    xc = x - mu
    var = jnp.mean(xc * xc, axis=-1, keepdims=True)
    return xc * lax.rsqrt(var + LN_EPS) * g + b


def _gelu(x):
    c = math.sqrt(2.0 / math.pi)
    return 0.5 * x * (1.0 + jnp.tanh(c * (x + 0.044715 * (x * x * x))))


def _silu(x):
    return x * jax.nn.sigmoid(x)


def _log_sigmoid(x):
    return jnp.minimum(x, 0.0) - jnp.log1p(jnp.exp(-jnp.abs(x)))


def _split3(x):
    hi = x.astype(BF16)
    r1 = x - hi.astype(F32)
    mid = r1.astype(BF16)
    lo = (r1 - mid.astype(F32)).astype(BF16)
    return hi, mid, lo


def _dot_exact_lhs(m_bf16, x):
    hi, mid, lo = _split3(x)
    return _dot(m_bf16, hi) + _dot(m_bf16, mid) + _dot(m_bf16, lo)


def _head_masks(width, n_heads):
    lane = lax.broadcasted_iota(jnp.int32, (1, width), 1)
    per = width // n_heads
    return [(lane >= h * per) & (lane < (h + 1) * per) for h in range(n_heads)]


def _interleave(valu_pieces, mxu_pieces):
    n_v, n_m = len(valu_pieces), len(mxu_pieces)
    done = 0
    for k, mp in enumerate(mxu_pieces):
        upto = (k + 1) * n_v // (n_m + 1)
        for vp in valu_pieces[done:upto]:
            vp()
        done = upto
        mp()
    for vp in valu_pieces[done:]:
        vp()


def _ada_kernel(c_ref, w_ref, b_ref, o_ref):
    cs = _silu(c_ref[...]).astype(BF16)
    o_ref[...] = _dot(cs, w_ref[...].astype(BF16)) + b_ref[...]


def _ada_call(c_all, w_ada, b_ada):
    n_rows = c_all.shape[0]
    tn = 1536
    return pl.pallas_call(
        _ada_kernel,
        grid=(DEPTH, 6 * D_MODEL // tn),
        in_specs=[
            pl.BlockSpec((n_rows, D_MODEL), lambda l, j: (0, 0)),
            pl.BlockSpec((None, D_MODEL, tn), lambda l, j: (l, 0, j)),
            pl.BlockSpec((None, 1, tn), lambda l, j: (l, 0, j)),
        ],
        out_specs=pl.BlockSpec((None, n_rows, tn), lambda l, j: (l, 0, j)),
        out_shape=jax.ShapeDtypeStruct((DEPTH, n_rows, 6 * D_MODEL), F32),
        compiler_params=pltpu.CompilerParams(
            dimension_semantics=("arbitrary", "arbitrary"), vmem_limit_bytes=VMEM_LIMIT),
        name="ada_mod",
    )(c_all, w_ada, b_ada.reshape(DEPTH, 1, 6 * D_MODEL))


def _ln_kernel(x_ref, g_ref, b_ref, o_ref):
    o_ref[...] = _layer_norm(x_ref[...], g_ref[...], b_ref[...])


def _ln_call(x2d, g, b):
    rows = x2d.shape[0]
    tr = 512
    return pl.pallas_call(
        _ln_kernel,
        grid=(rows // tr,),
        in_specs=[
            pl.BlockSpec((tr, D_MODEL), lambda i: (i, 0)),
            pl.BlockSpec((1, D_MODEL), lambda i: (0, 0)),
            pl.BlockSpec((1, D_MODEL), lambda i: (0, 0)),
        ],
        out_specs=pl.BlockSpec((tr, D_MODEL), lambda i: (i, 0)),
        out_shape=jax.ShapeDtypeStruct((rows, D_MODEL), F32),
        compiler_params=pltpu.CompilerParams(dimension_semantics=("arbitrary",)),
        name="ln_in",
    )(x2d, g.reshape(1, D_MODEL), b.reshape(1, D_MODEL))


def _sgu_chunk(z_ref, rows, y_ref, wcat, sbias, ln_g, ln_b, masks_b):
    su = _gelu(z_ref[rows, O_SU:O_SU + D_B])
    sv = _layer_norm(_gelu(z_ref[rows, O_SV:O_SV + D_B]), ln_g, ln_b)
    svb = sv.astype(BF16)
    zero = jnp.zeros_like(svb)
    svm = jnp.concatenate([jnp.where(m, svb, zero) for m in masks_b], axis=0)
    mixed = _dot(wcat, svm) + sbias
    y_ref[rows, D_A:D_A + D_B] = (su * mixed).astype(BF16)
    return sv


def _gla_gate(z_ref, rows, wa2, ba):
    cr = z_ref[rows, O_R:O_R + RANK_PAD].astype(BF16)
    return _log_sigmoid(_dot(cr, wa2) + ba) * (1.0 / GATE_TAU)


def _gla_head_out(z_ref, rows, y_ref, hh, o, gn):
    ms = jnp.mean(o * o, axis=-1, keepdims=True)
    on = o * lax.rsqrt(ms + LN_EPS) * gn
    g = z_ref[rows, O_G + hh * DV_C:O_G + (hh + 1) * DV_C]
    y_ref[rows, D_A + D_B + hh * DV_C:D_A + D_B + (hh + 1) * DV_C] = (_silu(g) * on).astype(BF16)


def _layer_spec(shape, l, n_grid):
    idx = (l,) + (0,) * len(shape)
    if n_grid == 1:
        return pl.BlockSpec((None,) + shape, lambda i: idx, pipeline_mode=pl.Buffered(1))
    return pl.BlockSpec((None,) + shape, lambda b, j: idx, pipeline_mode=pl.Buffered(1))


def _mixer_prompt_kernel(h_ref, mod_ref, win_ref, cw_ref, sg_ref, sb_ref, wcat_ref, sbias_ref,
                         wa2_ref, ba_ref, gn_ref, wo_ref, lng_ref, lnb_ref,
                         hout_ref, ca_ref, cv_ref, gla_ref,
                         u_ref, z_ref, xa_ref, y_ref, st_ref, *, ts, n_tiles):
    j = pl.program_id(1)

    @pl.when(j == 0)
    def _():
        xa_ref[0:SUBLANES, :] = jnp.zeros((SUBLANES, D_A), F32)
        st_ref[...] = jnp.zeros_like(st_ref)

    sh1, sc1, g1 = mod_ref[0:1, :], mod_ref[1:2, :], mod_ref[2:3, :]
    u_ref[...] = (h_ref[...] * (1.0 + sc1) + sh1).astype(BF16)

    def in_proj(p):
        rows = slice(p * MIX_RB, (p + 1) * MIX_RB)
        z_ref[rows, :] = _dot(u_ref[rows, :], win_ref[...])

    def out_proj(p):
        rows = slice(p * MIX_RB, (p + 1) * MIX_RB)
        y = _dot(y_ref[rows, :], wo_ref[...])
        hout_ref[rows, :] = _layer_norm(ALPHA * h_ref[rows, :] + g1 * y, lng_ref[...], lnb_ref[...])

    cw = cw_ref[...]
    wcat = wcat_ref[...].astype(BF16)
    sbias = sbias_ref[...]
    wa2 = wa2_ref[...]
    ba = ba_ref[...]
    gn = gn_ref[...]
    sg, sb = sg_ref[...], sb_ref[...]
    masks_b = _head_masks(D_B, H_B)
    masks_k = _head_masks(H_C * DK_C, H_C)
    row = lax.broadcasted_iota(jnp.int32, (CHUNK, CHUNK), 0)
    col = lax.broadcasted_iota(jnp.int32, (CHUNK, CHUNK), 1)
    causal = col <= row
    tri = causal.astype(BF16)

    def mix_chunk(c):
        r0 = c * CHUNK
        rows = slice(r0, r0 + CHUNK)
        x = z_ref[rows, O_AC:O_AC + D_A] * z_ref[rows, O_AIN:O_AIN + D_A]
        xa_ref[r0 + SUBLANES:r0 + SUBLANES + CHUNK, :] = x
        win = xa_ref[r0:r0 + CHUNK + SUBLANES, :]
        conv = (cw[0:1, :] * win[SUBLANES - 2:SUBLANES - 2 + CHUNK]
                + cw[1:2, :] * win[SUBLANES - 1:SUBLANES - 1 + CHUNK]
                + cw[2:3, :] * x)
        y_ref[rows, 0:D_A] = (z_ref[rows, O_AB:O_AB + D_A] * conv).astype(BF16)
        cv_ref[...] = _sgu_chunk(z_ref, rows, y_ref, wcat, sbias, sg, sb, masks_b)
        lg = _gla_gate(z_ref, rows, wa2, ba)
        b = _dot_exact_lhs(tri, lg)
        b_last = b[CHUNK - 1:CHUNK, :]
        q = z_ref[rows, O_Q:O_Q + H_C * DK_C]
        k = z_ref[rows, O_K:O_K + H_C * DK_C]
        qt = (q * (DK_C ** -0.5)) * jnp.exp(b)
        kt = (k * jnp.exp(-b)).astype(BF16)
        kh = k * jnp.exp(b_last - b)
        st = st_ref[...]
        stb = st.astype(BF16)
        st_new = st * jnp.exp(b_last)
        zq = jnp.zeros_like(qt)
        for hh in range(H_C):
            qm = jnp.where(masks_k[hh], qt, zq).astype(BF16)
            a = jnp.where(causal, _dot_nt(qm, kt), 0.0).astype(BF16)
            vb = z_ref[rows, O_V + hh * DV_C:O_V + (hh + 1) * DV_C].astype(BF16)
            o = _dot(a, vb) + _dot_nt(qm, stb)
            _gla_head_out(z_ref, rows, y_ref, hh, o, gn)
            khm = jnp.where(masks_k[hh], kh, zq).astype(BF16)
            st_new = st_new + _dot_tn(vb, khm)
        st_ref[...] = st_new

    n_rb = ts // MIX_RB
    per = MIX_RB // CHUNK
    in_proj(0)
    for p in range(n_rb):
        for c in range(p * per, (p + 1) * per):
            mix_chunk(c)
        if p + 1 < n_rb:
            in_proj(p + 1)
        out_proj(p)

    ca_ref[...] = xa_ref[ts + SUBLANES - 2:ts + SUBLANES, :]
    xa_ref[0:SUBLANES, :] = xa_ref[ts:ts + SUBLANES, :]

    @pl.when(j == n_tiles - 1)
    def _():
        gla_ref[...] = st_ref[...].T


def _mixer_prompt_call(l, h, mod, p):
    ts = TS_MIX
    n_tiles = SEQ // ts
    ls = functools.partial(_layer_spec, l=l, n_grid=2)
    kern = functools.partial(_mixer_prompt_kernel, ts=ts, n_tiles=n_tiles)
    return pl.pallas_call(
        kern,
        grid=(BATCH, n_tiles),
        in_specs=[
            pl.BlockSpec((ts, D_MODEL), lambda b, j: (b * n_tiles + j, 0)),
            pl.BlockSpec((None, None, 6, D_MODEL), lambda b, j: (l, b, 0, 0)),
            ls((D_MODEL, N_IN_PAD)),
            ls((3, D_A)),
            ls((1, D_B)),
            ls((1, D_B)),
            ls((CHUNK, H_B * CHUNK)),
            ls((CHUNK, D_B)),
            ls((RANK_PAD, H_C * DK_C)),
            ls((1, H_C * DK_C)),
            ls((1, DV_C)),
            ls((D_MODEL, D_MODEL)),
            ls((1, D_MODEL)),
            ls((1, D_MODEL)),
        ],
        out_specs=[
            pl.BlockSpec((ts, D_MODEL), lambda b, j: (b * n_tiles + j, 0)),
            pl.BlockSpec((None, 2, D_A), lambda b, j: (b, 0, 0)),
            pl.BlockSpec((None, CHUNK, D_B), lambda b, j: (b, 0, 0)),
            pl.BlockSpec((None, H_C * DK_C, DV_C), lambda b, j: (b, 0, 0)),
        ],
        out_shape=[
            jax.ShapeDtypeStruct((BATCH * SEQ, D_MODEL), F32),
            jax.ShapeDtypeStruct((BATCH, 2, D_A), F32),
            jax.ShapeDtypeStruct((BATCH, CHUNK, D_B), F32),
            jax.ShapeDtypeStruct((BATCH, H_C * DK_C, DV_C), F32),
        ],
        scratch_shapes=[
            pltpu.VMEM((ts, D_MODEL), BF16),
            pltpu.VMEM((ts, N_IN_PAD), F32),
            pltpu.VMEM((ts + SUBLANES, D_A), F32),
            pltpu.VMEM((ts, D_MODEL), BF16),
            pltpu.VMEM((DV_C, H_C * DK_C), F32),
        ],
        compiler_params=pltpu.CompilerParams(
            dimension_semantics=("arbitrary", "arbitrary"), vmem_limit_bytes=VMEM_LIMIT),
        name="mixer_prompt",
    )(h, mod, p["w_in"], p["conv_a_w"], p["sgu_ln_g"], p["sgu_ln_b"], p["wcat_p"], p["sbias_p"],
      p["w_a2"], p["b_a"], p["gla_norm_g"], p["w_o"], p["ln1_g"], p["ln1_b"])


def _expand_rows(x3):
    n, _, c = x3.shape
    return jnp.broadcast_to(x3, (n, DEC_SEQ, c)).reshape(n * DEC_SEQ, c)


def _sample_conv(x, past, cw):
    p0 = _expand_rows(past[:, 0:1, :])
    p1 = _expand_rows(past[:, 1:2, :])
    t_idx = lax.broadcasted_iota(jnp.int32, x.shape, 0) & (DEC_SEQ - 1)
    xm1 = jnp.where(t_idx >= 1, pltpu.roll(x, 1, 0), p1)
    xm2 = jnp.where(t_idx >= 2, pltpu.roll(x, 2, 0), jnp.where(t_idx == 1, p1, p0))
    return cw[0:1, :] * xm2 + cw[1:2, :] * xm1 + cw[2:3, :] * x


def _mixer_sample_kernel(h_ref, mod_ref, past_ref, s0_ref, win_ref, cw_ref, sg_ref, sb_ref,
                         wcat_ref, sbias_ref, wa2_ref, ba_ref, gn_ref, wo_ref, lng_ref, lnb_ref,
                         hout_ref, ca_ref, cv_ref, gla_ref,
                         z_ref, y_ref, qt_ref, kh_ref, eb_ref, oi_ref, *, tb):
    rr = tb * DEC_SEQ
    h = h_ref[...].reshape(rr, D_MODEL)
    sh1 = _expand_rows(mod_ref[:, 0:1, :])
    sc1 = _expand_rows(mod_ref[:, 1:2, :])
    g1 = _expand_rows(mod_ref[:, 2:3, :])
    u = (h * (1.0 + sc1) + sh1).astype(BF16)
    z_ref[...] = _dot(u, win_ref[...])

    x = z_ref[:, O_AC:O_AC + D_A] * z_ref[:, O_AIN:O_AIN + D_A]
    conv = _sample_conv(x, past_ref[...], cw_ref[...])
    y_ref[:, 0:D_A] = (z_ref[:, O_AB:O_AB + D_A] * conv).astype(BF16)
    ca_ref[...] = x.reshape(tb, DEC_SEQ, D_A)[:, DEC_SEQ - 2:DEC_SEQ, :]

    wcat = wcat_ref[...].astype(BF16)
    sbias = sbias_ref[...]
    wa2 = wa2_ref[...]
    ba = ba_ref[...]
    gn = gn_ref[...]
    sg, sb = sg_ref[...], sb_ref[...]
    masks_b = _head_masks(D_B, H_B)
    masks_k = _head_masks(H_C * DK_C, H_C)
    row = lax.broadcasted_iota(jnp.int32, (CHUNK, CHUNK), 0)
    col = lax.broadcasted_iota(jnp.int32, (CHUNK, CHUNK), 1)
    same_seq = (row // DEC_SEQ) == (col // DEC_SEQ)
    causal = same_seq & (col <= row)
    tri = causal.astype(BF16)
    blk = same_seq.astype(BF16)

    for gi in range(rr // CHUNK):
        rows = pl.ds(gi * CHUNK, CHUNK)
        sv = _sgu_chunk(z_ref, rows, y_ref, wcat, sbias, sg, sb, masks_b)
        cv_ref[gi * GROUP_SEQ:(gi + 1) * GROUP_SEQ] = sv.reshape(GROUP_SEQ, DEC_SEQ, D_B)
        lg = _gla_gate(z_ref, rows, wa2, ba)
        b = _dot_exact_lhs(tri, lg)
        b_tot = _dot_exact_lhs(blk, lg)
        q = z_ref[rows, O_Q:O_Q + H_C * DK_C]
        k = z_ref[rows, O_K:O_K + H_C * DK_C]
        qt = (q * (DK_C ** -0.5)) * jnp.exp(b)
        kt = (k * jnp.exp(-b)).astype(BF16)
        qt_ref[rows, :] = qt
        kh_ref[rows, :] = k * jnp.exp(b_tot - b)
        eb_ref[rows, :] = jnp.exp(b_tot)
        zq = jnp.zeros_like(qt)
        for hh in range(H_C):
            qm = jnp.where(masks_k[hh], qt, zq).astype(BF16)
            a = jnp.where(causal, _dot_nt(qm, kt), 0.0).astype(BF16)
            vb = z_ref[rows, O_V + hh * DV_C:O_V + (hh + 1) * DV_C].astype(BF16)
            oi_ref[rows, hh * DV_C:(hh + 1) * DV_C] = _dot(a, vb)

    ones16 = jnp.ones((2 * SUBLANES, DV_C), BF16)
    sub = lax.broadcasted_iota(jnp.int32, (SUBLANES, H_C * DK_C), 0)

    def seq_body(i, carry):
        r0 = pl.multiple_of(i * DEC_SEQ, DEC_SEQ)
        rows = pl.ds(r0, DEC_SEQ)
        s0 = s0_ref[i]
        qt = qt_ref[rows, :]
        kh = kh_ref[rows, :]
        zq = jnp.zeros_like(qt)
        q_exp = jnp.concatenate([jnp.where(m, qt, zq) for m in masks_k], axis=0).astype(BF16)
        k_exp = jnp.concatenate([jnp.where(m, kh, zq) for m in masks_k], axis=0).astype(BF16)
        o_int = _dot(q_exp, s0.astype(BF16))
        v = z_ref[rows, O_V:O_V + D_C]
        v_exp = jnp.concatenate([v[:, hh * DV_C:(hh + 1) * DV_C] for hh in range(H_C)],
                                axis=0).astype(BF16)
        eb = eb_ref[rows, :]
        hi, mid, lo = _split3(eb)
        e3 = jnp.where(sub == 0, hi.astype(F32),
                       jnp.where(sub == 1, mid.astype(F32),
                                 jnp.where(sub == 2, lo.astype(F32), jnp.zeros_like(eb))))
        e3 = jnp.concatenate([e3, jnp.zeros_like(eb)], axis=0).astype(BF16)
        decay = _dot_tn(e3, ones16)
        gla_ref[i] = decay * s0 + _dot_tn(k_exp, v_exp)
        for hh in range(H_C):
            oi_ref[rows, hh * DV_C:(hh + 1) * DV_C] += o_int[hh * DEC_SEQ:(hh + 1) * DEC_SEQ, :]
        return carry

    lax.fori_loop(0, tb, seq_body, 0)

    for gi in range(rr // CHUNK):
        rows = pl.ds(gi * CHUNK, CHUNK)
        for hh in range(H_C):
            _gla_head_out(z_ref, rows, y_ref, hh, oi_ref[rows, hh * DV_C:(hh + 1) * DV_C], gn)

    y = _dot(y_ref[...], wo_ref[...])
    hn = _layer_norm(ALPHA * h + g1 * y, lng_ref[...], lnb_ref[...])
    hout_ref[...] = hn.reshape(tb, DEC_SEQ, D_MODEL)


def _mixer_sample_call(l, h3, mod, past, s0, p):
    tb = TB_S
    ls = functools.partial(_layer_spec, l=l, n_grid=1)
    kern = functools.partial(_mixer_sample_kernel, tb=tb)
    rr = tb * DEC_SEQ
    return pl.pallas_call(
        kern,
        grid=(DEC_BATCH // tb,),
        in_specs=[
            pl.BlockSpec((tb, DEC_SEQ, D_MODEL), lambda i: (i, 0, 0)),
            pl.BlockSpec((None, tb, 6, D_MODEL), lambda i: (l, i, 0, 0)),
            pl.BlockSpec((None, tb, 2, D_A), lambda i: (l, i, 0, 0)),
            pl.BlockSpec((None, tb, H_C * DK_C, DV_C), lambda i: (l, i, 0, 0)),
            ls((D_MODEL, N_IN_PAD)),
            ls((3, D_A)),
            ls((1, D_B)),
            ls((1, D_B)),
            ls((CHUNK, H_B * CHUNK)),
            ls((CHUNK, D_B)),
            ls((RANK_PAD, H_C * DK_C)),
            ls((1, H_C * DK_C)),
            ls((1, DV_C)),
            ls((D_MODEL, D_MODEL)),
            ls((1, D_MODEL)),
            ls((1, D_MODEL)),
        ],
        out_specs=[
            pl.BlockSpec((tb, DEC_SEQ, D_MODEL), lambda i: (i, 0, 0)),
            pl.BlockSpec((tb, 2, D_A), lambda i: (i, 0, 0)),
            pl.BlockSpec((tb, DEC_SEQ, D_B), lambda i: (i, 0, 0)),
            pl.BlockSpec((tb, H_C * DK_C, DV_C), lambda i: (i, 0, 0)),
        ],
        out_shape=[
            jax.ShapeDtypeStruct((DEC_BATCH, DEC_SEQ, D_MODEL), F32),
            jax.ShapeDtypeStruct((DEC_BATCH, 2, D_A), F32),
            jax.ShapeDtypeStruct((DEC_BATCH, DEC_SEQ, D_B), F32),
            jax.ShapeDtypeStruct((DEC_BATCH, H_C * DK_C, DV_C), F32),
        ],
        scratch_shapes=[
            pltpu.VMEM((rr, N_IN_PAD), F32),
            pltpu.VMEM((rr, D_MODEL), BF16),
            pltpu.VMEM((rr, H_C * DK_C), F32),
            pltpu.VMEM((rr, H_C * DK_C), F32),
            pltpu.VMEM((rr, H_C * DK_C), F32),
            pltpu.VMEM((rr, D_C), F32),
        ],
        compiler_params=pltpu.CompilerParams(
            dimension_semantics=("arbitrary",), vmem_limit_bytes=VMEM_LIMIT),
        name="mixer_sample",
    )(h3, mod, past, s0, p["w_in"], p["conv_a_w"], p["sgu_ln_g"], p["sgu_ln_b"], p["wcat_s"],
      p["sbias_s"], p["w_a2"], p["b_a"], p["gla_norm_g"], p["w_o"], p["ln1_g"], p["ln1_b"])


FFN_RB = 32
FFN_CB = 256
N_CB = D_FF // FFN_CB
FFN_AHEAD = 3


def _ffn_kernel(*refs, rows, sample):
    if sample:
        (h_ref, mod_ref, past_ref, wup_ref, cw_ref, cb_ref, wdn_ref, lng_ref, lnb_ref,
         hout_ref, st_ref, u_ref, xp_ref, act_ref, acc_ref) = refs
        off = 0
        h = h_ref[...].reshape(rows, D_MODEL)
        sh2, sc2, g2 = (_expand_rows(mod_ref[:, k:k + 1, :]) for k in (3, 4, 5))
    else:
        (h_ref, mod_ref, wup_ref, cw_ref, cb_ref, wdn_ref, lng_ref, lnb_ref,
         hout_ref, st_ref, u_ref, xp_ref, act_ref, acc_ref) = refs
        off = SUBLANES
        h = h_ref[...]
        sh2, sc2, g2 = mod_ref[3:4, :], mod_ref[4:5, :], mod_ref[5:6, :]

        @pl.when(pl.program_id(1) == 0)
        def _():
            xp_ref[0:SUBLANES, :] = jnp.zeros((SUBLANES, 2 * D_FF), F32)

    u_ref[...] = (h * (1.0 + sc2) + sh2).astype(BF16)

    def block_cols(c):
        return (slice(c * FFN_CB, (c + 1) * FFN_CB),
                slice(D_FF + c * FFN_CB, D_FF + (c + 1) * FFN_CB))

    def up_pieces(c):
        def piece(cols):
            xp_ref[off:off + rows, cols] = _dot(u_ref[...], wup_ref[:, cols])
        return [functools.partial(piece, cols) for cols in block_cols(c)]

    def down_pieces(c):
        def piece(n):
            cols = slice(n * FFN_CB, (n + 1) * FFN_CB)
            d = _dot(act_ref[c % 2], wdn_ref[c * FFN_CB:(c + 1) * FFN_CB, cols])
            if c == 0:
                acc_ref[:, cols] = d
            else:
                acc_ref[:, cols] += d
        return [functools.partial(piece, n) for n in range(D_MODEL // FFN_CB)]

    def conv_rows(r, cols):
        if sample:
            seqs = slice(r // DEC_SEQ, (r + FFN_RB) // DEC_SEQ)
            x = xp_ref[r:r + FFN_RB, cols]
            st_ref[seqs, :, cols] = x.reshape(FFN_RB // DEC_SEQ, DEC_SEQ, FFN_CB)[:, DEC_SEQ - 2:, :]
            y = _sample_conv(x, past_ref[seqs, :, cols], cw_ref[:, cols])
        else:
            win = xp_ref[r:r + FFN_RB + SUBLANES, cols]
            y = (cw_ref[0:1, cols] * win[SUBLANES - 2:SUBLANES - 2 + FFN_RB]
                 + cw_ref[1:2, cols] * win[SUBLANES - 1:SUBLANES - 1 + FFN_RB]
                 + cw_ref[2:3, cols] * win[SUBLANES:SUBLANES + FFN_RB])
        return y + cb_ref[0:1, cols]

    def elementwise_pieces(c):
        ca, cg = block_cols(c)

        def piece(r):
            act_ref[c % 2, r:r + FFN_RB, :] = (
                _silu(conv_rows(r, cg)) * conv_rows(r, ca)).astype(BF16)
        return [functools.partial(piece, r) for r in range(0, rows, FFN_RB)]

    for c in range(FFN_AHEAD):
        _interleave([], up_pieces(c))
    for c in range(N_CB):
        mxu = up_pieces(c + FFN_AHEAD) if c + FFN_AHEAD < N_CB else []
        if c >= 1:
            mxu = mxu + down_pieces(c - 1)
        _interleave(elementwise_pieces(c), mxu)
    _interleave([], down_pieces(N_CB - 1))

    hn = _layer_norm(ALPHA * h + g2 * acc_ref[...], lng_ref[...], lnb_ref[...])
    if sample:
        hout_ref[...] = hn.reshape(rows // DEC_SEQ, DEC_SEQ, D_MODEL)
    else:
        hout_ref[...] = hn
        st_ref[...] = xp_ref[rows + SUBLANES - 2:rows + SUBLANES, :]
        xp_ref[0:SUBLANES, :] = xp_ref[rows:rows + SUBLANES, :]


def _ffn_scratch(rows, off):
    return [
        pltpu.VMEM((rows, D_MODEL), BF16),
        pltpu.VMEM((rows + off, 2 * D_FF), F32),
        pltpu.VMEM((2, rows, FFN_CB), BF16),
        pltpu.VMEM((rows, D_MODEL), F32),
    ]


def _ffn_prompt_call(l, h, mod, p):
    ts = TS_FFN
    n_tiles = SEQ // ts
    ls = functools.partial(_layer_spec, l=l, n_grid=2)
    kern = functools.partial(_ffn_kernel, rows=ts, sample=False)
    return pl.pallas_call(
        kern,
        grid=(BATCH, n_tiles),
        in_specs=[
            pl.BlockSpec((ts, D_MODEL), lambda b, j: (b * n_tiles + j, 0)),
            pl.BlockSpec((None, None, 6, D_MODEL), lambda b, j: (l, b, 0, 0)),
            ls((D_MODEL, 2 * D_FF)),
            ls((3, 2 * D_FF)),
            ls((1, 2 * D_FF)),
            ls((D_FF, D_MODEL)),
            ls((1, D_MODEL)),
            ls((1, D_MODEL)),
        ],
        out_specs=[
            pl.BlockSpec((ts, D_MODEL), lambda b, j: (b * n_tiles + j, 0)),
            pl.BlockSpec((None, 2, 2 * D_FF), lambda b, j: (b, 0, 0)),
        ],
        out_shape=[
            jax.ShapeDtypeStruct((BATCH * SEQ, D_MODEL), F32),
            jax.ShapeDtypeStruct((BATCH, 2, 2 * D_FF), F32),
        ],
        scratch_shapes=_ffn_scratch(ts, SUBLANES),
        compiler_params=pltpu.CompilerParams(
            dimension_semantics=("arbitrary", "arbitrary"), vmem_limit_bytes=VMEM_LIMIT),
        name="ffn_prompt",
    )(h, mod, p["w_up"], p["ffn_conv_w"], p["ffn_conv_b"], p["w_down"], p["ln2_g"], p["ln2_b"])


def _ffn_sample_call(l, h3, mod, past, p):
    tb = TB_S
    rr = tb * DEC_SEQ
    ls = functools.partial(_layer_spec, l=l, n_grid=1)
    kern = functools.partial(_ffn_kernel, rows=rr, sample=True)
    return pl.pallas_call(
        kern,
        grid=(DEC_BATCH // tb,),
        in_specs=[
            pl.BlockSpec((tb, DEC_SEQ, D_MODEL), lambda i: (i, 0, 0)),
            pl.BlockSpec((None, tb, 6, D_MODEL), lambda i: (l, i, 0, 0)),
            pl.BlockSpec((None, tb, 2, 2 * D_FF), lambda i: (l, i, 0, 0)),
            ls((D_MODEL, 2 * D_FF)),
            ls((3, 2 * D_FF)),
            ls((1, 2 * D_FF)),
            ls((D_FF, D_MODEL)),
            ls((1, D_MODEL)),
            ls((1, D_MODEL)),
        ],
        out_specs=[
            pl.BlockSpec((tb, DEC_SEQ, D_MODEL), lambda i: (i, 0, 0)),
            pl.BlockSpec((tb, 2, 2 * D_FF), lambda i: (i, 0, 0)),
        ],
        out_shape=[
            jax.ShapeDtypeStruct((DEC_BATCH, DEC_SEQ, D_MODEL), F32),
            jax.ShapeDtypeStruct((DEC_BATCH, 2, 2 * D_FF), F32),
        ],
        scratch_shapes=_ffn_scratch(rr, 0),
        compiler_params=pltpu.CompilerParams(
            dimension_semantics=("arbitrary",), vmem_limit_bytes=VMEM_LIMIT),
        name="ffn_sample",
    )(h3, mod, past, p["w_up"], p["ffn_conv_w"], p["ffn_conv_b"], p["w_down"], p["ln2_g"], p["ln2_b"])


def _prep_params(w_in, conv_a_w, sgu_ln_g, sgu_ln_b, sgu_w, sgu_b, gla_w_a2, gla_b_a, gla_norm_g,
                 w_o, ln1_g, ln1_b, ffn_w_up, ffn_conv_w, ffn_conv_b, ffn_w_down, ln2_g, ln2_b):
    tril = jnp.tril(jnp.ones((CHUNK, CHUNK), bool))
    w_p = jnp.where(tril[None, None], sgu_w, 0.0)
    wcat_p = jnp.transpose(w_p, (0, 2, 1, 3)).reshape(DEPTH, CHUNK, H_B * CHUNK)
    tril8 = jnp.tril(jnp.ones((DEC_SEQ, DEC_SEQ), bool))
    w8 = jnp.where(tril8[None, None], sgu_w[:, :, :DEC_SEQ, :DEC_SEQ], 0.0)
    eye = jnp.eye(GROUP_SEQ, dtype=F32)
    w_blk = jnp.einsum("ij,lhts->lhitjs", eye, w8).reshape(DEPTH, H_B, CHUNK, CHUNK)
    wcat_s = jnp.transpose(w_blk, (0, 2, 1, 3)).reshape(DEPTH, CHUNK, H_B * CHUNK)
    sbias_p = jnp.repeat(jnp.transpose(sgu_b, (0, 2, 1)), DH_B, axis=2)
    sbias_s = jnp.tile(sbias_p[:, :DEC_SEQ, :], (1, GROUP_SEQ, 1))
    return {
        "w_in": jnp.pad(w_in, ((0, 0), (0, 0), (0, N_IN_PAD - N_IN))).astype(BF16),
        "conv_a_w": conv_a_w,
        "sgu_ln_g": sgu_ln_g.reshape(DEPTH, 1, D_B),
        "sgu_ln_b": sgu_ln_b.reshape(DEPTH, 1, D_B),
        "wcat_p": wcat_p, "wcat_s": wcat_s, "sbias_p": sbias_p, "sbias_s": sbias_s,
        "w_a2": jnp.pad(gla_w_a2, ((0, 0), (0, RANK_PAD - GATE_RANK), (0, 0))).astype(BF16),
        "b_a": gla_b_a.reshape(DEPTH, 1, H_C * DK_C),
        "gla_norm_g": gla_norm_g.reshape(DEPTH, 1, DV_C),
        "w_o": w_o.astype(BF16),
        "ln1_g": ln1_g.reshape(DEPTH, 1, D_MODEL), "ln1_b": ln1_b.reshape(DEPTH, 1, D_MODEL),
        "w_up": ffn_w_up.astype(BF16),
        "ffn_conv_w": ffn_conv_w,
        "ffn_conv_b": ffn_conv_b.reshape(DEPTH, 1, 2 * D_FF),
        "w_down": ffn_w_down.astype(BF16),
        "ln2_g": ln2_g.reshape(DEPTH, 1, D_MODEL), "ln2_b": ln2_b.reshape(DEPTH, 1, D_MODEL),
    }


def kernel(x_prompt, x_sample, c_prompt, c_sample, state_conv_a, state_gla, state_ffn_conv,
           ln_in_g, ln_in_b, w_ada, b_ada, w_in, conv_a_w, sgu_ln_g, sgu_ln_b, sgu_w, sgu_b,
           gla_w_a2, gla_b_a, gla_norm_g, w_o, ln1_g, ln1_b, ffn_w_up, ffn_conv_w, ffn_conv_b,
           ffn_w_down, ln2_g, ln2_b):
    p = _prep_params(w_in, conv_a_w, sgu_ln_g, sgu_ln_b, sgu_w, sgu_b, gla_w_a2, gla_b_a,
                     gla_norm_g, w_o, ln1_g, ln1_b, ffn_w_up, ffn_conv_w, ffn_conv_b,
                     ffn_w_down, ln2_g, ln2_b)
    mod = _ada_call(jnp.concatenate([c_prompt, c_sample], axis=0), w_ada, b_ada)
    mod = mod.reshape(DEPTH, BATCH + DEC_BATCH, 6, D_MODEL)
    mod_p, mod_s = mod[:, :BATCH], mod[:, BATCH:]
    s0_all = state_gla.reshape(DEPTH, DEC_BATCH, H_C * DK_C, DV_C)

    hp = _ln_call(x_prompt.reshape(BATCH * SEQ, D_MODEL), ln_in_g, ln_in_b)
    hs = _ln_call(x_sample.reshape(DEC_BATCH * DEC_SEQ, D_MODEL), ln_in_g, ln_in_b)
    hs = hs.reshape(DEC_BATCH, DEC_SEQ, D_MODEL)

    outs_p = [[], [], [], []]
    outs_s = [[], [], [], []]
    for l in range(DEPTH):
        hp, ca, cv, gs = _mixer_prompt_call(l, hp, mod_p, p)
        hp, fs = _ffn_prompt_call(l, hp, mod_p, p)
        for acc, val in zip(outs_p, (ca, cv, gs, fs)):
            acc.append(val)
        hs, ca, cv, gs = _mixer_sample_call(l, hs, mod_s, state_conv_a, s0_all, p)
        hs, fs = _ffn_sample_call(l, hs, mod_s, state_ffn_conv, p)
        for acc, val in zip(outs_s, (ca, cv, gs, fs)):
            acc.append(val)

    ca_p, cv_p, gla_p, ffn_p = [jnp.stack(v) for v in outs_p]
    ca_s, cv_s, gla_s, ffn_s = [jnp.stack(v) for v in outs_s]
    return (hp.reshape(BATCH, SEQ, D_MODEL), hs,
            ca_p, ca_s, cv_p, cv_s,
            gla_p.reshape(DEPTH, BATCH, H_C, DK_C, DV_C),
            gla_s.reshape(DEPTH, DEC_BATCH, H_C, DK_C, DV_C),
            ffn_p, ffn_s)
```

```python
import functools
import math

import jax
import jax.numpy as jnp
from jax import lax
from jax.experimental import pallas as pl
from jax.experimental.pallas import tpu as pltpu

D_MODEL = 1024
BATCH = 8
SEQ = 2048
DEPTH = 4
DEC_BATCH = 128
DEC_SEQ = 8
D_A = 256
D_B = 256
H_B = 4
DH_B = 64
D_C = 512
H_C = 4
DV_C = 128
DK_C = 64
GATE_RANK = 16
GATE_TAU = 16.0
CHUNK = 128
D_FF = 2816
LN_EPS = 1e-5
ALPHA = (2.0 * DEPTH) ** 0.25
N_IN = 2832
N_IN_PAD = 2944
RANK_PAD = 128

O_AIN, O_AB, O_AC, O_SU, O_SV, O_Q, O_K, O_V, O_G, O_R = (
    0, 256, 512, 768, 1024, 1280, 1536, 1792, 2304, 2816)

SUBLANES = 8
VMEM_LIMIT = 56 * 1024 * 1024

TS_MIX = 1024
MIX_RB = 256
MIX_CB = 256
NORM_RB = 32
TS_FFN = 256
TB_S = 32
GROUP_SEQ = CHUNK // DEC_SEQ

F32 = jnp.float32
BF16 = jnp.bfloat16


def _dot(a, b):
    return jnp.dot(a, b, preferred_element_type=F32)


def _dot_nt(a, b):
    return lax.dot_general(a, b, (((1,), (1,)), ((), ())), preferred_element_type=F32)


def _dot_tn(a, b):
    return lax.dot_general(a, b, (((0,), (0,)), ((), ())), preferred_element_type=F32)


def _layer_norm(x, g, b):
    mu = jnp.mean(x, axis=-1, keepdims=True)
    xc = x - mu
    var = jnp.mean(xc * xc, axis=-1, keepdims=True)
    return xc * lax.rsqrt(var + LN_EPS) * g + b


def _gelu(x):
    c = math.sqrt(2.0 / math.pi)
    return 0.5 * x * (1.0 + jnp.tanh(c * (x + 0.044715 * (x * x * x))))


def _silu(x):
    return x * jax.nn.sigmoid(x)


def _log_sigmoid(x):
    return jnp.minimum(x, 0.0) - jnp.log1p(jnp.exp(-jnp.abs(x)))


def _split3(x):
    hi = x.astype(BF16)
    r1 = x - hi.astype(F32)
    mid = r1.astype(BF16)
    lo = (r1 - mid.astype(F32)).astype(BF16)
    return hi, mid, lo


def _dot_exact_lhs(m_bf16, x):
    hi, mid, lo = _split3(x)
    return _dot(m_bf16, hi) + _dot(m_bf16, mid) + _dot(m_bf16, lo)


def _head_masks(width, n_heads):
    lane = lax.broadcasted_iota(jnp.int32, (1, width), 1)
    per = width // n_heads
    return [(lane >= h * per) & (lane < (h + 1) * per) for h in range(n_heads)]


def _interleave(valu_pieces, mxu_pieces):
    n_v, n_m = len(valu_pieces), len(mxu_pieces)
    done = 0
    for k, mp in enumerate(mxu_pieces):
        upto = (k + 1) * n_v // (n_m + 1)
        for vp in valu_pieces[done:upto]:
            vp()
        done = upto
        mp()
    for vp in valu_pieces[done:]:
        vp()


def _ada_kernel(c_ref, w_ref, b_ref, o_ref):
    cs = _silu(c_ref[...]).astype(BF16)
    o_ref[...] = _dot(cs, w_ref[...].astype(BF16)) + b_ref[...]


def _ada_call(c_all, w_ada, b_ada):
    n_rows = c_all.shape[0]
    tn = 1536
    return pl.pallas_call(
        _ada_kernel,
        grid=(DEPTH, 6 * D_MODEL // tn),
        in_specs=[
            pl.BlockSpec((n_rows, D_MODEL), lambda l, j: (0, 0)),
            pl.BlockSpec((None, D_MODEL, tn), lambda l, j: (l, 0, j)),
            pl.BlockSpec((None, 1, tn), lambda l, j: (l, 0, j)),
        ],
        out_specs=pl.BlockSpec((None, n_rows, tn), lambda l, j: (l, 0, j)),
        out_shape=jax.ShapeDtypeStruct((DEPTH, n_rows, 6 * D_MODEL), F32),
        compiler_params=pltpu.CompilerParams(
            dimension_semantics=("arbitrary", "arbitrary"), vmem_limit_bytes=VMEM_LIMIT),
        name="ada_mod",
    )(c_all, w_ada, b_ada.reshape(DEPTH, 1, 6 * D_MODEL))


def _ln_kernel(x_ref, g_ref, b_ref, o_ref):
    o_ref[...] = _layer_norm(x_ref[...], g_ref[...], b_ref[...])


def _ln_call(x2d, g, b):
    rows = x2d.shape[0]
    tr = 512
    return pl.pallas_call(
        _ln_kernel,
        grid=(rows // tr,),
        in_specs=[
            pl.BlockSpec((tr, D_MODEL), lambda i: (i, 0)),
            pl.BlockSpec((1, D_MODEL), lambda i: (0, 0)),
            pl.BlockSpec((1, D_MODEL), lambda i: (0, 0)),
        ],
        out_specs=pl.BlockSpec((tr, D_MODEL), lambda i: (i, 0)),
        out_shape=jax.ShapeDtypeStruct((rows, D_MODEL), F32),
        compiler_params=pltpu.CompilerParams(dimension_semantics=("arbitrary",)),
        name="ln_in",
    )(x2d, g.reshape(1, D_MODEL), b.reshape(1, D_MODEL))


def _sgu_chunk(z_ref, rows, y_ref, wcat, sbias, ln_g, ln_b, masks_b):
    su = _gelu(z_ref[rows, O_SU:O_SU + D_B])
    sv = _layer_norm(_gelu(z_ref[rows, O_SV:O_SV + D_B]), ln_g, ln_b)
    svb = sv.astype(BF16)
    zero = jnp.zeros_like(svb)
    svm = jnp.concatenate([jnp.where(m, svb, zero) for m in masks_b], axis=0)
    mixed = _dot(wcat, svm) + sbias
    y_ref[rows, D_A:D_A + D_B] = (su * mixed).astype(BF16)
    return sv


def _gla_gate(z_ref, rows, wa2, ba):
    cr = z_ref[rows, O_R:O_R + RANK_PAD].astype(BF16)
    return _log_sigmoid(_dot(cr, wa2) + ba) * (1.0 / GATE_TAU)


def _gla_head_out(z_ref, rows, y_ref, hh, o, gn):
    ms = jnp.mean(o * o, axis=-1, keepdims=True)
    on = o * lax.rsqrt(ms + LN_EPS) * gn
    g = z_ref[rows, O_G + hh * DV_C:O_G + (hh + 1) * DV_C]
    y_ref[rows, D_A + D_B + hh * DV_C:D_A + D_B + (hh + 1) * DV_C] = (_silu(g) * on).astype(BF16)


def _layer_spec(shape, l, n_grid):
    idx = (l,) + (0,) * len(shape)
    if n_grid == 1:
        return pl.BlockSpec((None,) + shape, lambda i: idx, pipeline_mode=pl.Buffered(1))
    return pl.BlockSpec((None,) + shape, lambda b, j: idx, pipeline_mode=pl.Buffered(1))


def _mixer_prompt_kernel(h_ref, mod_ref, win_ref, cw_ref, sg_ref, sb_ref, wcat_ref, sbias_ref,
                         wa2_ref, ba_ref, gn_ref, wo_ref, lng_ref, lnb_ref,
                         hout_ref, ca_ref, cv_ref, gla_ref,
                         u_ref, z_ref, xa_ref, y_ref, st_ref, yo_ref, *, ts, n_tiles):
    j = pl.program_id(1)

    @pl.when(j == 0)
    def _():
        xa_ref[0:SUBLANES, :] = jnp.zeros((SUBLANES, D_A), F32)
        st_ref[...] = jnp.zeros_like(st_ref)

    sh1, sc1, g1 = mod_ref[0:1, :], mod_ref[1:2, :], mod_ref[2:3, :]
    u_ref[...] = (h_ref[...] * (1.0 + sc1) + sh1).astype(BF16)

    def in_proj_pieces(p):
        rows = slice(p * MIX_RB, (p + 1) * MIX_RB)

        def piece(c0):
            cols = slice(c0, min(c0 + MIX_CB, N_IN_PAD))
            z_ref[rows, cols] = _dot(u_ref[rows, :], win_ref[:, cols])
        return [functools.partial(piece, c0) for c0 in range(0, N_IN_PAD, MIX_CB)]

    def out_proj_pieces(p):
        rows = slice(p * MIX_RB, (p + 1) * MIX_RB)

        def piece(c0):
            cols = slice(c0, c0 + MIX_CB)
            yo_ref[rows, cols] = _dot(y_ref[rows, :], wo_ref[:, cols])
        return [functools.partial(piece, c0) for c0 in range(0, D_MODEL, MIX_CB)]

    def norm_pieces(p):
        def piece(r):
            rows = slice(r, r + NORM_RB)
            hout_ref[rows, :] = _layer_norm(ALPHA * h_ref[rows, :] + g1 * yo_ref[rows, :],
                                            lng_ref[...], lnb_ref[...])
        return [functools.partial(piece, r)
                for r in range(p * MIX_RB, (p + 1) * MIX_RB, NORM_RB)]

    cw = cw_ref[...]
    wcat = wcat_ref[...].astype(BF16)
    sbias = sbias_ref[...]
    wa2 = wa2_ref[...]
    ba = ba_ref[...]
    gn = gn_ref[...]
    sg, sb = sg_ref[...], sb_ref[...]
    masks_b = _head_masks(D_B, H_B)
    masks_k = _head_masks(H_C * DK_C, H_C)
    row = lax.broadcasted_iota(jnp.int32, (CHUNK, CHUNK), 0)
    col = lax.broadcasted_iota(jnp.int32, (CHUNK, CHUNK), 1)
    causal = col <= row
    tri = causal.astype(BF16)

    def mix_pieces(c):
        r0 = c * CHUNK
        rows = slice(r0, r0 + CHUNK)
        v = {}

        def conv_a():
            x = z_ref[rows, O_AC:O_AC + D_A] * z_ref[rows, O_AIN:O_AIN + D_A]
            xa_ref[r0 + SUBLANES:r0 + SUBLANES + CHUNK, :] = x
            win = xa_ref[r0:r0 + CHUNK + SUBLANES, :]
            conv = (cw[0:1, :] * win[SUBLANES - 2:SUBLANES - 2 + CHUNK]
                    + cw[1:2, :] * win[SUBLANES - 1:SUBLANES - 1 + CHUNK]
                    + cw[2:3, :] * x)
            y_ref[rows, 0:D_A] = (z_ref[rows, O_AB:O_AB + D_A] * conv).astype(BF16)

        def sgu():
            cv_ref[...] = _sgu_chunk(z_ref, rows, y_ref, wcat, sbias, sg, sb, masks_b)

        def gla_prep():
            lg = _gla_gate(z_ref, rows, wa2, ba)
            b = _dot_exact_lhs(tri, lg)
            b_last = b[CHUNK - 1:CHUNK, :]
            q = z_ref[rows, O_Q:O_Q + H_C * DK_C]
            k = z_ref[rows, O_K:O_K + H_C * DK_C]
            v["qt"] = (q * (DK_C ** -0.5)) * jnp.exp(b)
            v["kt"] = (k * jnp.exp(-b)).astype(BF16)
            v["kh"] = k * jnp.exp(b_last - b)
            st = st_ref[...]
            v["stb"] = st.astype(BF16)
            v["st_new"] = st * jnp.exp(b_last)

        def gla_head(hh):
            zq = jnp.zeros_like(v["qt"])
            qm = jnp.where(masks_k[hh], v["qt"], zq).astype(BF16)
            a = jnp.where(causal, _dot_nt(qm, v["kt"]), 0.0).astype(BF16)
            vb = z_ref[rows, O_V + hh * DV_C:O_V + (hh + 1) * DV_C].astype(BF16)
            o = _dot(a, vb) + _dot_nt(qm, v["stb"])
            _gla_head_out(z_ref, rows, y_ref, hh, o, gn)
            khm = jnp.where(masks_k[hh], v["kh"], zq).astype(BF16)
            v["st_new"] = v["st_new"] + _dot_tn(vb, khm)

        def gla_state():
            st_ref[...] = v["st_new"]

        return ([conv_a, sgu, gla_prep] + [functools.partial(gla_head, hh) for hh in range(H_C)]
                + [gla_state])

    n_rb = ts // MIX_RB
    per = MIX_RB // CHUNK
    _interleave([], in_proj_pieces(0))
    for p in range(n_rb):
        valu = [pc for c in range(p * per, (p + 1) * per) for pc in mix_pieces(c)]
        if p >= 2:
            valu = valu + norm_pieces(p - 2)
        mxu = in_proj_pieces(p + 1) if p + 1 < n_rb else []
        if p >= 1:
            mxu = mxu + out_proj_pieces(p - 1)
        _interleave(valu, mxu)
    _interleave(norm_pieces(n_rb - 2) if n_rb >= 2 else [], out_proj_pieces(n_rb - 1))
    _interleave(norm_pieces(n_rb - 1), [])

    ca_ref[...] = xa_ref[ts + SUBLANES - 2:ts + SUBLANES, :]
    xa_ref[0:SUBLANES, :] = xa_ref[ts:ts + SUBLANES, :]

    @pl.when(j == n_tiles - 1)
    def _():
        gla_ref[...] = st_ref[...].T


def _mixer_prompt_call(l, h, mod, p):
    ts = TS_MIX
    n_tiles = SEQ // ts
    ls = functools.partial(_layer_spec, l=l, n_grid=2)
    kern = functools.partial(_mixer_prompt_kernel, ts=ts, n_tiles=n_tiles)
    return pl.pallas_call(
        kern,
        grid=(BATCH, n_tiles),
        in_specs=[
            pl.BlockSpec((ts, D_MODEL), lambda b, j: (b * n_tiles + j, 0)),
            pl.BlockSpec((None, None, 6, D_MODEL), lambda b, j: (l, b, 0, 0)),
            ls((D_MODEL, N_IN_PAD)),
            ls((3, D_A)),
            ls((1, D_B)),
            ls((1, D_B)),
            ls((CHUNK, H_B * CHUNK)),
            ls((CHUNK, D_B)),
            ls((RANK_PAD, H_C * DK_C)),
            ls((1, H_C * DK_C)),
            ls((1, DV_C)),
            ls((D_MODEL, D_MODEL)),
            ls((1, D_MODEL)),
            ls((1, D_MODEL)),
        ],
        out_specs=[
            pl.BlockSpec((ts, D_MODEL), lambda b, j: (b * n_tiles + j, 0)),
            pl.BlockSpec((None, 2, D_A), lambda b, j: (b, 0, 0)),
            pl.BlockSpec((None, CHUNK, D_B), lambda b, j: (b, 0, 0)),
            pl.BlockSpec((None, H_C * DK_C, DV_C), lambda b, j: (b, 0, 0)),
        ],
        out_shape=[
            jax.ShapeDtypeStruct((BATCH * SEQ, D_MODEL), F32),
            jax.ShapeDtypeStruct((BATCH, 2, D_A), F32),
            jax.ShapeDtypeStruct((BATCH, CHUNK, D_B), F32),
            jax.ShapeDtypeStruct((BATCH, H_C * DK_C, DV_C), F32),
        ],
        scratch_shapes=[
            pltpu.VMEM((ts, D_MODEL), BF16),
            pltpu.VMEM((ts, N_IN_PAD), F32),
            pltpu.VMEM((ts + SUBLANES, D_A), F32),
            pltpu.VMEM((ts, D_MODEL), BF16),
            pltpu.VMEM((DV_C, H_C * DK_C), F32),
            pltpu.VMEM((ts, D_MODEL), F32),
        ],
        compiler_params=pltpu.CompilerParams(
            dimension_semantics=("arbitrary", "arbitrary"), vmem_limit_bytes=VMEM_LIMIT),
        name="mixer_prompt",
    )(h, mod, p["w_in"], p["conv_a_w"], p["sgu_ln_g"], p["sgu_ln_b"], p["wcat_p"], p["sbias_p"],
      p["w_a2"], p["b_a"], p["gla_norm_g"], p["w_o"], p["ln1_g"], p["ln1_b"])


def _expand_rows(x3):
    n, _, c = x3.shape
    return jnp.broadcast_to(x3, (n, DEC_SEQ, c)).reshape(n * DEC_SEQ, c)


def _sample_conv(x, past, cw):
    p0 = _expand_rows(past[:, 0:1, :])
    p1 = _expand_rows(past[:, 1:2, :])
    t_idx = lax.broadcasted_iota(jnp.int32, x.shape, 0) & (DEC_SEQ - 1)
    xm1 = jnp.where(t_idx >= 1, pltpu.roll(x, 1, 0), p1)
    xm2 = jnp.where(t_idx >= 2, pltpu.roll(x, 2, 0), jnp.where(t_idx == 1, p1, p0))
    return cw[0:1, :] * xm2 + cw[1:2, :] * xm1 + cw[2:3, :] * x


def _mixer_sample_kernel(h_ref, mod_ref, past_ref, s0_ref, win_ref, cw_ref, sg_ref, sb_ref,
                         wcat_ref, sbias_ref, wa2_ref, ba_ref, gn_ref, wo_ref, lng_ref, lnb_ref,
                         hout_ref, ca_ref, cv_ref, gla_ref,
                         z_ref, y_ref, qt_ref, kh_ref, eb_ref, oi_ref, *, tb):
    rr = tb * DEC_SEQ
    h = h_ref[...].reshape(rr, D_MODEL)
    sh1 = _expand_rows(mod_ref[:, 0:1, :])
    sc1 = _expand_rows(mod_ref[:, 1:2, :])
    g1 = _expand_rows(mod_ref[:, 2:3, :])
    u = (h * (1.0 + sc1) + sh1).astype(BF16)
    z_ref[...] = _dot(u, win_ref[...])

    x = z_ref[:, O_AC:O_AC + D_A] * z_ref[:, O_AIN:O_AIN + D_A]
    conv = _sample_conv(x, past_ref[...], cw_ref[...])
    y_ref[:, 0:D_A] = (z_ref[:, O_AB:O_AB + D_A] * conv).astype(BF16)
    ca_ref[...] = x.reshape(tb, DEC_SEQ, D_A)[:, DEC_SEQ - 2:DEC_SEQ, :]

    wcat = wcat_ref[...].astype(BF16)
    sbias = sbias_ref[...]
    wa2 = wa2_ref[...]
    ba = ba_ref[...]
    gn = gn_ref[...]
    sg, sb = sg_ref[...], sb_ref[...]
    masks_b = _head_masks(D_B, H_B)
    masks_k = _head_masks(H_C * DK_C, H_C)
    row = lax.broadcasted_iota(jnp.int32, (CHUNK, CHUNK), 0)
    col = lax.broadcasted_iota(jnp.int32, (CHUNK, CHUNK), 1)
    same_seq = (row // DEC_SEQ) == (col // DEC_SEQ)
    causal = same_seq & (col <= row)
    tri = causal.astype(BF16)
    blk = same_seq.astype(BF16)

    for gi in range(rr // CHUNK):
        rows = pl.ds(gi * CHUNK, CHUNK)
        sv = _sgu_chunk(z_ref, rows, y_ref, wcat, sbias, sg, sb, masks_b)
        cv_ref[gi * GROUP_SEQ:(gi + 1) * GROUP_SEQ] = sv.reshape(GROUP_SEQ, DEC_SEQ, D_B)
        lg = _gla_gate(z_ref, rows, wa2, ba)
        b = _dot_exact_lhs(tri, lg)
        b_tot = _dot_exact_lhs(blk, lg)
        q = z_ref[rows, O_Q:O_Q + H_C * DK_C]
        k = z_ref[rows, O_K:O_K + H_C * DK_C]
        qt = (q * (DK_C ** -0.5)) * jnp.exp(b)
        kt = (k * jnp.exp(-b)).astype(BF16)
        qt_ref[rows, :] = qt
        kh_ref[rows, :] = k * jnp.exp(b_tot - b)
        eb_ref[rows, :] = jnp.exp(b_tot)
        zq = jnp.zeros_like(qt)
        for hh in range(H_C):
            qm = jnp.where(masks_k[hh], qt, zq).astype(BF16)
            a = jnp.where(causal, _dot_nt(qm, kt), 0.0).astype(BF16)
            vb = z_ref[rows, O_V + hh * DV_C:O_V + (hh + 1) * DV_C].astype(BF16)
            oi_ref[rows, hh * DV_C:(hh + 1) * DV_C] = _dot(a, vb)

    ones16 = jnp.ones((2 * SUBLANES, DV_C), BF16)
    sub = lax.broadcasted_iota(jnp.int32, (SUBLANES, H_C * DK_C), 0)

    def seq_body(i, carry):
        r0 = pl.multiple_of(i * DEC_SEQ, DEC_SEQ)
        rows = pl.ds(r0, DEC_SEQ)
        s0 = s0_ref[i]
        qt = qt_ref[rows, :]
        kh = kh_ref[rows, :]
        zq = jnp.zeros_like(qt)
        q_exp = jnp.concatenate([jnp.where(m, qt, zq) for m in masks_k], axis=0).astype(BF16)
        k_exp = jnp.concatenate([jnp.where(m, kh, zq) for m in masks_k], axis=0).astype(BF16)
        o_int = _dot(q_exp, s0.astype(BF16))
        v = z_ref[rows, O_V:O_V + D_C]
        v_exp = jnp.concatenate([v[:, hh * DV_C:(hh + 1) * DV_C] for hh in range(H_C)],
                                axis=0).astype(BF16)
        eb = eb_ref[rows, :]
        hi, mid, lo = _split3(eb)
        e3 = jnp.where(sub == 0, hi.astype(F32),
                       jnp.where(sub == 1, mid.astype(F32),
                                 jnp.where(sub == 2, lo.astype(F32), jnp.zeros_like(eb))))
        e3 = jnp.concatenate([e3, jnp.zeros_like(eb)], axis=0).astype(BF16)
        decay = _dot_tn(e3, ones16)
        gla_ref[i] = decay * s0 + _dot_tn(k_exp, v_exp)
        for hh in range(H_C):
            oi_ref[rows, hh * DV_C:(hh + 1) * DV_C] += o_int[hh * DEC_SEQ:(hh + 1) * DEC_SEQ, :]
        return carry

    lax.fori_loop(0, tb, seq_body, 0)

    for gi in range(rr // CHUNK):
        rows = pl.ds(gi * CHUNK, CHUNK)
        for hh in range(H_C):
            _gla_head_out(z_ref, rows, y_ref, hh, oi_ref[rows, hh * DV_C:(hh + 1) * DV_C], gn)

    y = _dot(y_ref[...], wo_ref[...])
    hn = _layer_norm(ALPHA * h + g1 * y, lng_ref[...], lnb_ref[...])
    hout_ref[...] = hn.reshape(tb, DEC_SEQ, D_MODEL)


def _mixer_sample_call(l, h3, mod, past, s0, p):
    tb = TB_S
    ls = functools.partial(_layer_spec, l=l, n_grid=1)
    kern = functools.partial(_mixer_sample_kernel, tb=tb)
    rr = tb * DEC_SEQ
    return pl.pallas_call(
        kern,
        grid=(DEC_BATCH // tb,),
        in_specs=[
            pl.BlockSpec((tb, DEC_SEQ, D_MODEL), lambda i: (i, 0, 0)),
            pl.BlockSpec((None, tb, 6, D_MODEL), lambda i: (l, i, 0, 0)),
            pl.BlockSpec((None, tb, 2, D_A), lambda i: (l, i, 0, 0)),
            pl.BlockSpec((None, tb, H_C * DK_C, DV_C), lambda i: (l, i, 0, 0)),
            ls((D_MODEL, N_IN_PAD)),
            ls((3, D_A)),
            ls((1, D_B)),
            ls((1, D_B)),
            ls((CHUNK, H_B * CHUNK)),
            ls((CHUNK, D_B)),
            ls((RANK_PAD, H_C * DK_C)),
            ls((1, H_C * DK_C)),
            ls((1, DV_C)),
            ls((D_MODEL, D_MODEL)),
            ls((1, D_MODEL)),
            ls((1, D_MODEL)),
        ],
        out_specs=[
            pl.BlockSpec((tb, DEC_SEQ, D_MODEL), lambda i: (i, 0, 0)),
            pl.BlockSpec((tb, 2, D_A), lambda i: (i, 0, 0)),
            pl.BlockSpec((tb, DEC_SEQ, D_B), lambda i: (i, 0, 0)),
            pl.BlockSpec((tb, H_C * DK_C, DV_C), lambda i: (i, 0, 0)),
        ],
        out_shape=[
            jax.ShapeDtypeStruct((DEC_BATCH, DEC_SEQ, D_MODEL), F32),
            jax.ShapeDtypeStruct((DEC_BATCH, 2, D_A), F32),
            jax.ShapeDtypeStruct((DEC_BATCH, DEC_SEQ, D_B), F32),
            jax.ShapeDtypeStruct((DEC_BATCH, H_C * DK_C, DV_C), F32),
        ],
        scratch_shapes=[
            pltpu.VMEM((rr, N_IN_PAD), F32),
            pltpu.VMEM((rr, D_MODEL), BF16),
            pltpu.VMEM((rr, H_C * DK_C), F32),
            pltpu.VMEM((rr, H_C * DK_C), F32),
            pltpu.VMEM((rr, H_C * DK_C), F32),
            pltpu.VMEM((rr, D_C), F32),
        ],
        compiler_params=pltpu.CompilerParams(
            dimension_semantics=("arbitrary",), vmem_limit_bytes=VMEM_LIMIT),
        name="mixer_sample",
    )(h3, mod, past, s0, p["w_in"], p["conv_a_w"], p["sgu_ln_g"], p["sgu_ln_b"], p["wcat_s"],
      p["sbias_s"], p["w_a2"], p["b_a"], p["gla_norm_g"], p["w_o"], p["ln1_g"], p["ln1_b"])


FFN_RB = 32
FFN_CB = 256
N_CB = D_FF // FFN_CB
FFN_AHEAD = 3
FFN_SUB = 256


def _ffn_kernel(*refs, rows, sample):
    if sample:
        (h_ref, mod_ref, past_ref, wup_ref, cw_ref, cb_ref, wdn_ref, lng_ref, lnb_ref,
         hout_ref, st_ref, u_ref, xp_ref, act_ref, acc_ref) = refs
        off = 0
    else:
        (h_ref, mod_ref, wup_ref, cw_ref, cb_ref, wdn_ref, lng_ref, lnb_ref,
         hout_ref, st_ref, u_ref, xp_ref, act_ref, acc_ref) = refs
        off = SUBLANES

        @pl.when(pl.program_id(1) == 0)
        def _():
            xp_ref[0:SUBLANES, :] = jnp.zeros((SUBLANES, 2 * D_FF), F32)

    def rows_of(r):
        if sample:
            seqs = slice(r // DEC_SEQ, (r + FFN_RB) // DEC_SEQ)
            return (h_ref[seqs].reshape(FFN_RB, D_MODEL),
                    [_expand_rows(mod_ref[seqs, k:k + 1, :]) for k in (3, 4, 5)])
        return h_ref[r:r + FFN_RB, :], [mod_ref[k:k + 1, :] for k in (3, 4, 5)]

    def modulate_pieces(s):
        def piece(r):
            hr, (sh2, sc2, _) = rows_of(r)
            u_ref[r:r + FFN_RB, :] = (hr * (1.0 + sc2) + sh2).astype(BF16)
        return [functools.partial(piece, r)
                for r in range(s * FFN_SUB, (s + 1) * FFN_SUB, FFN_RB)]

    n_sub = rows // FFN_SUB
    items = [(s, c) for s in range(n_sub) for c in range(N_CB)]

    def block_cols(c):
        return (slice(c * FFN_CB, (c + 1) * FFN_CB),
                slice(D_FF + c * FFN_CB, D_FF + (c + 1) * FFN_CB))

    def up_pieces(k):
        s, c = items[k]
        r0 = s * FFN_SUB

        def piece(cols):
            xp_ref[off + r0:off + r0 + FFN_SUB, cols] = _dot(u_ref[r0:r0 + FFN_SUB, :], wup_ref[:, cols])
        return [functools.partial(piece, cols) for cols in block_cols(c)]

    def down_pieces(k):
        s, c = items[k]
        r0 = s * FFN_SUB

        def piece(n):
            cols = slice(n * FFN_CB, (n + 1) * FFN_CB)
            d = _dot(act_ref[k % 2], wdn_ref[c * FFN_CB:(c + 1) * FFN_CB, cols])
            if c == 0:
                acc_ref[r0:r0 + FFN_SUB, cols] = d
            else:
                acc_ref[r0:r0 + FFN_SUB, cols] += d
        return [functools.partial(piece, n) for n in range(D_MODEL // FFN_CB)]

    def conv_rows(r, cols):
        if sample:
            seqs = slice(r // DEC_SEQ, (r + FFN_RB) // DEC_SEQ)
            x = xp_ref[r:r + FFN_RB, cols]
            st_ref[seqs, :, cols] = x.reshape(FFN_RB // DEC_SEQ, DEC_SEQ, FFN_CB)[:, DEC_SEQ - 2:, :]
            y = _sample_conv(x, past_ref[seqs, :, cols], cw_ref[:, cols])
        else:
            win = xp_ref[r:r + FFN_RB + SUBLANES, cols]
            y = (cw_ref[0:1, cols] * win[SUBLANES - 2:SUBLANES - 2 + FFN_RB]
                 + cw_ref[1:2, cols] * win[SUBLANES - 1:SUBLANES - 1 + FFN_RB]
                 + cw_ref[2:3, cols] * win[SUBLANES:SUBLANES + FFN_RB])
        return y + cb_ref[0:1, cols]

    def elementwise_pieces(k):
        s, c = items[k]
        ca, cg = block_cols(c)

        def piece(r):
            act_ref[k % 2, r - s * FFN_SUB:r - s * FFN_SUB + FFN_RB, :] = (
                _silu(conv_rows(r, cg)) * conv_rows(r, ca)).astype(BF16)
        return [functools.partial(piece, r)
                for r in range(s * FFN_SUB, (s + 1) * FFN_SUB, FFN_RB)]

    def finish_pieces(s):
        def piece(r):
            hr, (_, _, g2) = rows_of(r)
            hn = _layer_norm(ALPHA * hr + g2 * acc_ref[r:r + FFN_RB, :], lng_ref[...], lnb_ref[...])
            if sample:
                seqs = slice(r // DEC_SEQ, (r + FFN_RB) // DEC_SEQ)
                hout_ref[seqs] = hn.reshape(FFN_RB // DEC_SEQ, DEC_SEQ, D_MODEL)
            else:
                hout_ref[r:r + FFN_RB, :] = hn
        return [functools.partial(piece, r)
                for r in range(s * FFN_SUB, (s + 1) * FFN_SUB, FFN_RB)]

    side = [[] for _ in items]
    for s in range(n_sub):
        if s + 1 < n_sub:
            early = modulate_pieces(s + 1)
            span = N_CB - FFN_AHEAD
            for i, pc in enumerate(early):
                side[s * N_CB + i * span // len(early)].append(pc)
            late = finish_pieces(s)
            for i, pc in enumerate(late):
                side[(s + 1) * N_CB + 1 + i * (N_CB - 1) // len(late)].append(pc)

    n_items = len(items)
    _interleave(modulate_pieces(0), [])
    for k in range(FFN_AHEAD):
        _interleave([], up_pieces(k))
    for k in range(n_items):
        mxu = up_pieces(k + FFN_AHEAD) if k + FFN_AHEAD < n_items else []
        if k >= 1:
            mxu = mxu + down_pieces(k - 1)
        _interleave(elementwise_pieces(k) + side[k], mxu)
    _interleave([], down_pieces(n_items - 1))
    _interleave(finish_pieces(n_sub - 1), [])

    if not sample:
        st_ref[...] = xp_ref[rows + SUBLANES - 2:rows + SUBLANES, :]
        xp_ref[0:SUBLANES, :] = xp_ref[rows:rows + SUBLANES, :]


def _ffn_scratch(rows, off):
    return [
        pltpu.VMEM((rows, D_MODEL), BF16),
        pltpu.VMEM((rows + off, 2 * D_FF), F32),
        pltpu.VMEM((2, FFN_SUB, FFN_CB), BF16),
        pltpu.VMEM((rows, D_MODEL), F32),
    ]


def _ffn_prompt_call(l, h, mod, p):
    ts = TS_FFN
    n_tiles = SEQ // ts
    ls = functools.partial(_layer_spec, l=l, n_grid=2)
    kern = functools.partial(_ffn_kernel, rows=ts, sample=False)
    return pl.pallas_call(
        kern,
        grid=(BATCH, n_tiles),
        in_specs=[
            pl.BlockSpec((ts, D_MODEL), lambda b, j: (b * n_tiles + j, 0)),
            pl.BlockSpec((None, None, 6, D_MODEL), lambda b, j: (l, b, 0, 0)),
            ls((D_MODEL, 2 * D_FF)),
            ls((3, 2 * D_FF)),
            ls((1, 2 * D_FF)),
            ls((D_FF, D_MODEL)),
            ls((1, D_MODEL)),
            ls((1, D_MODEL)),
        ],
        out_specs=[
            pl.BlockSpec((ts, D_MODEL), lambda b, j: (b * n_tiles + j, 0)),
            pl.BlockSpec((None, 2, 2 * D_FF), lambda b, j: (b, 0, 0)),
        ],
        out_shape=[
            jax.ShapeDtypeStruct((BATCH * SEQ, D_MODEL), F32),
            jax.ShapeDtypeStruct((BATCH, 2, 2 * D_FF), F32),
        ],
        scratch_shapes=_ffn_scratch(ts, SUBLANES),
        compiler_params=pltpu.CompilerParams(
            dimension_semantics=("arbitrary", "arbitrary"), vmem_limit_bytes=VMEM_LIMIT),
        name="ffn_prompt",
    )(h, mod, p["w_up"], p["ffn_conv_w"], p["ffn_conv_b"], p["w_down"], p["ln2_g"], p["ln2_b"])


def _ffn_sample_call(l, h3, mod, past, p):
    tb = TB_S
    rr = tb * DEC_SEQ
    ls = functools.partial(_layer_spec, l=l, n_grid=1)
    kern = functools.partial(_ffn_kernel, rows=rr, sample=True)
    return pl.pallas_call(
        kern,
        grid=(DEC_BATCH // tb,),
        in_specs=[
            pl.BlockSpec((tb, DEC_SEQ, D_MODEL), lambda i: (i, 0, 0)),
            pl.BlockSpec((None, tb, 6, D_MODEL), lambda i: (l, i, 0, 0)),
            pl.BlockSpec((None, tb, 2, 2 * D_FF), lambda i: (l, i, 0, 0)),
            ls((D_MODEL, 2 * D_FF)),
            ls((3, 2 * D_FF)),
            ls((1, 2 * D_FF)),
            ls((D_FF, D_MODEL)),
            ls((1, D_MODEL)),
            ls((1, D_MODEL)),
        ],
        out_specs=[
            pl.BlockSpec((tb, DEC_SEQ, D_MODEL), lambda i: (i, 0, 0)),
            pl.BlockSpec((tb, 2, 2 * D_FF), lambda i: (i, 0, 0)),
        ],
        out_shape=[
            jax.ShapeDtypeStruct((DEC_BATCH, DEC_SEQ, D_MODEL), F32),
            jax.ShapeDtypeStruct((DEC_BATCH, 2, 2 * D_FF), F32),
        ],
        scratch_shapes=_ffn_scratch(rr, 0),
        compiler_params=pltpu.CompilerParams(
            dimension_semantics=("arbitrary",), vmem_limit_bytes=VMEM_LIMIT),
        name="ffn_sample",
    )(h3, mod, past, p["w_up"], p["ffn_conv_w"], p["ffn_conv_b"], p["w_down"], p["ln2_g"], p["ln2_b"])


def _prep_params(w_in, conv_a_w, sgu_ln_g, sgu_ln_b, sgu_w, sgu_b, gla_w_a2, gla_b_a, gla_norm_g,
                 w_o, ln1_g, ln1_b, ffn_w_up, ffn_conv_w, ffn_conv_b, ffn_w_down, ln2_g, ln2_b):
    tril = jnp.tril(jnp.ones((CHUNK, CHUNK), bool))
    w_p = jnp.where(tril[None, None], sgu_w, 0.0)
    wcat_p = jnp.transpose(w_p, (0, 2, 1, 3)).reshape(DEPTH, CHUNK, H_B * CHUNK)
    tril8 = jnp.tril(jnp.ones((DEC_SEQ, DEC_SEQ), bool))
    w8 = jnp.where(tril8[None, None], sgu_w[:, :, :DEC_SEQ, :DEC_SEQ], 0.0)
    eye = jnp.eye(GROUP_SEQ, dtype=F32)
    w_blk = jnp.einsum("ij,lhts->lhitjs", eye, w8).reshape(DEPTH, H_B, CHUNK, CHUNK)
    wcat_s = jnp.transpose(w_blk, (0, 2, 1, 3)).reshape(DEPTH, CHUNK, H_B * CHUNK)
    sbias_p = jnp.repeat(jnp.transpose(sgu_b, (0, 2, 1)), DH_B, axis=2)
    sbias_s = jnp.tile(sbias_p[:, :DEC_SEQ, :], (1, GROUP_SEQ, 1))
    return {
        "w_in": jnp.pad(w_in, ((0, 0), (0, 0), (0, N_IN_PAD - N_IN))).astype(BF16),
        "conv_a_w": conv_a_w,
        "sgu_ln_g": sgu_ln_g.reshape(DEPTH, 1, D_B),
        "sgu_ln_b": sgu_ln_b.reshape(DEPTH, 1, D_B),
        "wcat_p": wcat_p, "wcat_s": wcat_s, "sbias_p": sbias_p, "sbias_s": sbias_s,
        "w_a2": jnp.pad(gla_w_a2, ((0, 0), (0, RANK_PAD - GATE_RANK), (0, 0))).astype(BF16),
        "b_a": gla_b_a.reshape(DEPTH, 1, H_C * DK_C),
        "gla_norm_g": gla_norm_g.reshape(DEPTH, 1, DV_C),
        "w_o": w_o.astype(BF16),
        "ln1_g": ln1_g.reshape(DEPTH, 1, D_MODEL), "ln1_b": ln1_b.reshape(DEPTH, 1, D_MODEL),
        "w_up": ffn_w_up.astype(BF16),
        "ffn_conv_w": ffn_conv_w,
        "ffn_conv_b": ffn_conv_b.reshape(DEPTH, 1, 2 * D_FF),
        "w_down": ffn_w_down.astype(BF16),
        "ln2_g": ln2_g.reshape(DEPTH, 1, D_MODEL), "ln2_b": ln2_b.reshape(DEPTH, 1, D_MODEL),
    }


def kernel(x_prompt, x_sample, c_prompt, c_sample, state_conv_a, state_gla, state_ffn_conv,
           ln_in_g, ln_in_b, w_ada, b_ada, w_in, conv_a_w, sgu_ln_g, sgu_ln_b, sgu_w, sgu_b,
           gla_w_a2, gla_b_a, gla_norm_g, w_o, ln1_g, ln1_b, ffn_w_up, ffn_conv_w, ffn_conv_b,
           ffn_w_down, ln2_g, ln2_b):
    p = _prep_params(w_in, conv_a_w, sgu_ln_g, sgu_ln_b, sgu_w, sgu_b, gla_w_a2, gla_b_a,
                     gla_norm_g, w_o, ln1_g, ln1_b, ffn_w_up, ffn_conv_w, ffn_conv_b,
                     ffn_w_down, ln2_g, ln2_b)
    mod = _ada_call(jnp.concatenate([c_prompt, c_sample], axis=0), w_ada, b_ada)
    mod = mod.reshape(DEPTH, BATCH + DEC_BATCH, 6, D_MODEL)
    mod_p, mod_s = mod[:, :BATCH], mod[:, BATCH:]
    s0_all = state_gla.reshape(DEPTH, DEC_BATCH, H_C * DK_C, DV_C)

    hp = _ln_call(x_prompt.reshape(BATCH * SEQ, D_MODEL), ln_in_g, ln_in_b)
    hs = _ln_call(x_sample.reshape(DEC_BATCH * DEC_SEQ, D_MODEL), ln_in_g, ln_in_b)
    hs = hs.reshape(DEC_BATCH, DEC_SEQ, D_MODEL)

    outs_p = [[], [], [], []]
    outs_s = [[], [], [], []]
    for l in range(DEPTH):
        hp, ca, cv, gs = _mixer_prompt_call(l, hp, mod_p, p)
        hp, fs = _ffn_prompt_call(l, hp, mod_p, p)
        for acc, val in zip(outs_p, (ca, cv, gs, fs)):
            acc.append(val)
        hs, ca, cv, gs = _mixer_sample_call(l, hs, mod_s, state_conv_a, s0_all, p)
        hs, fs = _ffn_sample_call(l, hs, mod_s, state_ffn_conv, p)
        for acc, val in zip(outs_s, (ca, cv, gs, fs)):
            acc.append(val)

    ca_p, cv_p, gla_p, ffn_p = [jnp.stack(v) for v in outs_p]
    ca_s, cv_s, gla_s, ffn_s = [jnp.stack(v) for v in outs_s]
    return (hp.reshape(BATCH, SEQ, D_MODEL), hs,
            ca_p, ca_s, cv_p, cv_s,
            gla_p.reshape(DEPTH, BATCH, H_C, DK_C, DV_C),
            gla_s.reshape(DEPTH, DEC_BATCH, H_C, DK_C, DV_C),
            ffn_p, ffn_s)
```

```python
import functools
import math

import jax
import jax.numpy as jnp
from jax import lax
from jax.experimental import pallas as pl
from jax.experimental.pallas import tpu as pltpu

D_MODEL = 1024
BATCH = 8
SEQ = 2048
DEPTH = 4
DEC_BATCH = 128
DEC_SEQ = 8
D_A = 256
D_B = 256
H_B = 4
DH_B = 64
D_C = 512
H_C = 4
DV_C = 128
DK_C = 64
GATE_RANK = 16
GATE_TAU = 16.0
CHUNK = 128
D_FF = 2816
LN_EPS = 1e-5
ALPHA = (2.0 * DEPTH) ** 0.25
N_IN = 2832
N_IN_PAD = 2944
RANK_PAD = 128

O_AIN, O_AB, O_AC, O_SU, O_SV, O_Q, O_K, O_V, O_G, O_R = (
    0, 256, 512, 768, 1024, 1280, 1536, 1792, 2304, 2816)

SUBLANES = 8
VMEM_LIMIT = 56 * 1024 * 1024

TS_MIX = 1024
MIX_RB = 256
MIX_CB = 256
NORM_RB = 32
TS_FFN = 256
TB_S = 32
GROUP_SEQ = CHUNK // DEC_SEQ
SEQ_UNROLL = 4

F32 = jnp.float32
BF16 = jnp.bfloat16


def _dot(a, b):
    return jnp.dot(a, b, preferred_element_type=F32)


def _dot_nt(a, b):
    return lax.dot_general(a, b, (((1,), (1,)), ((), ())), preferred_element_type=F32)


def _dot_tn(a, b):
    return lax.dot_general(a, b, (((0,), (0,)), ((), ())), preferred_element_type=F32)


def _layer_norm(x, g, b):
    mu = jnp.mean(x, axis=-1, keepdims=True)
    xc = x - mu
    var = jnp.mean(xc * xc, axis=-1, keepdims=True)
    return xc * lax.rsqrt(var + LN_EPS) * g + b


def _gelu(x):
    c = math.sqrt(2.0 / math.pi)
    return 0.5 * x * (1.0 + jnp.tanh(c * (x + 0.044715 * (x * x * x))))


def _silu(x):
    return x * jax.nn.sigmoid(x)


def _log_sigmoid(x):
    return jnp.minimum(x, 0.0) - jnp.log1p(jnp.exp(-jnp.abs(x)))


def _split3(x):
    hi = x.astype(BF16)
    r1 = x - hi.astype(F32)
    mid = r1.astype(BF16)
    lo = (r1 - mid.astype(F32)).astype(BF16)
    return hi, mid, lo


def _dot_exact_lhs(m_bf16, x):
    hi, mid, lo = _split3(x)
    return _dot(m_bf16, hi) + _dot(m_bf16, mid) + _dot(m_bf16, lo)


def _head_masks(width, n_heads):
    lane = lax.broadcasted_iota(jnp.int32, (1, width), 1)
    per = width // n_heads
    return [(lane >= h * per) & (lane < (h + 1) * per) for h in range(n_heads)]


def _interleave(valu_pieces, mxu_pieces):
    n_v, n_m = len(valu_pieces), len(mxu_pieces)
    done = 0
    for k, mp in enumerate(mxu_pieces):
        upto = (k + 1) * n_v // (n_m + 1)
        for vp in valu_pieces[done:upto]:
            vp()
        done = upto
        mp()
    for vp in valu_pieces[done:]:
        vp()


def _ada_kernel(c_ref, w_ref, b_ref, o_ref):
    cs = _silu(c_ref[...]).astype(BF16)
    o_ref[...] = _dot(cs, w_ref[...].astype(BF16)) + b_ref[...]


def _ada_call(c_all, w_ada, b_ada):
    n_rows = c_all.shape[0]
    tn = 1536
    return pl.pallas_call(
        _ada_kernel,
        grid=(DEPTH, 6 * D_MODEL // tn),
        in_specs=[
            pl.BlockSpec((n_rows, D_MODEL), lambda l, j: (0, 0)),
            pl.BlockSpec((None, D_MODEL, tn), lambda l, j: (l, 0, j)),
            pl.BlockSpec((None, 1, tn), lambda l, j: (l, 0, j)),
        ],
        out_specs=pl.BlockSpec((None, n_rows, tn), lambda l, j: (l, 0, j)),
        out_shape=jax.ShapeDtypeStruct((DEPTH, n_rows, 6 * D_MODEL), F32),
        compiler_params=pltpu.CompilerParams(
            dimension_semantics=("arbitrary", "arbitrary"), vmem_limit_bytes=VMEM_LIMIT),
        name="ada_mod",
    )(c_all, w_ada, b_ada.reshape(DEPTH, 1, 6 * D_MODEL))


def _sgu_chunk(z_ref, rows, y_ref, wcat, sbias, ln_g, ln_b, masks_b):
    su = _gelu(z_ref[rows, O_SU:O_SU + D_B])
    sv = _layer_norm(_gelu(z_ref[rows, O_SV:O_SV + D_B]), ln_g, ln_b)
    svb = sv.astype(BF16)
    zero = jnp.zeros_like(svb)
    svm = jnp.concatenate([jnp.where(m, svb, zero) for m in masks_b], axis=0)
    mixed = _dot(wcat, svm) + sbias
    y_ref[rows, D_A:D_A + D_B] = (su * mixed).astype(BF16)
    return sv


def _gla_gate(z_ref, rows, wa2, ba):
    cr = z_ref[rows, O_R:O_R + RANK_PAD].astype(BF16)
    return _log_sigmoid(_dot(cr, wa2) + ba) * (1.0 / GATE_TAU)


def _gla_head_out(z_ref, rows, y_ref, hh, o, gn):
    ms = jnp.mean(o * o, axis=-1, keepdims=True)
    on = o * lax.rsqrt(ms + LN_EPS) * gn
    g = z_ref[rows, O_G + hh * DV_C:O_G + (hh + 1) * DV_C]
    y_ref[rows, D_A + D_B + hh * DV_C:D_A + D_B + (hh + 1) * DV_C] = (_silu(g) * on).astype(BF16)


def _layer_spec(shape, l, n_grid):
    idx = (l,) + (0,) * len(shape)
    if n_grid == 1:
        return pl.BlockSpec((None,) + shape, lambda i: idx, pipeline_mode=pl.Buffered(1))
    return pl.BlockSpec((None,) + shape, lambda b, j: idx, pipeline_mode=pl.Buffered(1))


def _mixer_prompt_kernel(*refs, ts, n_tiles, first):
    if first:
        (x_ref, mod_ref, lnig_ref, lnib_ref, win_ref, cw_ref, sg_ref, sb_ref, wcat_ref, sbias_ref,
         wa2_ref, ba_ref, gn_ref, wo_ref, lng_ref, lnb_ref,
         hout_ref, ca_ref, cv_ref, gla_ref,
         u_ref, z_ref, xa_ref, y_ref, st_ref, yo_ref, h_ref) = refs
        for r in range(0, ts, NORM_RB):
            h_ref[r:r + NORM_RB, :] = _layer_norm(x_ref[r:r + NORM_RB, :], lnig_ref[...], lnib_ref[...])
    else:
        (h_ref, mod_ref, win_ref, cw_ref, sg_ref, sb_ref, wcat_ref, sbias_ref,
         wa2_ref, ba_ref, gn_ref, wo_ref, lng_ref, lnb_ref,
         hout_ref, ca_ref, cv_ref, gla_ref,
         u_ref, z_ref, xa_ref, y_ref, st_ref, yo_ref) = refs
    j = pl.program_id(1)

    @pl.when(j == 0)
    def _():
        xa_ref[0:SUBLANES, :] = jnp.zeros((SUBLANES, D_A), F32)
        st_ref[...] = jnp.zeros_like(st_ref)

    sh1, sc1, g1 = mod_ref[0:1, :], mod_ref[1:2, :], mod_ref[2:3, :]
    for r in range(0, ts, NORM_RB):
        u_ref[r:r + NORM_RB, :] = (h_ref[r:r + NORM_RB, :] * (1.0 + sc1) + sh1).astype(BF16)

    def in_proj_pieces(p):
        rows = slice(p * MIX_RB, (p + 1) * MIX_RB)

        def piece(c0):
            cols = slice(c0, min(c0 + MIX_CB, N_IN_PAD))
            z_ref[rows, cols] = _dot(u_ref[rows, :], win_ref[:, cols])
        return [functools.partial(piece, c0) for c0 in range(0, N_IN_PAD, MIX_CB)]

    def out_proj_pieces(p):
        rows = slice(p * MIX_RB, (p + 1) * MIX_RB)

        def piece(c0):
            cols = slice(c0, c0 + MIX_CB)
            yo_ref[rows, cols] = _dot(y_ref[rows, :], wo_ref[:, cols])
        return [functools.partial(piece, c0) for c0 in range(0, D_MODEL, MIX_CB)]

    def norm_pieces(p):
        def piece(r):
            rows = slice(r, r + NORM_RB)
            hout_ref[rows, :] = _layer_norm(ALPHA * h_ref[rows, :] + g1 * yo_ref[rows, :],
                                            lng_ref[...], lnb_ref[...])
        return [functools.partial(piece, r)
                for r in range(p * MIX_RB, (p + 1) * MIX_RB, NORM_RB)]

    cw = cw_ref[...]
    wcat = wcat_ref[...].astype(BF16)
    sbias = sbias_ref[...]
    wa2 = wa2_ref[...]
    ba = ba_ref[...]
    gn = gn_ref[...]
    sg, sb = sg_ref[...], sb_ref[...]
    masks_b = _head_masks(D_B, H_B)
    masks_k = _head_masks(H_C * DK_C, H_C)
    row = lax.broadcasted_iota(jnp.int32, (CHUNK, CHUNK), 0)
    col = lax.broadcasted_iota(jnp.int32, (CHUNK, CHUNK), 1)
    causal = col <= row
    tri = causal.astype(BF16)

    def mix_pieces(c):
        r0 = c * CHUNK
        rows = slice(r0, r0 + CHUNK)
        v = {}

        def conv_a():
            x = z_ref[rows, O_AC:O_AC + D_A] * z_ref[rows, O_AIN:O_AIN + D_A]
            xa_ref[r0 + SUBLANES:r0 + SUBLANES + CHUNK, :] = x
            win = xa_ref[r0:r0 + CHUNK + SUBLANES, :]
            conv = (cw[0:1, :] * win[SUBLANES - 2:SUBLANES - 2 + CHUNK]
                    + cw[1:2, :] * win[SUBLANES - 1:SUBLANES - 1 + CHUNK]
                    + cw[2:3, :] * x)
            y_ref[rows, 0:D_A] = (z_ref[rows, O_AB:O_AB + D_A] * conv).astype(BF16)

        def sgu():
            cv_ref[...] = _sgu_chunk(z_ref, rows, y_ref, wcat, sbias, sg, sb, masks_b)

        def gla_prep():
            lg = _gla_gate(z_ref, rows, wa2, ba)
            b = _dot_exact_lhs(tri, lg)
            b_last = b[CHUNK - 1:CHUNK, :]
            b_mid = b[CHUNK // 2 - 1:CHUNK // 2, :]
            q = z_ref[rows, O_Q:O_Q + H_C * DK_C]
            k = z_ref[rows, O_K:O_K + H_C * DK_C]
            v["qt"] = (q * (DK_C ** -0.5)) * jnp.exp(b - b_mid)
            v["kt"] = (k * jnp.exp(b_mid - b)).astype(BF16)
            v["kh"] = k * jnp.exp(b_last - b)
            st = st_ref[...]
            v["stb"] = (st * jnp.exp(b_mid)).astype(BF16)
            v["st_new"] = st * jnp.exp(b_last)

        def gla_head(hh):
            zq = jnp.zeros_like(v["qt"])
            qm = jnp.where(masks_k[hh], v["qt"], zq).astype(BF16)
            a = jnp.where(causal, _dot_nt(qm, v["kt"]), 0.0).astype(BF16)
            vb = z_ref[rows, O_V + hh * DV_C:O_V + (hh + 1) * DV_C].astype(BF16)
            o = _dot(a, vb) + _dot_nt(qm, v["stb"])
            _gla_head_out(z_ref, rows, y_ref, hh, o, gn)
            khm = jnp.where(masks_k[hh], v["kh"], zq).astype(BF16)
            v["st_new"] = v["st_new"] + _dot_tn(vb, khm)

        def gla_state():
            st_ref[...] = v["st_new"]

        return ([conv_a, sgu, gla_prep] + [functools.partial(gla_head, hh) for hh in range(H_C)]
                + [gla_state])

    n_rb = ts // MIX_RB
    per = MIX_RB // CHUNK
    _interleave([], in_proj_pieces(0))
    for p in range(n_rb):
        valu = [pc for c in range(p * per, (p + 1) * per) for pc in mix_pieces(c)]
        if p >= 2:
            valu = valu + norm_pieces(p - 2)
        mxu = in_proj_pieces(p + 1) if p + 1 < n_rb else []
        if p >= 1:
            mxu = mxu + out_proj_pieces(p - 1)
        _interleave(valu, mxu)
    _interleave(norm_pieces(n_rb - 2) if n_rb >= 2 else [], out_proj_pieces(n_rb - 1))
    _interleave(norm_pieces(n_rb - 1), [])

    ca_ref[...] = xa_ref[ts + SUBLANES - 2:ts + SUBLANES, :]
    xa_ref[0:SUBLANES, :] = xa_ref[ts:ts + SUBLANES, :]

    @pl.when(j == n_tiles - 1)
    def _():
        gla_ref[...] = st_ref[...].T


def _mixer_prompt_call(l, h, mod, p):
    ts = TS_MIX
    n_tiles = SEQ // ts
    first = l == 0
    ls = functools.partial(_layer_spec, l=l, n_grid=2)
    kern = functools.partial(_mixer_prompt_kernel, ts=ts, n_tiles=n_tiles, first=first)
    ln_in_specs = [pl.BlockSpec((1, D_MODEL), lambda b, j: (0, 0))] * 2 if first else []
    ln_in_args = (p["ln_in_g"], p["ln_in_b"]) if first else ()
    ln_in_scratch = [pltpu.VMEM((ts, D_MODEL), F32)] if first else []
    return pl.pallas_call(
        kern,
        grid=(BATCH, n_tiles),
        in_specs=[
            pl.BlockSpec((ts, D_MODEL), lambda b, j: (b * n_tiles + j, 0)),
            pl.BlockSpec((None, None, 6, D_MODEL), lambda b, j: (l, b, 0, 0)),
            *ln_in_specs,
            ls((D_MODEL, N_IN_PAD)),
            ls((3, D_A)),
            ls((1, D_B)),
            ls((1, D_B)),
            ls((CHUNK, H_B * CHUNK)),
            ls((CHUNK, D_B)),
            ls((RANK_PAD, H_C * DK_C)),
            ls((1, H_C * DK_C)),
            ls((1, DV_C)),
            ls((D_MODEL, D_MODEL)),
            ls((1, D_MODEL)),
            ls((1, D_MODEL)),
        ],
        out_specs=[
            pl.BlockSpec((ts, D_MODEL), lambda b, j: (b * n_tiles + j, 0)),
            pl.BlockSpec((None, 2, D_A), lambda b, j: (b, 0, 0)),
            pl.BlockSpec((None, CHUNK, D_B), lambda b, j: (b, 0, 0)),
            pl.BlockSpec((None, H_C * DK_C, DV_C), lambda b, j: (b, 0, 0)),
        ],
        out_shape=[
            jax.ShapeDtypeStruct((BATCH * SEQ, D_MODEL), F32),
            jax.ShapeDtypeStruct((BATCH, 2, D_A), F32),
            jax.ShapeDtypeStruct((BATCH, CHUNK, D_B), F32),
            jax.ShapeDtypeStruct((BATCH, H_C * DK_C, DV_C), F32),
        ],
        scratch_shapes=[
            pltpu.VMEM((ts, D_MODEL), BF16),
            pltpu.VMEM((ts, N_IN_PAD), F32),
            pltpu.VMEM((ts + SUBLANES, D_A), F32),
            pltpu.VMEM((ts, D_MODEL), BF16),
            pltpu.VMEM((DV_C, H_C * DK_C), F32),
            pltpu.VMEM((ts, D_MODEL), F32),
            *ln_in_scratch,
        ],
        compiler_params=pltpu.CompilerParams(
            dimension_semantics=("arbitrary", "arbitrary"), vmem_limit_bytes=VMEM_LIMIT),
        name="mixer_prompt",
    )(h, mod, *ln_in_args, p["w_in"], p["conv_a_w"], p["sgu_ln_g"], p["sgu_ln_b"], p["wcat_p"], p["sbias_p"],
      p["w_a2"], p["b_a"], p["gla_norm_g"], p["w_o"], p["ln1_g"], p["ln1_b"])


def _expand_rows(x3):
    n, _, c = x3.shape
    return jnp.broadcast_to(x3, (n, DEC_SEQ, c)).reshape(n * DEC_SEQ, c)


def _sample_conv(x, past, cw):
    p0 = _expand_rows(past[:, 0:1, :])
    p1 = _expand_rows(past[:, 1:2, :])
    t_idx = lax.broadcasted_iota(jnp.int32, x.shape, 0) & (DEC_SEQ - 1)
    xm1 = jnp.where(t_idx >= 1, pltpu.roll(x, 1, 0), p1)
    xm2 = jnp.where(t_idx >= 2, pltpu.roll(x, 2, 0), jnp.where(t_idx == 1, p1, p0))
    return cw[0:1, :] * xm2 + cw[1:2, :] * xm1 + cw[2:3, :] * x


def _mixer_sample_kernel(*refs, tb, first):
    if first:
        (h_ref, mod_ref, lnig_ref, lnib_ref, past_ref, s0_ref, win_ref, cw_ref, sg_ref, sb_ref,
         wcat_ref, sbias_ref, wa2_ref, ba_ref, gn_ref, wo_ref, lng_ref, lnb_ref,
         hout_ref, ca_ref, cv_ref, gla_ref,
         z_ref, y_ref, qt_ref, kh_ref, eb_ref, oi_ref, os_ref) = refs
    else:
        (h_ref, mod_ref, past_ref, s0_ref, win_ref, cw_ref, sg_ref, sb_ref,
         wcat_ref, sbias_ref, wa2_ref, ba_ref, gn_ref, wo_ref, lng_ref, lnb_ref,
         hout_ref, ca_ref, cv_ref, gla_ref,
         z_ref, y_ref, qt_ref, kh_ref, eb_ref, oi_ref, os_ref) = refs
    rr = tb * DEC_SEQ
    h = h_ref[...].reshape(rr, D_MODEL)
    if first:
        h = _layer_norm(h, lnig_ref[...], lnib_ref[...])
    sh1 = _expand_rows(mod_ref[:, 0:1, :])
    sc1 = _expand_rows(mod_ref[:, 1:2, :])
    g1 = _expand_rows(mod_ref[:, 2:3, :])
    u = (h * (1.0 + sc1) + sh1).astype(BF16)
    z_ref[...] = _dot(u, win_ref[...])

    x = z_ref[:, O_AC:O_AC + D_A] * z_ref[:, O_AIN:O_AIN + D_A]
    conv = _sample_conv(x, past_ref[...], cw_ref[...])
    y_ref[:, 0:D_A] = (z_ref[:, O_AB:O_AB + D_A] * conv).astype(BF16)
    ca_ref[...] = x.reshape(tb, DEC_SEQ, D_A)[:, DEC_SEQ - 2:DEC_SEQ, :]

    wcat = wcat_ref[...].astype(BF16)
    sbias = sbias_ref[...]
    wa2 = wa2_ref[...]
    ba = ba_ref[...]
    gn = gn_ref[...]
    sg, sb = sg_ref[...], sb_ref[...]
    masks_b = _head_masks(D_B, H_B)
    masks_k = _head_masks(H_C * DK_C, H_C)
    row = lax.broadcasted_iota(jnp.int32, (CHUNK, CHUNK), 0)
    col = lax.broadcasted_iota(jnp.int32, (CHUNK, CHUNK), 1)
    same_seq = (row // DEC_SEQ) == (col // DEC_SEQ)
    causal = same_seq & (col <= row)
    tri = causal.astype(BF16)
    blk = same_seq.astype(BF16)

    for gi in range(rr // CHUNK):
        rows = pl.ds(gi * CHUNK, CHUNK)
        sv = _sgu_chunk(z_ref, rows, y_ref, wcat, sbias, sg, sb, masks_b)
        cv_ref[gi * GROUP_SEQ:(gi + 1) * GROUP_SEQ] = sv.reshape(GROUP_SEQ, DEC_SEQ, D_B)
        lg = _gla_gate(z_ref, rows, wa2, ba)
        b = _dot_exact_lhs(tri, lg)
        b_tot = _dot_exact_lhs(blk, lg)
        q = z_ref[rows, O_Q:O_Q + H_C * DK_C]
        k = z_ref[rows, O_K:O_K + H_C * DK_C]
        qt = (q * (DK_C ** -0.5)) * jnp.exp(b)
        kt = (k * jnp.exp(-b)).astype(BF16)
        qt_ref[rows, :] = qt
        kh_ref[rows, :] = k * jnp.exp(b_tot - b)
        eb_ref[rows, :] = jnp.exp(b_tot)
        zq = jnp.zeros_like(qt)
        for hh in range(H_C):
            qm = jnp.where(masks_k[hh], qt, zq).astype(BF16)
            a = jnp.where(causal, _dot_nt(qm, kt), 0.0).astype(BF16)
            vb = z_ref[rows, O_V + hh * DV_C:O_V + (hh + 1) * DV_C].astype(BF16)
            oi_ref[rows, hh * DV_C:(hh + 1) * DV_C] = _dot(a, vb)

    ones16 = jnp.ones((2 * SUBLANES, DV_C), BF16)
    sub = lax.broadcasted_iota(jnp.int32, (SUBLANES, H_C * DK_C), 0)

    def seq_body(i, carry):
        r0 = pl.multiple_of(i * DEC_SEQ, DEC_SEQ)
        rows = pl.ds(r0, DEC_SEQ)
        s0 = s0_ref[i]
        qt = qt_ref[rows, :]
        kh = kh_ref[rows, :]
        zq = jnp.zeros_like(qt)
        q_exp = jnp.concatenate([jnp.where(m, qt, zq) for m in masks_k], axis=0).astype(BF16)
        k_exp = jnp.concatenate([jnp.where(m, kh, zq) for m in masks_k], axis=0).astype(BF16)
        o_int = _dot(q_exp, s0.astype(BF16))
        v = z_ref[rows, O_V:O_V + D_C]
        v_exp = jnp.concatenate([v[:, hh * DV_C:(hh + 1) * DV_C] for hh in range(H_C)],
                                axis=0).astype(BF16)
        eb = eb_ref[rows, :]
        hi, mid, lo = _split3(eb)
        e3 = jnp.where(sub == 0, hi.astype(F32),
                       jnp.where(sub == 1, mid.astype(F32),
                                 jnp.where(sub == 2, lo.astype(F32), jnp.zeros_like(eb))))
        e3 = jnp.concatenate([e3, jnp.zeros_like(eb)], axis=0).astype(BF16)
        decay = _dot_tn(e3, ones16)
        gla_ref[i] = decay * s0 + _dot_tn(k_exp, v_exp)
        for hh in range(H_C):
            os_ref[rows, hh * DV_C:(hh + 1) * DV_C] = o_int[hh * DEC_SEQ:(hh + 1) * DEC_SEQ, :]
        return carry

    lax.fori_loop(0, tb, seq_body, 0, unroll=SEQ_UNROLL)

    for gi in range(rr // CHUNK):
        rows = pl.ds(gi * CHUNK, CHUNK)
        for hh in range(H_C):
            cols = slice(hh * DV_C, (hh + 1) * DV_C)
            _gla_head_out(z_ref, rows, y_ref, hh, oi_ref[rows, cols] + os_ref[rows, cols], gn)

    y = _dot(y_ref[...], wo_ref[...])
    hn = _layer_norm(ALPHA * h + g1 * y, lng_ref[...], lnb_ref[...])
    hout_ref[...] = hn.reshape(tb, DEC_SEQ, D_MODEL)


def _mixer_sample_call(l, h3, mod, past, s0, p):
    tb = TB_S
    ls = functools.partial(_layer_spec, l=l, n_grid=1)
    first = l == 0
    kern = functools.partial(_mixer_sample_kernel, tb=tb, first=first)
    ln_in_specs = [pl.BlockSpec((1, D_MODEL), lambda i: (0, 0))] * 2 if first else []
    ln_in_args = (p["ln_in_g"], p["ln_in_b"]) if first else ()
    rr = tb * DEC_SEQ
    return pl.pallas_call(
        kern,
        grid=(DEC_BATCH // tb,),
        in_specs=[
            pl.BlockSpec((tb, DEC_SEQ, D_MODEL), lambda i: (i, 0, 0)),
            pl.BlockSpec((None, tb, 6, D_MODEL), lambda i: (l, i, 0, 0)),
            *ln_in_specs,
            pl.BlockSpec((None, tb, 2, D_A), lambda i: (l, i, 0, 0)),
            pl.BlockSpec((None, tb, H_C * DK_C, DV_C), lambda i: (l, i, 0, 0)),
            ls((D_MODEL, N_IN_PAD)),
            ls((3, D_A)),
            ls((1, D_B)),
            ls((1, D_B)),
            ls((CHUNK, H_B * CHUNK)),
            ls((CHUNK, D_B)),
            ls((RANK_PAD, H_C * DK_C)),
            ls((1, H_C * DK_C)),
            ls((1, DV_C)),
            ls((D_MODEL, D_MODEL)),
            ls((1, D_MODEL)),
            ls((1, D_MODEL)),
        ],
        out_specs=[
            pl.BlockSpec((tb, DEC_SEQ, D_MODEL), lambda i: (i, 0, 0)),
            pl.BlockSpec((tb, 2, D_A), lambda i: (i, 0, 0)),
            pl.BlockSpec((tb, DEC_SEQ, D_B), lambda i: (i, 0, 0)),
            pl.BlockSpec((tb, H_C * DK_C, DV_C), lambda i: (i, 0, 0)),
        ],
        out_shape=[
            jax.ShapeDtypeStruct((DEC_BATCH, DEC_SEQ, D_MODEL), F32),
            jax.ShapeDtypeStruct((DEC_BATCH, 2, D_A), F32),
            jax.ShapeDtypeStruct((DEC_BATCH, DEC_SEQ, D_B), F32),
            jax.ShapeDtypeStruct((DEC_BATCH, H_C * DK_C, DV_C), F32),
        ],
        scratch_shapes=[
            pltpu.VMEM((rr, N_IN_PAD), F32),
            pltpu.VMEM((rr, D_MODEL), BF16),
            pltpu.VMEM((rr, H_C * DK_C), F32),
            pltpu.VMEM((rr, H_C * DK_C), F32),
            pltpu.VMEM((rr, H_C * DK_C), F32),
            pltpu.VMEM((rr, D_C), F32),
            pltpu.VMEM((rr, D_C), F32),
        ],
        compiler_params=pltpu.CompilerParams(
            dimension_semantics=("arbitrary",), vmem_limit_bytes=VMEM_LIMIT),
        name="mixer_sample",
    )(h3, mod, *ln_in_args, past, s0, p["w_in"], p["conv_a_w"], p["sgu_ln_g"], p["sgu_ln_b"], p["wcat_s"],
      p["sbias_s"], p["w_a2"], p["b_a"], p["gla_norm_g"], p["w_o"], p["ln1_g"], p["ln1_b"])


FFN_RB = 32
FFN_CB = 256
N_CB = D_FF // FFN_CB
FFN_AHEAD = 3
FFN_AHEAD_EXTRA = 4
FFN_SUB = 256


def _ffn_kernel(*refs, rows, sample):
    if sample:
        (h_ref, mod_ref, past_ref, wup_ref, cw_ref, cb_ref, wdn_ref, lng_ref, lnb_ref,
         hout_ref, st_ref, u_ref, xp_ref, act_ref, acc_ref) = refs
        off = 0
    else:
        (h_ref, mod_ref, wup_ref, cw_ref, cb_ref, wdn_ref, lng_ref, lnb_ref,
         hout_ref, st_ref, u_ref, xp_ref, act_ref, acc_ref) = refs
        off = SUBLANES

        @pl.when(pl.program_id(1) == 0)
        def _():
            xp_ref[0:SUBLANES, :] = jnp.zeros((SUBLANES, 2 * D_FF), F32)

    def rows_of(r):
        if sample:
            seqs = slice(r // DEC_SEQ, (r + FFN_RB) // DEC_SEQ)
            return (h_ref[seqs].reshape(FFN_RB, D_MODEL),
                    [_expand_rows(mod_ref[seqs, k:k + 1, :]) for k in (3, 4, 5)])
        return h_ref[r:r + FFN_RB, :], [mod_ref[k:k + 1, :] for k in (3, 4, 5)]

    def modulate_pieces(s):
        def piece(r):
            hr, (sh2, sc2, _) = rows_of(r)
            u_ref[r:r + FFN_RB, :] = (hr * (1.0 + sc2) + sh2).astype(BF16)
        return [functools.partial(piece, r)
                for r in range(s * FFN_SUB, (s + 1) * FFN_SUB, FFN_RB)]

    n_sub = rows // FFN_SUB
    items = [(s, c) for s in range(n_sub) for c in range(N_CB)]

    def block_cols(c):
        return (slice(c * FFN_CB, (c + 1) * FFN_CB),
                slice(D_FF + c * FFN_CB, D_FF + (c + 1) * FFN_CB))

    def up_pieces(k):
        s, c = items[k]
        r0 = s * FFN_SUB

        def piece(cols):
            xp_ref[off + r0:off + r0 + FFN_SUB, cols] = _dot(u_ref[r0:r0 + FFN_SUB, :], wup_ref[:, cols])
        return [functools.partial(piece, cols) for cols in block_cols(c)]

    def down_pieces(k):
        s, c = items[k]
        r0 = s * FFN_SUB

        def piece(n):
            cols = slice(n * FFN_CB, (n + 1) * FFN_CB)
            d = _dot(act_ref[k % 2], wdn_ref[c * FFN_CB:(c + 1) * FFN_CB, cols])
            if c == 0:
                acc_ref[r0:r0 + FFN_SUB, cols] = d
            else:
                acc_ref[r0:r0 + FFN_SUB, cols] += d
        return [functools.partial(piece, n) for n in range(D_MODEL // FFN_CB)]

    def conv_rows(r, cols):
        if sample:
            seqs = slice(r // DEC_SEQ, (r + FFN_RB) // DEC_SEQ)
            x = xp_ref[r:r + FFN_RB, cols]
            st_ref[seqs, :, cols] = x.reshape(FFN_RB // DEC_SEQ, DEC_SEQ, FFN_CB)[:, DEC_SEQ - 2:, :]
            y = _sample_conv(x, past_ref[seqs, :, cols], cw_ref[:, cols])
        else:
            win = xp_ref[r:r + FFN_RB + SUBLANES, cols]
            y = (cw_ref[0:1, cols] * win[SUBLANES - 2:SUBLANES - 2 + FFN_RB]
                 + cw_ref[1:2, cols] * win[SUBLANES - 1:SUBLANES - 1 + FFN_RB]
                 + cw_ref[2:3, cols] * win[SUBLANES:SUBLANES + FFN_RB])
        return y + cb_ref[0:1, cols]

    def elementwise_pieces(k):
        s, c = items[k]
        ca, cg = block_cols(c)

        def piece(r):
            act_ref[k % 2, r - s * FFN_SUB:r - s * FFN_SUB + FFN_RB, :] = (
                _silu(conv_rows(r, cg)) * conv_rows(r, ca)).astype(BF16)
        return [functools.partial(piece, r)
                for r in range(s * FFN_SUB, (s + 1) * FFN_SUB, FFN_RB)]

    def finish_pieces(s):
        def piece(r):
            hr, (_, _, g2) = rows_of(r)
            hn = _layer_norm(ALPHA * hr + g2 * acc_ref[r:r + FFN_RB, :], lng_ref[...], lnb_ref[...])
            if sample:
                seqs = slice(r // DEC_SEQ, (r + FFN_RB) // DEC_SEQ)
                hout_ref[seqs] = hn.reshape(FFN_RB // DEC_SEQ, DEC_SEQ, D_MODEL)
            else:
                hout_ref[r:r + FFN_RB, :] = hn
        return [functools.partial(piece, r)
                for r in range(s * FFN_SUB, (s + 1) * FFN_SUB, FFN_RB)]

    ahead = FFN_AHEAD + FFN_AHEAD_EXTRA * (n_sub - 1)
    side = [[] for _ in items]
    for s in range(n_sub):
        if s + 1 < n_sub:
            early = modulate_pieces(s + 1)
            span = N_CB - ahead
            for i, pc in enumerate(early):
                side[s * N_CB + i * span // len(early)].append(pc)
            late = finish_pieces(s)
            for i, pc in enumerate(late):
                side[(s + 1) * N_CB + 1 + i * (N_CB - 1) // len(late)].append(pc)

    n_items = len(items)
    _interleave(modulate_pieces(0), [])
    for k in range(ahead):
        _interleave([], up_pieces(k))
    for k in range(n_items):
        mxu = up_pieces(k + ahead) if k + ahead < n_items else []
        if k >= 1:
            mxu = mxu + down_pieces(k - 1)
        _interleave(elementwise_pieces(k) + side[k], mxu)
    _interleave([], down_pieces(n_items - 1))
    _interleave(finish_pieces(n_sub - 1), [])

    if not sample:
        st_ref[...] = xp_ref[rows + SUBLANES - 2:rows + SUBLANES, :]
        xp_ref[0:SUBLANES, :] = xp_ref[rows:rows + SUBLANES, :]


def _ffn_scratch(rows, off):
    return [
        pltpu.VMEM((rows, D_MODEL), BF16),
        pltpu.VMEM((rows + off, 2 * D_FF), F32),
        pltpu.VMEM((2, FFN_SUB, FFN_CB), BF16),
        pltpu.VMEM((rows, D_MODEL), F32),
    ]


def _ffn_prompt_call(l, h, mod, p):
    ts = TS_FFN
    n_tiles = SEQ // ts
    ls = functools.partial(_layer_spec, l=l, n_grid=2)
    kern = functools.partial(_ffn_kernel, rows=ts, sample=False)
    return pl.pallas_call(
        kern,
        grid=(BATCH, n_tiles),
        in_specs=[
            pl.BlockSpec((ts, D_MODEL), lambda b, j: (b * n_tiles + j, 0)),
            pl.BlockSpec((None, None, 6, D_MODEL), lambda b, j: (l, b, 0, 0)),
            ls((D_MODEL, 2 * D_FF)),
            ls((3, 2 * D_FF)),
            ls((1, 2 * D_FF)),
            ls((D_FF, D_MODEL)),
            ls((1, D_MODEL)),
            ls((1, D_MODEL)),
        ],
        out_specs=[
            pl.BlockSpec((ts, D_MODEL), lambda b, j: (b * n_tiles + j, 0)),
            pl.BlockSpec((None, 2, 2 * D_FF), lambda b, j: (b, 0, 0)),
        ],
        out_shape=[
            jax.ShapeDtypeStruct((BATCH * SEQ, D_MODEL), F32),
            jax.ShapeDtypeStruct((BATCH, 2, 2 * D_FF), F32),
        ],
        scratch_shapes=_ffn_scratch(ts, SUBLANES),
        compiler_params=pltpu.CompilerParams(
            dimension_semantics=("arbitrary", "arbitrary"), vmem_limit_bytes=VMEM_LIMIT),
        name="ffn_prompt",
    )(h, mod, p["w_up"], p["ffn_conv_w"], p["ffn_conv_b"], p["w_down"], p["ln2_g"], p["ln2_b"])


def _ffn_sample_call(l, h3, mod, past, p):
    tb = TB_S
    rr = tb * DEC_SEQ
    ls = functools.partial(_layer_spec, l=l, n_grid=1)
    kern = functools.partial(_ffn_kernel, rows=rr, sample=True)
    return pl.pallas_call(
        kern,
        grid=(DEC_BATCH // tb,),
        in_specs=[
            pl.BlockSpec((tb, DEC_SEQ, D_MODEL), lambda i: (i, 0, 0)),
            pl.BlockSpec((None, tb, 6, D_MODEL), lambda i: (l, i, 0, 0)),
            pl.BlockSpec((None, tb, 2, 2 * D_FF), lambda i: (l, i, 0, 0)),
            ls((D_MODEL, 2 * D_FF)),
            ls((3, 2 * D_FF)),
            ls((1, 2 * D_FF)),
            ls((D_FF, D_MODEL)),
            ls((1, D_MODEL)),
            ls((1, D_MODEL)),
        ],
        out_specs=[
            pl.BlockSpec((tb, DEC_SEQ, D_MODEL), lambda i: (i, 0, 0)),
            pl.BlockSpec((tb, 2, 2 * D_FF), lambda i: (i, 0, 0)),
        ],
        out_shape=[
            jax.ShapeDtypeStruct((DEC_BATCH, DEC_SEQ, D_MODEL), F32),
            jax.ShapeDtypeStruct((DEC_BATCH, 2, 2 * D_FF), F32),
        ],
        scratch_shapes=_ffn_scratch(rr, 0),
        compiler_params=pltpu.CompilerParams(
            dimension_semantics=("arbitrary",), vmem_limit_bytes=VMEM_LIMIT),
        name="ffn_sample",
    )(h3, mod, past, p["w_up"], p["ffn_conv_w"], p["ffn_conv_b"], p["w_down"], p["ln2_g"], p["ln2_b"])


def _prep_params(w_in, conv_a_w, sgu_ln_g, sgu_ln_b, sgu_w, sgu_b, gla_w_a2, gla_b_a, gla_norm_g,
                 w_o, ln1_g, ln1_b, ffn_w_up, ffn_conv_w, ffn_conv_b, ffn_w_down, ln2_g, ln2_b):
    tril = jnp.tril(jnp.ones((CHUNK, CHUNK), bool))
    w_p = jnp.where(tril[None, None], sgu_w, 0.0)
    wcat_p = jnp.transpose(w_p, (0, 2, 1, 3)).reshape(DEPTH, CHUNK, H_B * CHUNK)
    tril8 = jnp.tril(jnp.ones((DEC_SEQ, DEC_SEQ), bool))
    w8 = jnp.where(tril8[None, None], sgu_w[:, :, :DEC_SEQ, :DEC_SEQ], 0.0)
    eye = jnp.eye(GROUP_SEQ, dtype=F32)
    w_blk = jnp.einsum("ij,lhts->lhitjs", eye, w8).reshape(DEPTH, H_B, CHUNK, CHUNK)
    wcat_s = jnp.transpose(w_blk, (0, 2, 1, 3)).reshape(DEPTH, CHUNK, H_B * CHUNK)
    sbias_p = jnp.repeat(jnp.transpose(sgu_b, (0, 2, 1)), DH_B, axis=2)
    sbias_s = jnp.tile(sbias_p[:, :DEC_SEQ, :], (1, GROUP_SEQ, 1))
    return {
        "w_in": jnp.pad(w_in, ((0, 0), (0, 0), (0, N_IN_PAD - N_IN))).astype(BF16),
        "conv_a_w": conv_a_w,
        "sgu_ln_g": sgu_ln_g.reshape(DEPTH, 1, D_B),
        "sgu_ln_b": sgu_ln_b.reshape(DEPTH, 1, D_B),
        "wcat_p": wcat_p, "wcat_s": wcat_s, "sbias_p": sbias_p, "sbias_s": sbias_s,
        "w_a2": jnp.pad(gla_w_a2, ((0, 0), (0, RANK_PAD - GATE_RANK), (0, 0))).astype(BF16),
        "b_a": gla_b_a.reshape(DEPTH, 1, H_C * DK_C),
        "gla_norm_g": gla_norm_g.reshape(DEPTH, 1, DV_C),
        "w_o": w_o.astype(BF16),
        "ln1_g": ln1_g.reshape(DEPTH, 1, D_MODEL), "ln1_b": ln1_b.reshape(DEPTH, 1, D_MODEL),
        "w_up": ffn_w_up.astype(BF16),
        "ffn_conv_w": ffn_conv_w,
        "ffn_conv_b": ffn_conv_b.reshape(DEPTH, 1, 2 * D_FF),
        "w_down": ffn_w_down.astype(BF16),
        "ln2_g": ln2_g.reshape(DEPTH, 1, D_MODEL), "ln2_b": ln2_b.reshape(DEPTH, 1, D_MODEL),
    }


def kernel(x_prompt, x_sample, c_prompt, c_sample, state_conv_a, state_gla, state_ffn_conv,
           ln_in_g, ln_in_b, w_ada, b_ada, w_in, conv_a_w, sgu_ln_g, sgu_ln_b, sgu_w, sgu_b,
           gla_w_a2, gla_b_a, gla_norm_g, w_o, ln1_g, ln1_b, ffn_w_up, ffn_conv_w, ffn_conv_b,
           ffn_w_down, ln2_g, ln2_b):
    p = _prep_params(w_in, conv_a_w, sgu_ln_g, sgu_ln_b, sgu_w, sgu_b, gla_w_a2, gla_b_a,
                     gla_norm_g, w_o, ln1_g, ln1_b, ffn_w_up, ffn_conv_w, ffn_conv_b,
                     ffn_w_down, ln2_g, ln2_b)
    mod = _ada_call(jnp.concatenate([c_prompt, c_sample], axis=0), w_ada, b_ada)
    mod = mod.reshape(DEPTH, BATCH + DEC_BATCH, 6, D_MODEL)
    mod_p, mod_s = mod[:, :BATCH], mod[:, BATCH:]
    s0_all = state_gla.reshape(DEPTH, DEC_BATCH, H_C * DK_C, DV_C)

    p["ln_in_g"] = ln_in_g.reshape(1, D_MODEL)
    p["ln_in_b"] = ln_in_b.reshape(1, D_MODEL)
    hp = x_prompt.reshape(BATCH * SEQ, D_MODEL)
    hs = x_sample

    outs_p = [[], [], [], []]
    outs_s = [[], [], [], []]
    for l in range(DEPTH):
        hp, ca, cv, gs = _mixer_prompt_call(l, hp, mod_p, p)
        hp, fs = _ffn_prompt_call(l, hp, mod_p, p)
        for acc, val in zip(outs_p, (ca, cv, gs, fs)):
            acc.append(val)
        hs, ca, cv, gs = _mixer_sample_call(l, hs, mod_s, state_conv_a, s0_all, p)
        hs, fs = _ffn_sample_call(l, hs, mod_s, state_ffn_conv, p)
        for acc, val in zip(outs_s, (ca, cv, gs, fs)):
            acc.append(val)

    ca_p, cv_p, gla_p, ffn_p = [jnp.stack(v) for v in outs_p]
    ca_s, cv_s, gla_s, ffn_s = [jnp.stack(v) for v in outs_s]
    return (hp.reshape(BATCH, SEQ, D_MODEL), hs,
            ca_p, ca_s, cv_p, cv_s,
            gla_p.reshape(DEPTH, BATCH, H_C, DK_C, DV_C),
            gla_s.reshape(DEPTH, DEC_BATCH, H_C, DK_C, DV_C),
            ffn_p, ffn_s)
```

```python
import functools
import math

import jax
import jax.numpy as jnp
from jax import lax
from jax.experimental import pallas as pl
from jax.experimental.pallas import tpu as pltpu

D_MODEL = 1024
BATCH = 8
SEQ = 2048
DEPTH = 4
DEC_BATCH = 128
DEC_SEQ = 8
D_A = 256
D_B = 256
H_B = 4
DH_B = 64
D_C = 512
H_C = 4
DV_C = 128
DK_C = 64
GATE_RANK = 16
GATE_TAU = 16.0
CHUNK = 128
D_FF = 2816
LN_EPS = 1e-5
ALPHA = (2.0 * DEPTH) ** 0.25
N_IN = 2832
N_IN_PAD = 2944
RANK_PAD = 128

O_AIN, O_AB, O_AC, O_SU, O_SV, O_Q, O_K, O_V, O_G, O_R = (
    0, 256, 512, 768, 1024, 1280, 1536, 1792, 2304, 2816)

SUBLANES = 8
VMEM_LIMIT = 56 * 1024 * 1024

TS_MIX = 1024
MIX_RB = 256
MIX_CB = 256
NORM_RB = 32
TS_FFN = 256
TB_S = 32
GROUP_SEQ = CHUNK // DEC_SEQ
SEQ_UNROLL = 4

F32 = jnp.float32
BF16 = jnp.bfloat16


def _dot(a, b):
    return jnp.dot(a, b, preferred_element_type=F32)


def _dot_nt(a, b):
    return lax.dot_general(a, b, (((1,), (1,)), ((), ())), preferred_element_type=F32)


def _dot_tn(a, b):
    return lax.dot_general(a, b, (((0,), (0,)), ((), ())), preferred_element_type=F32)


def _layer_norm(x, g, b):
    mu = jnp.mean(x, axis=-1, keepdims=True)
    xc = x - mu
    var = jnp.mean(xc * xc, axis=-1, keepdims=True)
    return xc * lax.rsqrt(var + LN_EPS) * g + b


def _gelu(x):
    c = math.sqrt(2.0 / math.pi)
    return 0.5 * x * (1.0 + jnp.tanh(c * (x + 0.044715 * (x * x * x))))


def _silu(x):
    return x * jax.nn.sigmoid(x)


def _log_sigmoid(x):
    return jnp.minimum(x, 0.0) - jnp.log1p(jnp.exp(-jnp.abs(x)))


def _split3(x):
    hi = x.astype(BF16)
    r1 = x - hi.astype(F32)
    mid = r1.astype(BF16)
    lo = (r1 - mid.astype(F32)).astype(BF16)
    return hi, mid, lo


def _dot_exact_lhs(m_bf16, x):
    hi, mid, lo = _split3(x)
    return _dot(m_bf16, hi) + _dot(m_bf16, mid) + _dot(m_bf16, lo)


def _head_masks(width, n_heads):
    lane = lax.broadcasted_iota(jnp.int32, (1, width), 1)
    per = width // n_heads
    return [(lane >= h * per) & (lane < (h + 1) * per) for h in range(n_heads)]


def _interleave(valu_pieces, mxu_pieces):
    n_v, n_m = len(valu_pieces), len(mxu_pieces)
    done = 0
    for k, mp in enumerate(mxu_pieces):
        upto = (k + 1) * n_v // (n_m + 1)
        for vp in valu_pieces[done:upto]:
            vp()
        done = upto
        mp()
    for vp in valu_pieces[done:]:
        vp()


def _ada_kernel(c_ref, w_ref, b_ref, o_ref):
    cs = _silu(c_ref[...]).astype(BF16)
    o_ref[...] = _dot(cs, w_ref[...].astype(BF16)) + b_ref[...]


def _ada_call(c_all, w_ada, b_ada):
    n_rows = c_all.shape[0]
    tn = 1536
    return pl.pallas_call(
        _ada_kernel,
        grid=(DEPTH, 6 * D_MODEL // tn),
        in_specs=[
            pl.BlockSpec((n_rows, D_MODEL), lambda l, j: (0, 0)),
            pl.BlockSpec((None, D_MODEL, tn), lambda l, j: (l, 0, j)),
            pl.BlockSpec((None, 1, tn), lambda l, j: (l, 0, j)),
        ],
        out_specs=pl.BlockSpec((None, n_rows, tn), lambda l, j: (l, 0, j)),
        out_shape=jax.ShapeDtypeStruct((DEPTH, n_rows, 6 * D_MODEL), F32),
        compiler_params=pltpu.CompilerParams(
            dimension_semantics=("arbitrary", "arbitrary"), vmem_limit_bytes=VMEM_LIMIT),
        name="ada_mod",
    )(c_all, w_ada, b_ada.reshape(DEPTH, 1, 6 * D_MODEL))


def _sgu_chunk(z_ref, zrows, y_ref, rows, wcat, sbias, ln_g, ln_b, masks_b):
    su = _gelu(z_ref[zrows, O_SU:O_SU + D_B])
    sv = _layer_norm(_gelu(z_ref[zrows, O_SV:O_SV + D_B]), ln_g, ln_b)
    svb = sv.astype(BF16)
    zero = jnp.zeros_like(svb)
    svm = jnp.concatenate([jnp.where(m, svb, zero) for m in masks_b], axis=0)
    mixed = _dot(wcat, svm) + sbias
    y_ref[rows, D_A:D_A + D_B] = (su * mixed).astype(BF16)
    return sv


def _gla_gate(z_ref, rows, wa2, ba):
    cr = z_ref[rows, O_R:O_R + RANK_PAD].astype(BF16)
    return _log_sigmoid(_dot(cr, wa2) + ba) * (1.0 / GATE_TAU)


def _gla_head_out(z_ref, zrows, y_ref, rows, hh, o, gn):
    ms = jnp.mean(o * o, axis=-1, keepdims=True)
    on = o * lax.rsqrt(ms + LN_EPS) * gn
    g = z_ref[zrows, O_G + hh * DV_C:O_G + (hh + 1) * DV_C]
    y_ref[rows, D_A + D_B + hh * DV_C:D_A + D_B + (hh + 1) * DV_C] = (_silu(g) * on).astype(BF16)


def _const_spec(shape, n_grid):
    zeros = (0,) * len(shape)
    if n_grid == 1:
        return pl.BlockSpec(shape, lambda i: zeros, pipeline_mode=pl.Buffered(1))
    return pl.BlockSpec(shape, lambda b, j: zeros, pipeline_mode=pl.Buffered(1))


def _layer_spec(shape, l, n_grid):
    idx = (l,) + (0,) * len(shape)
    if n_grid == 1:
        return pl.BlockSpec((None,) + shape, lambda i: idx, pipeline_mode=pl.Buffered(1))
    return pl.BlockSpec((None,) + shape, lambda b, j: idx, pipeline_mode=pl.Buffered(1))


def _mixer_prompt_kernel(*refs, ts, n_tiles, first):
    if first:
        (x_ref, mod_ref, wupf_ref, wdnf_ref, lnig_ref, lnib_ref, win_ref, cw_ref, sg_ref, sb_ref,
         wcat_ref, sbias_ref, wa2_ref, ba_ref, gn_ref, wo_ref, lng_ref, lnb_ref,
         hout_ref, ca_ref, cv_ref, gla_ref, wupb_ref, wdnb_ref,
         u_ref, z_ref, xa_ref, y_ref, st_ref, yo_ref, h_ref) = refs
        for r in range(0, ts, NORM_RB):
            h_ref[r:r + NORM_RB, :] = _layer_norm(x_ref[r:r + NORM_RB, :], lnig_ref[...], lnib_ref[...])
    else:
        (h_ref, mod_ref, wupf_ref, wdnf_ref, win_ref, cw_ref, sg_ref, sb_ref,
         wcat_ref, sbias_ref, wa2_ref, ba_ref, gn_ref, wo_ref, lng_ref, lnb_ref,
         hout_ref, ca_ref, cv_ref, gla_ref, wupb_ref, wdnb_ref,
         u_ref, z_ref, xa_ref, y_ref, st_ref, yo_ref) = refs
    wupb_ref[...] = wupf_ref[...].astype(BF16)
    wdnb_ref[...] = wdnf_ref[...].astype(BF16)
    j = pl.program_id(1)

    @pl.when(j == 0)
    def _():
        xa_ref[0:SUBLANES, :] = jnp.zeros((SUBLANES, D_A), F32)
        st_ref[...] = jnp.zeros_like(st_ref)

    sh1, sc1, g1 = mod_ref[0:1, :], mod_ref[1:2, :], mod_ref[2:3, :]
    for r in range(0, ts, NORM_RB):
        u_ref[r:r + NORM_RB, :] = (h_ref[r:r + NORM_RB, :] * (1.0 + sc1) + sh1).astype(BF16)

    def z_rows(r0, n):
        p, local = divmod(r0, MIX_RB)
        base = (p % 2) * MIX_RB + local
        return slice(base, base + n)

    def in_proj_pieces(p):
        rows = slice(p * MIX_RB, (p + 1) * MIX_RB)
        zrows = z_rows(p * MIX_RB, MIX_RB)

        def piece(c0):
            cols = slice(c0, min(c0 + MIX_CB, N_IN_PAD))
            z_ref[zrows, cols] = _dot(u_ref[rows, :], win_ref[:, cols])
        return [functools.partial(piece, c0) for c0 in range(0, N_IN_PAD, MIX_CB)]

    def out_proj_pieces(p):
        rows = slice(p * MIX_RB, (p + 1) * MIX_RB)

        def piece(c0):
            cols = slice(c0, c0 + MIX_CB)
            yo_ref[rows, cols] = _dot(y_ref[rows, :], wo_ref[:, cols])
        return [functools.partial(piece, c0) for c0 in range(0, D_MODEL, MIX_CB)]

    def norm_pieces(p):
        def piece(r):
            rows = slice(r, r + NORM_RB)
            hout_ref[rows, :] = _layer_norm(ALPHA * h_ref[rows, :] + g1 * yo_ref[rows, :],
                                            lng_ref[...], lnb_ref[...])
        return [functools.partial(piece, r)
                for r in range(p * MIX_RB, (p + 1) * MIX_RB, NORM_RB)]

    cw = cw_ref[...]
    wcat = wcat_ref[...].astype(BF16)
    sbias = sbias_ref[...]
    wa2 = wa2_ref[...]
    ba = ba_ref[...]
    gn = gn_ref[...]
    sg, sb = sg_ref[...], sb_ref[...]
    masks_b = _head_masks(D_B, H_B)
    masks_k = _head_masks(H_C * DK_C, H_C)
    row = lax.broadcasted_iota(jnp.int32, (CHUNK, CHUNK), 0)
    col = lax.broadcasted_iota(jnp.int32, (CHUNK, CHUNK), 1)
    causal = col <= row
    tri = causal.astype(BF16)

    def mix_pieces(c):
        r0 = c * CHUNK
        rows = slice(r0, r0 + CHUNK)
        zrows = z_rows(r0, CHUNK)
        v = {}

        def conv_a():
            x = z_ref[zrows, O_AC:O_AC + D_A] * z_ref[zrows, O_AIN:O_AIN + D_A]
            xa_ref[r0 + SUBLANES:r0 + SUBLANES + CHUNK, :] = x
            win = xa_ref[r0:r0 + CHUNK + SUBLANES, :]
            conv = (cw[0:1, :] * win[SUBLANES - 2:SUBLANES - 2 + CHUNK]
                    + cw[1:2, :] * win[SUBLANES - 1:SUBLANES - 1 + CHUNK]
                    + cw[2:3, :] * x)
            y_ref[rows, 0:D_A] = (z_ref[zrows, O_AB:O_AB + D_A] * conv).astype(BF16)

        def sgu():
            cv_ref[...] = _sgu_chunk(z_ref, zrows, y_ref, rows, wcat, sbias, sg, sb, masks_b)

        def gla_prep():
            lg = _gla_gate(z_ref, zrows, wa2, ba)
            b = _dot_exact_lhs(tri, lg)
            b_last = b[CHUNK - 1:CHUNK, :]
            b_mid = b[CHUNK // 2 - 1:CHUNK // 2, :]
            q = z_ref[zrows, O_Q:O_Q + H_C * DK_C]
            k = z_ref[zrows, O_K:O_K + H_C * DK_C]
            v["qt"] = (q * (DK_C ** -0.5)) * jnp.exp(b - b_mid)
            v["kt"] = (k * jnp.exp(b_mid - b)).astype(BF16)
            v["kh"] = k * jnp.exp(b_last - b)
            st = st_ref[...]
            v["stb"] = (st * jnp.exp(b_mid)).astype(BF16)
            v["st_new"] = st * jnp.exp(b_last)

        def gla_head(hh):
            zq = jnp.zeros_like(v["qt"])
            qm = jnp.where(masks_k[hh], v["qt"], zq).astype(BF16)
            a = jnp.where(causal, _dot_nt(qm, v["kt"]), 0.0).astype(BF16)
            vb = z_ref[zrows, O_V + hh * DV_C:O_V + (hh + 1) * DV_C].astype(BF16)
            o = _dot(a, vb) + _dot_nt(qm, v["stb"])
            _gla_head_out(z_ref, zrows, y_ref, rows, hh, o, gn)
            khm = jnp.where(masks_k[hh], v["kh"], zq).astype(BF16)
            v["st_new"] = v["st_new"] + _dot_tn(vb, khm)

        def gla_state():
            st_ref[...] = v["st_new"]

        return ([conv_a, sgu, gla_prep] + [functools.partial(gla_head, hh) for hh in range(H_C)]
                + [gla_state])

    n_rb = ts // MIX_RB
    per = MIX_RB // CHUNK
    _interleave([], in_proj_pieces(0))
    for p in range(n_rb):
        valu = [pc for c in range(p * per, (p + 1) * per) for pc in mix_pieces(c)]
        if p >= 2:
            valu = valu + norm_pieces(p - 2)
        mxu = in_proj_pieces(p + 1) if p + 1 < n_rb else []
        if p >= 1:
            mxu = mxu + out_proj_pieces(p - 1)
        _interleave(valu, mxu)
    _interleave(norm_pieces(n_rb - 2) if n_rb >= 2 else [], out_proj_pieces(n_rb - 1))
    _interleave(norm_pieces(n_rb - 1), [])

    ca_ref[...] = xa_ref[ts + SUBLANES - 2:ts + SUBLANES, :]
    xa_ref[0:SUBLANES, :] = xa_ref[ts:ts + SUBLANES, :]

    @pl.when(j == n_tiles - 1)
    def _():
        gla_ref[...] = st_ref[...].T


def _mixer_prompt_call(l, h, mod, p, w_in_b, w_o_b):
    ts = TS_MIX
    n_tiles = SEQ // ts
    n_steps = BATCH * n_tiles
    first = l == 0
    ls = functools.partial(_layer_spec, l=l, n_grid=2)
    cs = functools.partial(_const_spec, n_grid=2)
    kern = functools.partial(_mixer_prompt_kernel, ts=ts, n_tiles=n_tiles, first=first)
    ln_in_specs = [pl.BlockSpec((1, D_MODEL), lambda b, j: (0, 0))] * 2 if first else []
    ln_in_args = (p["ln_in_g"], p["ln_in_b"]) if first else ()
    ln_in_scratch = [pltpu.VMEM((ts, D_MODEL), F32)] if first else []
    up_rows = D_MODEL // n_steps
    dn_rows = D_FF // n_steps
    return pl.pallas_call(
        kern,
        grid=(BATCH, n_tiles),
        in_specs=[
            pl.BlockSpec((ts, D_MODEL), lambda b, j: (b * n_tiles + j, 0)),
            pl.BlockSpec((None, None, 6, D_MODEL), lambda b, j: (l, b, 0, 0)),
            pl.BlockSpec((None, up_rows, 2 * D_FF), lambda b, j: (l, b * n_tiles + j, 0)),
            pl.BlockSpec((None, dn_rows, D_MODEL), lambda b, j: (l, b * n_tiles + j, 0)),
            *ln_in_specs,
            cs((D_MODEL, N_IN_PAD)),
            ls((3, D_A)),
            ls((1, D_B)),
            ls((1, D_B)),
            ls((CHUNK, H_B * CHUNK)),
            ls((CHUNK, D_B)),
            ls((RANK_PAD, H_C * DK_C)),
            ls((1, H_C * DK_C)),
            ls((1, DV_C)),
            cs((D_MODEL, D_MODEL)),
            ls((1, D_MODEL)),
            ls((1, D_MODEL)),
        ],
        out_specs=[
            pl.BlockSpec((ts, D_MODEL), lambda b, j: (b * n_tiles + j, 0)),
            pl.BlockSpec((None, 2, D_A), lambda b, j: (b, 0, 0)),
            pl.BlockSpec((None, CHUNK, D_B), lambda b, j: (b, 0, 0)),
            pl.BlockSpec((None, H_C * DK_C, DV_C), lambda b, j: (b, 0, 0)),
            pl.BlockSpec((up_rows, 2 * D_FF), lambda b, j: (b * n_tiles + j, 0)),
            pl.BlockSpec((dn_rows, D_MODEL), lambda b, j: (b * n_tiles + j, 0)),
        ],
        out_shape=[
            jax.ShapeDtypeStruct((BATCH * SEQ, D_MODEL), F32),
            jax.ShapeDtypeStruct((BATCH, 2, D_A), F32),
            jax.ShapeDtypeStruct((BATCH, CHUNK, D_B), F32),
            jax.ShapeDtypeStruct((BATCH, H_C * DK_C, DV_C), F32),
            jax.ShapeDtypeStruct((D_MODEL, 2 * D_FF), BF16),
            jax.ShapeDtypeStruct((D_FF, D_MODEL), BF16),
        ],
        scratch_shapes=[
            pltpu.VMEM((ts, D_MODEL), BF16),
            pltpu.VMEM((2 * MIX_RB, N_IN_PAD), F32),
            pltpu.VMEM((ts + SUBLANES, D_A), F32),
            pltpu.VMEM((ts, D_MODEL), BF16),
            pltpu.VMEM((DV_C, H_C * DK_C), F32),
            pltpu.VMEM((ts, D_MODEL), F32),
            *ln_in_scratch,
        ],
        compiler_params=pltpu.CompilerParams(
            dimension_semantics=("arbitrary", "arbitrary"), vmem_limit_bytes=VMEM_LIMIT),
        name="mixer_prompt",
    )(h, mod, p["ffn_w_up"], p["ffn_w_down"], *ln_in_args, w_in_b, p["conv_a_w"], p["sgu_ln_g"],
      p["sgu_ln_b"], p["wcat_p"], p["sbias_p"], p["w_a2"], p["b_a"], p["gla_norm_g"], w_o_b,
      p["ln1_g"], p["ln1_b"])


def _expand_rows(x3):
    n, _, c = x3.shape
    return jnp.broadcast_to(x3, (n, DEC_SEQ, c)).reshape(n * DEC_SEQ, c)


def _sample_conv(x, past, cw):
    p0 = _expand_rows(past[:, 0:1, :])
    p1 = _expand_rows(past[:, 1:2, :])
    t_idx = lax.broadcasted_iota(jnp.int32, x.shape, 0) & (DEC_SEQ - 1)
    xm1 = jnp.where(t_idx >= 1, pltpu.roll(x, 1, 0), p1)
    xm2 = jnp.where(t_idx >= 2, pltpu.roll(x, 2, 0), jnp.where(t_idx == 1, p1, p0))
    return cw[0:1, :] * xm2 + cw[1:2, :] * xm1 + cw[2:3, :] * x


def _mixer_sample_kernel(*refs, tb, first):
    if first:
        (h_ref, mod_ref, lnig_ref, lnib_ref, past_ref, s0_ref, win_ref, cw_ref, sg_ref, sb_ref,
         wcat_ref, sbias_ref, wa2_ref, ba_ref, gn_ref, wo_ref, lng_ref, lnb_ref,
         hout_ref, ca_ref, cv_ref, gla_ref,
         z_ref, y_ref, qt_ref, kh_ref, eb_ref, oi_ref, os_ref) = refs
    else:
        (h_ref, mod_ref, past_ref, s0_ref, win_ref, cw_ref, sg_ref, sb_ref,
         wcat_ref, sbias_ref, wa2_ref, ba_ref, gn_ref, wo_ref, lng_ref, lnb_ref,
         hout_ref, ca_ref, cv_ref, gla_ref,
         z_ref, y_ref, qt_ref, kh_ref, eb_ref, oi_ref, os_ref) = refs
    rr = tb * DEC_SEQ
    h = h_ref[...].reshape(rr, D_MODEL)
    if first:
        h = _layer_norm(h, lnig_ref[...], lnib_ref[...])
    sh1 = _expand_rows(mod_ref[:, 0:1, :])
    sc1 = _expand_rows(mod_ref[:, 1:2, :])
    g1 = _expand_rows(mod_ref[:, 2:3, :])
    u = (h * (1.0 + sc1) + sh1).astype(BF16)
    z_ref[...] = _dot(u, win_ref[...])

    x = z_ref[:, O_AC:O_AC + D_A] * z_ref[:, O_AIN:O_AIN + D_A]
    conv = _sample_conv(x, past_ref[...], cw_ref[...])
    y_ref[:, 0:D_A] = (z_ref[:, O_AB:O_AB + D_A] * conv).astype(BF16)
    ca_ref[...] = x.reshape(tb, DEC_SEQ, D_A)[:, DEC_SEQ - 2:DEC_SEQ, :]

    wcat = wcat_ref[...].astype(BF16)
    sbias = sbias_ref[...]
    wa2 = wa2_ref[...]
    ba = ba_ref[...]
    gn = gn_ref[...]
    sg, sb = sg_ref[...], sb_ref[...]
    masks_b = _head_masks(D_B, H_B)
    masks_k = _head_masks(H_C * DK_C, H_C)
    row = lax.broadcasted_iota(jnp.int32, (CHUNK, CHUNK), 0)
    col = lax.broadcasted_iota(jnp.int32, (CHUNK, CHUNK), 1)
    same_seq = (row // DEC_SEQ) == (col // DEC_SEQ)
    causal = same_seq & (col <= row)
    tri = causal.astype(BF16)
    blk = same_seq.astype(BF16)

    for gi in range(rr // CHUNK):
        rows = pl.ds(gi * CHUNK, CHUNK)
        sv = _sgu_chunk(z_ref, rows, y_ref, rows, wcat, sbias, sg, sb, masks_b)
        cv_ref[gi * GROUP_SEQ:(gi + 1) * GROUP_SEQ] = sv.reshape(GROUP_SEQ, DEC_SEQ, D_B)
        lg = _gla_gate(z_ref, rows, wa2, ba)
        b = _dot_exact_lhs(tri, lg)
        b_tot = _dot_exact_lhs(blk, lg)
        q = z_ref[rows, O_Q:O_Q + H_C * DK_C]
        k = z_ref[rows, O_K:O_K + H_C * DK_C]
        qt = (q * (DK_C ** -0.5)) * jnp.exp(b)
        kt = (k * jnp.exp(-b)).astype(BF16)
        qt_ref[rows, :] = qt
        kh_ref[rows, :] = k * jnp.exp(b_tot - b)
        eb_ref[rows, :] = jnp.exp(b_tot)
        zq = jnp.zeros_like(qt)
        for hh in range(H_C):
            qm = jnp.where(masks_k[hh], qt, zq).astype(BF16)
            a = jnp.where(causal, _dot_nt(qm, kt), 0.0).astype(BF16)
            vb = z_ref[rows, O_V + hh * DV_C:O_V + (hh + 1) * DV_C].astype(BF16)
            oi_ref[rows, hh * DV_C:(hh + 1) * DV_C] = _dot(a, vb)

    ones16 = jnp.ones((2 * SUBLANES, DV_C), BF16)
    sub = lax.broadcasted_iota(jnp.int32, (SUBLANES, H_C * DK_C), 0)

    def seq_body(i, carry):
        r0 = pl.multiple_of(i * DEC_SEQ, DEC_SEQ)
        rows = pl.ds(r0, DEC_SEQ)
        s0 = s0_ref[i]
        qt = qt_ref[rows, :]
        kh = kh_ref[rows, :]
        zq = jnp.zeros_like(qt)
        q_exp = jnp.concatenate([jnp.where(m, qt, zq) for m in masks_k], axis=0).astype(BF16)
        k_exp = jnp.concatenate([jnp.where(m, kh, zq) for m in masks_k], axis=0).astype(BF16)
        o_int = _dot(q_exp, s0.astype(BF16))
        v = z_ref[rows, O_V:O_V + D_C]
        v_exp = jnp.concatenate([v[:, hh * DV_C:(hh + 1) * DV_C] for hh in range(H_C)],
                                axis=0).astype(BF16)
        eb = eb_ref[rows, :]
        hi, mid, lo = _split3(eb)
        e3 = jnp.where(sub == 0, hi.astype(F32),
                       jnp.where(sub == 1, mid.astype(F32),
                                 jnp.where(sub == 2, lo.astype(F32), jnp.zeros_like(eb))))
        e3 = jnp.concatenate([e3, jnp.zeros_like(eb)], axis=0).astype(BF16)
        decay = _dot_tn(e3, ones16)
        gla_ref[i] = decay * s0 + _dot_tn(k_exp, v_exp)
        for hh in range(H_C):
            os_ref[rows, hh * DV_C:(hh + 1) * DV_C] = o_int[hh * DEC_SEQ:(hh + 1) * DEC_SEQ, :]
        return carry

    lax.fori_loop(0, tb, seq_body, 0, unroll=SEQ_UNROLL)

    for gi in range(rr // CHUNK):
        rows = pl.ds(gi * CHUNK, CHUNK)
        for hh in range(H_C):
            cols = slice(hh * DV_C, (hh + 1) * DV_C)
            _gla_head_out(z_ref, rows, y_ref, rows, hh, oi_ref[rows, cols] + os_ref[rows, cols], gn)

    y = _dot(y_ref[...], wo_ref[...])
    hn = _layer_norm(ALPHA * h + g1 * y, lng_ref[...], lnb_ref[...])
    hout_ref[...] = hn.reshape(tb, DEC_SEQ, D_MODEL)


def _mixer_sample_call(l, h3, mod, past, s0, p, w_in_b, w_o_b):
    tb = TB_S
    ls = functools.partial(_layer_spec, l=l, n_grid=1)
    cs = functools.partial(_const_spec, n_grid=1)
    first = l == 0
    kern = functools.partial(_mixer_sample_kernel, tb=tb, first=first)
    ln_in_specs = [pl.BlockSpec((1, D_MODEL), lambda i: (0, 0))] * 2 if first else []
    ln_in_args = (p["ln_in_g"], p["ln_in_b"]) if first else ()
    rr = tb * DEC_SEQ
    return pl.pallas_call(
        kern,
        grid=(DEC_BATCH // tb,),
        in_specs=[
            pl.BlockSpec((tb, DEC_SEQ, D_MODEL), lambda i: (i, 0, 0)),
            pl.BlockSpec((None, tb, 6, D_MODEL), lambda i: (l, i, 0, 0)),
            *ln_in_specs,
            pl.BlockSpec((None, tb, 2, D_A), lambda i: (l, i, 0, 0)),
            pl.BlockSpec((None, tb, H_C * DK_C, DV_C), lambda i: (l, i, 0, 0)),
            cs((D_MODEL, N_IN_PAD)),
            ls((3, D_A)),
            ls((1, D_B)),
            ls((1, D_B)),
            ls((CHUNK, H_B * CHUNK)),
            ls((CHUNK, D_B)),
            ls((RANK_PAD, H_C * DK_C)),
            ls((1, H_C * DK_C)),
            ls((1, DV_C)),
            cs((D_MODEL, D_MODEL)),
            ls((1, D_MODEL)),
            ls((1, D_MODEL)),
        ],
        out_specs=[
            pl.BlockSpec((tb, DEC_SEQ, D_MODEL), lambda i: (i, 0, 0)),
            pl.BlockSpec((tb, 2, D_A), lambda i: (i, 0, 0)),
            pl.BlockSpec((tb, DEC_SEQ, D_B), lambda i: (i, 0, 0)),
            pl.BlockSpec((tb, H_C * DK_C, DV_C), lambda i: (i, 0, 0)),
        ],
        out_shape=[
            jax.ShapeDtypeStruct((DEC_BATCH, DEC_SEQ, D_MODEL), F32),
            jax.ShapeDtypeStruct((DEC_BATCH, 2, D_A), F32),
            jax.ShapeDtypeStruct((DEC_BATCH, DEC_SEQ, D_B), F32),
            jax.ShapeDtypeStruct((DEC_BATCH, H_C * DK_C, DV_C), F32),
        ],
        scratch_shapes=[
            pltpu.VMEM((rr, N_IN_PAD), F32),
            pltpu.VMEM((rr, D_MODEL), BF16),
            pltpu.VMEM((rr, H_C * DK_C), F32),
            pltpu.VMEM((rr, H_C * DK_C), F32),
            pltpu.VMEM((rr, H_C * DK_C), F32),
            pltpu.VMEM((rr, D_C), F32),
            pltpu.VMEM((rr, D_C), F32),
        ],
        compiler_params=pltpu.CompilerParams(
            dimension_semantics=("arbitrary",), vmem_limit_bytes=VMEM_LIMIT),
        name="mixer_sample",
    )(h3, mod, *ln_in_args, past, s0, w_in_b, p["conv_a_w"], p["sgu_ln_g"], p["sgu_ln_b"], p["wcat_s"],
      p["sbias_s"], p["w_a2"], p["b_a"], p["gla_norm_g"], w_o_b, p["ln1_g"], p["ln1_b"])


FFN_RB = 32
FFN_CB = 256
N_CB = D_FF // FFN_CB
FFN_AHEAD = 3
FFN_AHEAD_EXTRA = 4
FFN_SUB = 256


def _ffn_kernel(*refs, rows, sample, convert_next):
    if sample:
        (h_ref, mod_ref, past_ref, wup_ref, cw_ref, cb_ref, wdn_ref, lng_ref, lnb_ref,
         hout_ref, st_ref, u_ref, xp_ref, act_ref, acc_ref) = refs
        off = 0
    else:
        if convert_next:
            (h_ref, mod_ref, winf_ref, wof_ref, wup_ref, cw_ref, cb_ref, wdn_ref, lng_ref, lnb_ref,
             hout_ref, st_ref, winb_ref, wob_ref, u_ref, xp_ref, act_ref, acc_ref) = refs
            winb_ref[...] = jnp.zeros_like(winb_ref)
            winb_ref[:, 0:N_IN] = winf_ref[...].astype(BF16)
            wob_ref[...] = wof_ref[...].astype(BF16)
        else:
            (h_ref, mod_ref, wup_ref, cw_ref, cb_ref, wdn_ref, lng_ref, lnb_ref,
             hout_ref, st_ref, u_ref, xp_ref, act_ref, acc_ref) = refs
        off = SUBLANES

        @pl.when(pl.program_id(1) == 0)
        def _():
            xp_ref[0:SUBLANES, :] = jnp.zeros((SUBLANES, 2 * D_FF), F32)

    def rows_of(r):
        if sample:
            seqs = slice(r // DEC_SEQ, (r + FFN_RB) // DEC_SEQ)
            return (h_ref[seqs].reshape(FFN_RB, D_MODEL),
                    [_expand_rows(mod_ref[seqs, k:k + 1, :]) for k in (3, 4, 5)])
        return h_ref[r:r + FFN_RB, :], [mod_ref[k:k + 1, :] for k in (3, 4, 5)]

    def modulate_pieces(s):
        def piece(r):
            hr, (sh2, sc2, _) = rows_of(r)
            u_ref[r:r + FFN_RB, :] = (hr * (1.0 + sc2) + sh2).astype(BF16)
        return [functools.partial(piece, r)
                for r in range(s * FFN_SUB, (s + 1) * FFN_SUB, FFN_RB)]

    n_sub = rows // FFN_SUB
    items = [(s, c) for s in range(n_sub) for c in range(N_CB)]

    def block_cols(c):
        return (slice(c * FFN_CB, (c + 1) * FFN_CB),
                slice(D_FF + c * FFN_CB, D_FF + (c + 1) * FFN_CB))

    def up_pieces(k):
        s, c = items[k]
        r0 = s * FFN_SUB

        def piece(cols):
            xp_ref[off + r0:off + r0 + FFN_SUB, cols] = _dot(u_ref[r0:r0 + FFN_SUB, :], wup_ref[:, cols])
        return [functools.partial(piece, cols) for cols in block_cols(c)]

    def down_pieces(k):
        s, c = items[k]
        r0 = s * FFN_SUB

        def piece(n):
            cols = slice(n * FFN_CB, (n + 1) * FFN_CB)
            d = _dot(act_ref[k % 2], wdn_ref[c * FFN_CB:(c + 1) * FFN_CB, cols])
            if c == 0:
                acc_ref[r0:r0 + FFN_SUB, cols] = d
            else:
                acc_ref[r0:r0 + FFN_SUB, cols] += d
        return [functools.partial(piece, n) for n in range(D_MODEL // FFN_CB)]

    def conv_rows(r, cols):
        if sample:
            seqs = slice(r // DEC_SEQ, (r + FFN_RB) // DEC_SEQ)
            x = xp_ref[r:r + FFN_RB, cols]
            st_ref[seqs, :, cols] = x.reshape(FFN_RB // DEC_SEQ, DEC_SEQ, FFN_CB)[:, DEC_SEQ - 2:, :]
            y = _sample_conv(x, past_ref[seqs, :, cols], cw_ref[:, cols])
        else:
            win = xp_ref[r:r + FFN_RB + SUBLANES, cols]
            y = (cw_ref[0:1, cols] * win[SUBLANES - 2:SUBLANES - 2 + FFN_RB]
                 + cw_ref[1:2, cols] * win[SUBLANES - 1:SUBLANES - 1 + FFN_RB]
                 + cw_ref[2:3, cols] * win[SUBLANES:SUBLANES + FFN_RB])
        return y + cb_ref[0:1, cols]

    def elementwise_pieces(k):
        s, c = items[k]
        ca, cg = block_cols(c)

        def piece(r):
            act_ref[k % 2, r - s * FFN_SUB:r - s * FFN_SUB + FFN_RB, :] = (
                _silu(conv_rows(r, cg)) * conv_rows(r, ca)).astype(BF16)
        return [functools.partial(piece, r)
                for r in range(s * FFN_SUB, (s + 1) * FFN_SUB, FFN_RB)]

    def finish_pieces(s):
        def piece(r):
            hr, (_, _, g2) = rows_of(r)
            hn = _layer_norm(ALPHA * hr + g2 * acc_ref[r:r + FFN_RB, :], lng_ref[...], lnb_ref[...])
            if sample:
                seqs = slice(r // DEC_SEQ, (r + FFN_RB) // DEC_SEQ)
                hout_ref[seqs] = hn.reshape(FFN_RB // DEC_SEQ, DEC_SEQ, D_MODEL)
            else:
                hout_ref[r:r + FFN_RB, :] = hn
        return [functools.partial(piece, r)
                for r in range(s * FFN_SUB, (s + 1) * FFN_SUB, FFN_RB)]

    ahead = FFN_AHEAD + FFN_AHEAD_EXTRA * (n_sub - 1)
    side = [[] for _ in items]
    for s in range(n_sub):
        if s + 1 < n_sub:
            early = modulate_pieces(s + 1)
            span = N_CB - ahead
            for i, pc in enumerate(early):
                side[s * N_CB + i * span // len(early)].append(pc)
            late = finish_pieces(s)
            for i, pc in enumerate(late):
                side[(s + 1) * N_CB + 1 + i * (N_CB - 1) // len(late)].append(pc)

    n_items = len(items)
    _interleave(modulate_pieces(0), [])
    for k in range(ahead):
        _interleave([], up_pieces(k))
    for k in range(n_items):
        mxu = up_pieces(k + ahead) if k + ahead < n_items else []
        if k >= 1:
            mxu = mxu + down_pieces(k - 1)
        _interleave(elementwise_pieces(k) + side[k], mxu)
    _interleave([], down_pieces(n_items - 1))
    _interleave(finish_pieces(n_sub - 1), [])

    if not sample:
        st_ref[...] = xp_ref[rows + SUBLANES - 2:rows + SUBLANES, :]
        xp_ref[0:SUBLANES, :] = xp_ref[rows:rows + SUBLANES, :]


def _ffn_scratch(rows, off):
    return [
        pltpu.VMEM((rows, D_MODEL), BF16),
        pltpu.VMEM((rows + off, 2 * D_FF), F32),
        pltpu.VMEM((2, FFN_SUB, FFN_CB), BF16),
        pltpu.VMEM((rows, D_MODEL), F32),
    ]


def _ffn_prompt_call(l, h, mod, p, w_up_b, w_dn_b):
    ts = TS_FFN
    n_tiles = SEQ // ts
    n_steps = BATCH * n_tiles
    convert_next = l + 1 < DEPTH
    ls = functools.partial(_layer_spec, l=l, n_grid=2)
    cs = functools.partial(_const_spec, n_grid=2)
    kern = functools.partial(_ffn_kernel, rows=ts, sample=False, convert_next=convert_next)
    slab = D_MODEL // n_steps
    nxt_specs, nxt_args, nxt_out_specs, nxt_out_shapes = [], (), [], []
    if convert_next:
        nxt_specs = [pl.BlockSpec((None, slab, N_IN), lambda b, j: (l + 1, b * n_tiles + j, 0)),
                     pl.BlockSpec((None, slab, D_MODEL), lambda b, j: (l + 1, b * n_tiles + j, 0))]
        nxt_args = (p["w_in_f32"], p["w_o_f32"])
        nxt_out_specs = [pl.BlockSpec((slab, N_IN_PAD), lambda b, j: (b * n_tiles + j, 0)),
                         pl.BlockSpec((slab, D_MODEL), lambda b, j: (b * n_tiles + j, 0))]
        nxt_out_shapes = [jax.ShapeDtypeStruct((D_MODEL, N_IN_PAD), BF16),
                          jax.ShapeDtypeStruct((D_MODEL, D_MODEL), BF16)]
    return pl.pallas_call(
        kern,
        grid=(BATCH, n_tiles),
        in_specs=[
            pl.BlockSpec((ts, D_MODEL), lambda b, j: (b * n_tiles + j, 0)),
            pl.BlockSpec((None, None, 6, D_MODEL), lambda b, j: (l, b, 0, 0)),
            *nxt_specs,
            cs((D_MODEL, 2 * D_FF)),
            ls((3, 2 * D_FF)),
            ls((1, 2 * D_FF)),
            cs((D_FF, D_MODEL)),
            ls((1, D_MODEL)),
            ls((1, D_MODEL)),
        ],
        out_specs=[
            pl.BlockSpec((ts, D_MODEL), lambda b, j: (b * n_tiles + j, 0)),
            pl.BlockSpec((None, 2, 2 * D_FF), lambda b, j: (b, 0, 0)),
            *nxt_out_specs,
        ],
        out_shape=[
            jax.ShapeDtypeStruct((BATCH * SEQ, D_MODEL), F32),
            jax.ShapeDtypeStruct((BATCH, 2, 2 * D_FF), F32),
            *nxt_out_shapes,
        ],
        scratch_shapes=_ffn_scratch(ts, SUBLANES),
        compiler_params=pltpu.CompilerParams(
            dimension_semantics=("arbitrary", "arbitrary"), vmem_limit_bytes=VMEM_LIMIT),
        name="ffn_prompt",
    )(h, mod, *nxt_args, w_up_b, p["ffn_conv_w"], p["ffn_conv_b"], w_dn_b, p["ln2_g"], p["ln2_b"])


def _ffn_sample_call(l, h3, mod, past, p, w_up_b, w_dn_b):
    tb = TB_S
    rr = tb * DEC_SEQ
    ls = functools.partial(_layer_spec, l=l, n_grid=1)
    cs = functools.partial(_const_spec, n_grid=1)
    kern = functools.partial(_ffn_kernel, rows=rr, sample=True, convert_next=False)
    return pl.pallas_call(
        kern,
        grid=(DEC_BATCH // tb,),
        in_specs=[
            pl.BlockSpec((tb, DEC_SEQ, D_MODEL), lambda i: (i, 0, 0)),
            pl.BlockSpec((None, tb, 6, D_MODEL), lambda i: (l, i, 0, 0)),
            pl.BlockSpec((None, tb, 2, 2 * D_FF), lambda i: (l, i, 0, 0)),
            cs((D_MODEL, 2 * D_FF)),
            ls((3, 2 * D_FF)),
            ls((1, 2 * D_FF)),
            cs((D_FF, D_MODEL)),
            ls((1, D_MODEL)),
            ls((1, D_MODEL)),
        ],
        out_specs=[
            pl.BlockSpec((tb, DEC_SEQ, D_MODEL), lambda i: (i, 0, 0)),
            pl.BlockSpec((tb, 2, 2 * D_FF), lambda i: (i, 0, 0)),
        ],
        out_shape=[
            jax.ShapeDtypeStruct((DEC_BATCH, DEC_SEQ, D_MODEL), F32),
            jax.ShapeDtypeStruct((DEC_BATCH, 2, 2 * D_FF), F32),
        ],
        scratch_shapes=_ffn_scratch(rr, 0),
        compiler_params=pltpu.CompilerParams(
            dimension_semantics=("arbitrary",), vmem_limit_bytes=VMEM_LIMIT),
        name="ffn_sample",
    )(h3, mod, past, w_up_b, p["ffn_conv_w"], p["ffn_conv_b"], w_dn_b, p["ln2_g"], p["ln2_b"])


def _prep_params(w_in, conv_a_w, sgu_ln_g, sgu_ln_b, sgu_w, sgu_b, gla_w_a2, gla_b_a, gla_norm_g,
                 w_o, ln1_g, ln1_b, ffn_w_up, ffn_conv_w, ffn_conv_b, ffn_w_down, ln2_g, ln2_b):
    tril = jnp.tril(jnp.ones((CHUNK, CHUNK), bool))
    w_p = jnp.where(tril[None, None], sgu_w, 0.0)
    wcat_p = jnp.transpose(w_p, (0, 2, 1, 3)).reshape(DEPTH, CHUNK, H_B * CHUNK)
    tril8 = jnp.tril(jnp.ones((DEC_SEQ, DEC_SEQ), bool))
    w8 = jnp.where(tril8[None, None], sgu_w[:, :, :DEC_SEQ, :DEC_SEQ], 0.0)
    eye = jnp.eye(GROUP_SEQ, dtype=F32)
    w_blk = jnp.einsum("ij,lhts->lhitjs", eye, w8).reshape(DEPTH, H_B, CHUNK, CHUNK)
    wcat_s = jnp.transpose(w_blk, (0, 2, 1, 3)).reshape(DEPTH, CHUNK, H_B * CHUNK)
    sbias_p = jnp.repeat(jnp.transpose(sgu_b, (0, 2, 1)), DH_B, axis=2)
    sbias_s = jnp.tile(sbias_p[:, :DEC_SEQ, :], (1, GROUP_SEQ, 1))
    return {
        "w_in0": jnp.pad(w_in[0], ((0, 0), (0, N_IN_PAD - N_IN))).astype(BF16),
        "w_o0": w_o[0].astype(BF16),
        "w_in_f32": w_in, "w_o_f32": w_o, "ffn_w_up": ffn_w_up, "ffn_w_down": ffn_w_down,
        "conv_a_w": conv_a_w,
        "sgu_ln_g": sgu_ln_g.reshape(DEPTH, 1, D_B),
        "sgu_ln_b": sgu_ln_b.reshape(DEPTH, 1, D_B),
        "wcat_p": wcat_p, "wcat_s": wcat_s, "sbias_p": sbias_p, "sbias_s": sbias_s,
        "w_a2": jnp.pad(gla_w_a2, ((0, 0), (0, RANK_PAD - GATE_RANK), (0, 0))).astype(BF16),
        "b_a": gla_b_a.reshape(DEPTH, 1, H_C * DK_C),
        "gla_norm_g": gla_norm_g.reshape(DEPTH, 1, DV_C),
        "ln1_g": ln1_g.reshape(DEPTH, 1, D_MODEL), "ln1_b": ln1_b.reshape(DEPTH, 1, D_MODEL),
        "ffn_conv_w": ffn_conv_w,
        "ffn_conv_b": ffn_conv_b.reshape(DEPTH, 1, 2 * D_FF),
        "ln2_g": ln2_g.reshape(DEPTH, 1, D_MODEL), "ln2_b": ln2_b.reshape(DEPTH, 1, D_MODEL),
    }


def kernel(x_prompt, x_sample, c_prompt, c_sample, state_conv_a, state_gla, state_ffn_conv,
           ln_in_g, ln_in_b, w_ada, b_ada, w_in, conv_a_w, sgu_ln_g, sgu_ln_b, sgu_w, sgu_b,
           gla_w_a2, gla_b_a, gla_norm_g, w_o, ln1_g, ln1_b, ffn_w_up, ffn_conv_w, ffn_conv_b,
           ffn_w_down, ln2_g, ln2_b):
    p = _prep_params(w_in, conv_a_w, sgu_ln_g, sgu_ln_b, sgu_w, sgu_b, gla_w_a2, gla_b_a,
                     gla_norm_g, w_o, ln1_g, ln1_b, ffn_w_up, ffn_conv_w, ffn_conv_b,
                     ffn_w_down, ln2_g, ln2_b)
    mod = _ada_call(jnp.concatenate([c_prompt, c_sample], axis=0), w_ada, b_ada)
    mod = mod.reshape(DEPTH, BATCH + DEC_BATCH, 6, D_MODEL)
    mod_p, mod_s = mod[:, :BATCH], mod[:, BATCH:]
    s0_all = state_gla.reshape(DEPTH, DEC_BATCH, H_C * DK_C, DV_C)

    p["ln_in_g"] = ln_in_g.reshape(1, D_MODEL)
    p["ln_in_b"] = ln_in_b.reshape(1, D_MODEL)
    hp = x_prompt.reshape(BATCH * SEQ, D_MODEL)
    hs = x_sample

    outs_p = [[], [], [], []]
    outs_s = [[], [], [], []]
    w_in_b, w_o_b = p["w_in0"], p["w_o0"]
    for l in range(DEPTH):
        hp, ca, cv, gs, w_up_b, w_dn_b = _mixer_prompt_call(l, hp, mod_p, p, w_in_b, w_o_b)
        hs, ca_s_l, cv_s_l, gs_s_l = _mixer_sample_call(l, hs, mod_s, state_conv_a, s0_all, p,
                                                        w_in_b, w_o_b)
        ffn_out = _ffn_prompt_call(l, hp, mod_p, p, w_up_b, w_dn_b)
        hp, fs = ffn_out[0], ffn_out[1]
        if l + 1 < DEPTH:
            w_in_b, w_o_b = ffn_out[2], ffn_out[3]
        for acc, val in zip(outs_p, (ca, cv, gs, fs)):
            acc.append(val)
        hs, fs = _ffn_sample_call(l, hs, mod_s, state_ffn_conv, p, w_up_b, w_dn_b)
        for acc, val in zip(outs_s, (ca_s_l, cv_s_l, gs_s_l, fs)):
            acc.append(val)

    ca_p, cv_p, gla_p, ffn_p = [jnp.stack(v) for v in outs_p]
    ca_s, cv_s, gla_s, ffn_s = [jnp.stack(v) for v in outs_s]
    return (hp.reshape(BATCH, SEQ, D_MODEL), hs,
            ca_p, ca_s, cv_p, cv_s,
            gla_p.reshape(DEPTH, BATCH, H_C, DK_C, DV_C),
            gla_s.reshape(DEPTH, DEC_BATCH, H_C, DK_C, DV_C),
            ffn_p, ffn_s)
```

```python
import functools
import math

import jax
import jax.numpy as jnp
from jax import lax
from jax.experimental import pallas as pl
from jax.experimental.pallas import tpu as pltpu

D_MODEL = 1024
BATCH = 8
SEQ = 2048
DEPTH = 4
DEC_BATCH = 128
DEC_SEQ = 8
D_A = 256
D_B = 256
H_B = 4
DH_B = 64
D_C = 512
H_C = 4
DV_C = 128
DK_C = 64
GATE_RANK = 16
GATE_TAU = 16.0
CHUNK = 128
D_FF = 2816
LN_EPS = 1e-5
ALPHA = (2.0 * DEPTH) ** 0.25
N_IN = 2832
N_IN_PAD = 2944
RANK_PAD = 128

O_AIN, O_AB, O_AC, O_SU, O_SV, O_Q, O_K, O_V, O_G, O_R = (
    0, 256, 512, 768, 1024, 1280, 1536, 1792, 2304, 2816)

SUBLANES = 8
VMEM_LIMIT = 56 * 1024 * 1024

TS_MIX = 1024
MIX_RB = 256
MIX_CB = 256
NORM_RB = 32
TS_FFN = 256
TB_S = 32
GROUP_SEQ = CHUNK // DEC_SEQ
SEQ_UNROLL = 4

F32 = jnp.float32
BF16 = jnp.bfloat16


def _dot(a, b):
    return jnp.dot(a, b, preferred_element_type=F32)


def _dot_nt(a, b):
    return lax.dot_general(a, b, (((1,), (1,)), ((), ())), preferred_element_type=F32)


def _dot_tn(a, b):
    return lax.dot_general(a, b, (((0,), (0,)), ((), ())), preferred_element_type=F32)


def _layer_norm(x, g, b):
    mu = jnp.mean(x, axis=-1, keepdims=True)
    xc = x - mu
    var = jnp.mean(xc * xc, axis=-1, keepdims=True)
    return xc * lax.rsqrt(var + LN_EPS) * g + b


def _gelu(x):
    c = math.sqrt(2.0 / math.pi)
    return 0.5 * x * (1.0 + jnp.tanh(c * (x + 0.044715 * (x * x * x))))


def _silu(x):
    return x * jax.nn.sigmoid(x)


def _log_sigmoid(x):
    return jnp.minimum(x, 0.0) - jnp.log1p(jnp.exp(-jnp.abs(x)))


def _split3(x):
    hi = x.astype(BF16)
    r1 = x - hi.astype(F32)
    mid = r1.astype(BF16)
    lo = (r1 - mid.astype(F32)).astype(BF16)
    return hi, mid, lo


def _dot_exact_lhs(m_bf16, x):
    hi, mid, lo = _split3(x)
    return _dot(m_bf16, hi) + _dot(m_bf16, mid) + _dot(m_bf16, lo)


def _head_masks(width, n_heads):
    lane = lax.broadcasted_iota(jnp.int32, (1, width), 1)
    per = width // n_heads
    return [(lane >= h * per) & (lane < (h + 1) * per) for h in range(n_heads)]


def _interleave(valu_pieces, mxu_pieces):
    n_v, n_m = len(valu_pieces), len(mxu_pieces)
    done = 0
    for k, mp in enumerate(mxu_pieces):
        upto = (k + 1) * n_v // (n_m + 1)
        for vp in valu_pieces[done:upto]:
            vp()
        done = upto
        mp()
    for vp in valu_pieces[done:]:
        vp()


def _ada_kernel(c_ref, w_ref, b_ref, o_ref):
    cs = _silu(c_ref[...]).astype(BF16)
    m = _dot(cs, w_ref[...].astype(BF16)) + b_ref[...]
    for k in range(6):
        @pl.when(pl.program_id(1) == k)
        def _():
            o_ref[:, k, :] = m


def _ada_call(c_all, w_ada, b_ada):
    n_rows = c_all.shape[0]
    return pl.pallas_call(
        _ada_kernel,
        grid=(DEPTH, 6),
        in_specs=[
            pl.BlockSpec((n_rows, D_MODEL), lambda l, j: (0, 0)),
            pl.BlockSpec((None, D_MODEL, D_MODEL), lambda l, j: (l, 0, j)),
            pl.BlockSpec((None, 1, D_MODEL), lambda l, j: (l, 0, j)),
        ],
        out_specs=pl.BlockSpec((None, n_rows, 6, D_MODEL), lambda l, j: (l, 0, 0, 0)),
        out_shape=jax.ShapeDtypeStruct((DEPTH, n_rows, 6, D_MODEL), F32),
        compiler_params=pltpu.CompilerParams(
            dimension_semantics=("arbitrary", "arbitrary"), vmem_limit_bytes=VMEM_LIMIT),
        name="ada_mod",
    )(c_all, w_ada, b_ada.reshape(DEPTH, 1, 6 * D_MODEL))


def _to_bf16_padded(src_ref, dst_ref):
    n = src_ref.shape[-1]
    if dst_ref.shape[-1] != n:
        dst_ref[...] = jnp.zeros_like(dst_ref)
    dst_ref[:, 0:n] = src_ref[...].astype(BF16)


def _convert_kernel(winf_ref, wof_ref, winb_ref, wob_ref):
    _to_bf16_padded(winf_ref, winb_ref)
    _to_bf16_padded(wof_ref, wob_ref)


def _convert_layer0_call(w_in, w_o):
    tr = 128
    return pl.pallas_call(
        _convert_kernel,
        grid=(D_MODEL // tr,),
        in_specs=[
            pl.BlockSpec((None, tr, N_IN), lambda i: (0, i, 0)),
            pl.BlockSpec((None, tr, D_MODEL), lambda i: (0, i, 0)),
        ],
        out_specs=[
            pl.BlockSpec((tr, N_IN_PAD), lambda i: (i, 0)),
            pl.BlockSpec((tr, D_MODEL), lambda i: (i, 0)),
        ],
        out_shape=[
            jax.ShapeDtypeStruct((D_MODEL, N_IN_PAD), BF16),
            jax.ShapeDtypeStruct((D_MODEL, D_MODEL), BF16),
        ],
        compiler_params=pltpu.CompilerParams(dimension_semantics=("arbitrary",)),
        name="convert_layer0",
    )(w_in, w_o)


def _sgu_chunk(z_ref, zrows, y_ref, rows, wcat, sbias, ln_g, ln_b, masks_b):
    su = _gelu(z_ref[zrows, O_SU:O_SU + D_B])
    sv = _layer_norm(_gelu(z_ref[zrows, O_SV:O_SV + D_B]), ln_g, ln_b)
    svb = sv.astype(BF16)
    zero = jnp.zeros_like(svb)
    svm = jnp.concatenate([jnp.where(m, svb, zero) for m in masks_b], axis=0)
    mixed = _dot(wcat, svm) + sbias
    y_ref[rows, D_A:D_A + D_B] = (su * mixed).astype(BF16)
    return sv


def _gla_gate(z_ref, rows, wa2, ba):
    cr = z_ref[rows, O_R:O_R + RANK_PAD].astype(BF16)
    return _log_sigmoid(_dot(cr, wa2) + ba) * (1.0 / GATE_TAU)


def _gla_head_out(z_ref, zrows, y_ref, rows, hh, o, gn):
    ms = jnp.mean(o * o, axis=-1, keepdims=True)
    on = o * lax.rsqrt(ms + LN_EPS) * gn
    g = z_ref[zrows, O_G + hh * DV_C:O_G + (hh + 1) * DV_C]
    y_ref[rows, D_A + D_B + hh * DV_C:D_A + D_B + (hh + 1) * DV_C] = (_silu(g) * on).astype(BF16)


def _const_spec(shape, n_grid):
    zeros = (0,) * len(shape)
    if n_grid == 1:
        return pl.BlockSpec(shape, lambda i: zeros, pipeline_mode=pl.Buffered(1))
    return pl.BlockSpec(shape, lambda b, j: zeros, pipeline_mode=pl.Buffered(1))


def _layer_spec(shape, l, n_grid):
    idx = (l,) + (0,) * len(shape)
    if n_grid == 1:
        return pl.BlockSpec((None,) + shape, lambda i: idx, pipeline_mode=pl.Buffered(1))
    return pl.BlockSpec((None,) + shape, lambda b, j: idx, pipeline_mode=pl.Buffered(1))


def _mixer_prompt_kernel(*refs, ts, n_tiles, first):
    if first:
        (x_ref, mod_ref, wupf_ref, wdnf_ref, lnig_ref, lnib_ref, win_ref, cw_ref, sg_ref, sb_ref,
         wcat_ref, sbias_ref, wa2_ref, ba_ref, gn_ref, wo_ref, lng_ref, lnb_ref,
         hout_ref, ca_ref, cv_ref, gla_ref, wupb_ref, wdnb_ref,
         u_ref, z_ref, xa_ref, y_ref, st_ref, yo_ref, h_ref) = refs
        for r in range(0, ts, NORM_RB):
            h_ref[r:r + NORM_RB, :] = _layer_norm(x_ref[r:r + NORM_RB, :], lnig_ref[...], lnib_ref[...])
    else:
        (h_ref, mod_ref, wupf_ref, wdnf_ref, win_ref, cw_ref, sg_ref, sb_ref,
         wcat_ref, sbias_ref, wa2_ref, ba_ref, gn_ref, wo_ref, lng_ref, lnb_ref,
         hout_ref, ca_ref, cv_ref, gla_ref, wupb_ref, wdnb_ref,
         u_ref, z_ref, xa_ref, y_ref, st_ref, yo_ref) = refs
    wupb_ref[...] = wupf_ref[...].astype(BF16)
    wdnb_ref[...] = wdnf_ref[...].astype(BF16)
    j = pl.program_id(1)

    @pl.when(j == 0)
    def _():
        xa_ref[0:SUBLANES, :] = jnp.zeros((SUBLANES, D_A), F32)
        st_ref[...] = jnp.zeros_like(st_ref)

    sh1, sc1, g1 = mod_ref[0:1, :], mod_ref[1:2, :], mod_ref[2:3, :]
    for r in range(0, ts, NORM_RB):
        u_ref[r:r + NORM_RB, :] = (h_ref[r:r + NORM_RB, :] * (1.0 + sc1) + sh1).astype(BF16)

    def z_rows(r0, n):
        p, local = divmod(r0, MIX_RB)
        base = (p % 2) * MIX_RB + local
        return slice(base, base + n)

    def in_proj_pieces(p):
        rows = slice(p * MIX_RB, (p + 1) * MIX_RB)
        zrows = z_rows(p * MIX_RB, MIX_RB)

        def piece(c0):
            cols = slice(c0, min(c0 + MIX_CB, N_IN_PAD))
            z_ref[zrows, cols] = _dot(u_ref[rows, :], win_ref[:, cols])
        return [functools.partial(piece, c0) for c0 in range(0, N_IN_PAD, MIX_CB)]

    def out_proj_pieces(p):
        rows = slice(p * MIX_RB, (p + 1) * MIX_RB)

        def piece(c0):
            cols = slice(c0, c0 + MIX_CB)
            yo_ref[rows, cols] = _dot(y_ref[rows, :], wo_ref[:, cols])
        return [functools.partial(piece, c0) for c0 in range(0, D_MODEL, MIX_CB)]

    def norm_pieces(p):
        def piece(r):
            rows = slice(r, r + NORM_RB)
            hout_ref[rows, :] = _layer_norm(ALPHA * h_ref[rows, :] + g1 * yo_ref[rows, :],
                                            lng_ref[...], lnb_ref[...])
        return [functools.partial(piece, r)
                for r in range(p * MIX_RB, (p + 1) * MIX_RB, NORM_RB)]

    cw = cw_ref[...]
    wcat = wcat_ref[...].astype(BF16)
    sbias = sbias_ref[...]
    wa2 = wa2_ref[...]
    ba = ba_ref[...]
    gn = gn_ref[...]
    sg, sb = sg_ref[...], sb_ref[...]
    masks_b = _head_masks(D_B, H_B)
    masks_k = _head_masks(H_C * DK_C, H_C)
    row = lax.broadcasted_iota(jnp.int32, (CHUNK, CHUNK), 0)
    col = lax.broadcasted_iota(jnp.int32, (CHUNK, CHUNK), 1)
    causal = col <= row
    tri = causal.astype(BF16)

    def mix_pieces(c):
        r0 = c * CHUNK
        rows = slice(r0, r0 + CHUNK)
        zrows = z_rows(r0, CHUNK)
        v = {}

        def conv_a():
            x = z_ref[zrows, O_AC:O_AC + D_A] * z_ref[zrows, O_AIN:O_AIN + D_A]
            xa_ref[r0 + SUBLANES:r0 + SUBLANES + CHUNK, :] = x
            win = xa_ref[r0:r0 + CHUNK + SUBLANES, :]
            conv = (cw[0:1, :] * win[SUBLANES - 2:SUBLANES - 2 + CHUNK]
                    + cw[1:2, :] * win[SUBLANES - 1:SUBLANES - 1 + CHUNK]
                    + cw[2:3, :] * x)
            y_ref[rows, 0:D_A] = (z_ref[zrows, O_AB:O_AB + D_A] * conv).astype(BF16)

        def sgu():
            cv_ref[...] = _sgu_chunk(z_ref, zrows, y_ref, rows, wcat, sbias, sg, sb, masks_b)

        def gla_prep():
            lg = _gla_gate(z_ref, zrows, wa2, ba)
            b = _dot_exact_lhs(tri, lg)
            b_last = b[CHUNK - 1:CHUNK, :]
            b_mid = b[CHUNK // 2 - 1:CHUNK // 2, :]
            q = z_ref[zrows, O_Q:O_Q + H_C * DK_C]
            k = z_ref[zrows, O_K:O_K + H_C * DK_C]
            v["qt"] = (q * (DK_C ** -0.5)) * jnp.exp(b - b_mid)
            v["kt"] = (k * jnp.exp(b_mid - b)).astype(BF16)
            v["kh"] = k * jnp.exp(b_last - b)
            st = st_ref[...]
            v["stb"] = (st * jnp.exp(b_mid)).astype(BF16)
            v["st_new"] = st * jnp.exp(b_last)

        def gla_head(hh):
            zq = jnp.zeros_like(v["qt"])
            qm = jnp.where(masks_k[hh], v["qt"], zq).astype(BF16)
            a = jnp.where(causal, _dot_nt(qm, v["kt"]), 0.0).astype(BF16)
            vb = z_ref[zrows, O_V + hh * DV_C:O_V + (hh + 1) * DV_C].astype(BF16)
            o = _dot(a, vb) + _dot_nt(qm, v["stb"])
            _gla_head_out(z_ref, zrows, y_ref, rows, hh, o, gn)
            khm = jnp.where(masks_k[hh], v["kh"], zq).astype(BF16)
            v["st_new"] = v["st_new"] + _dot_tn(vb, khm)

        def gla_state():
            st_ref[...] = v["st_new"]

        return ([conv_a, sgu, gla_prep] + [functools.partial(gla_head, hh) for hh in range(H_C)]
                + [gla_state])

    n_rb = ts // MIX_RB
    per = MIX_RB // CHUNK
    _interleave([], in_proj_pieces(0))
    for p in range(n_rb):
        valu = [pc for c in range(p * per, (p + 1) * per) for pc in mix_pieces(c)]
        if p >= 2:
            valu = valu + norm_pieces(p - 2)
        mxu = in_proj_pieces(p + 1) if p + 1 < n_rb else []
        if p >= 1:
            mxu = mxu + out_proj_pieces(p - 1)
        _interleave(valu, mxu)
    _interleave(norm_pieces(n_rb - 2) if n_rb >= 2 else [], out_proj_pieces(n_rb - 1))
    _interleave(norm_pieces(n_rb - 1), [])

    ca_ref[...] = xa_ref[ts + SUBLANES - 2:ts + SUBLANES, :]
    xa_ref[0:SUBLANES, :] = xa_ref[ts:ts + SUBLANES, :]

    @pl.when(j == n_tiles - 1)
    def _():
        gla_ref[...] = st_ref[...].T


def _mixer_prompt_call(l, h, mod, p, w_in_b, w_o_b):
    ts = TS_MIX
    n_tiles = SEQ // ts
    n_steps = BATCH * n_tiles
    first = l == 0
    ls = functools.partial(_layer_spec, l=l, n_grid=2)
    cs = functools.partial(_const_spec, n_grid=2)
    kern = functools.partial(_mixer_prompt_kernel, ts=ts, n_tiles=n_tiles, first=first)
    ln_in_specs = [pl.BlockSpec((1, D_MODEL), lambda b, j: (0, 0))] * 2 if first else []
    ln_in_args = (p["ln_in_g"], p["ln_in_b"]) if first else ()
    ln_in_scratch = [pltpu.VMEM((ts, D_MODEL), F32)] if first else []
    up_rows = D_MODEL // n_steps
    dn_rows = D_FF // n_steps
    return pl.pallas_call(
        kern,
        grid=(BATCH, n_tiles),
        in_specs=[
            pl.BlockSpec((ts, D_MODEL), lambda b, j: (b * n_tiles + j, 0)),
            pl.BlockSpec((None, None, 6, D_MODEL), lambda b, j: (l, DEC_BATCH + b, 0, 0)),
            pl.BlockSpec((None, up_rows, 2 * D_FF), lambda b, j: (l, b * n_tiles + j, 0)),
            pl.BlockSpec((None, dn_rows, D_MODEL), lambda b, j: (l, b * n_tiles + j, 0)),
            *ln_in_specs,
            cs((D_MODEL, N_IN_PAD)),
            ls((3, D_A)),
            ls((1, D_B)),
            ls((1, D_B)),
            ls((CHUNK, H_B * CHUNK)),
            ls((CHUNK, D_B)),
            ls((RANK_PAD, H_C * DK_C)),
            ls((1, H_C * DK_C)),
            ls((1, DV_C)),
            cs((D_MODEL, D_MODEL)),
            ls((1, D_MODEL)),
            ls((1, D_MODEL)),
        ],
        out_specs=[
            pl.BlockSpec((ts, D_MODEL), lambda b, j: (b * n_tiles + j, 0)),
            pl.BlockSpec((None, 2, D_A), lambda b, j: (b, 0, 0)),
            pl.BlockSpec((None, CHUNK, D_B), lambda b, j: (b, 0, 0)),
            pl.BlockSpec((None, H_C * DK_C, DV_C), lambda b, j: (b, 0, 0)),
            pl.BlockSpec((up_rows, 2 * D_FF), lambda b, j: (b * n_tiles + j, 0)),
            pl.BlockSpec((dn_rows, D_MODEL), lambda b, j: (b * n_tiles + j, 0)),
        ],
        out_shape=[
            jax.ShapeDtypeStruct((BATCH * SEQ, D_MODEL), F32),
            jax.ShapeDtypeStruct((BATCH, 2, D_A), F32),
            jax.ShapeDtypeStruct((BATCH, CHUNK, D_B), F32),
            jax.ShapeDtypeStruct((BATCH, H_C * DK_C, DV_C), F32),
            jax.ShapeDtypeStruct((D_MODEL, 2 * D_FF), BF16),
            jax.ShapeDtypeStruct((D_FF, D_MODEL), BF16),
        ],
        scratch_shapes=[
            pltpu.VMEM((ts, D_MODEL), BF16),
            pltpu.VMEM((2 * MIX_RB, N_IN_PAD), F32),
            pltpu.VMEM((ts + SUBLANES, D_A), F32),
            pltpu.VMEM((ts, D_MODEL), BF16),
            pltpu.VMEM((DV_C, H_C * DK_C), F32),
            pltpu.VMEM((ts, D_MODEL), F32),
            *ln_in_scratch,
        ],
        compiler_params=pltpu.CompilerParams(
            dimension_semantics=("arbitrary", "arbitrary"), vmem_limit_bytes=VMEM_LIMIT),
        name="mixer_prompt",
    )(h, mod, p["ffn_w_up"], p["ffn_w_down"], *ln_in_args, w_in_b, p["conv_a_w"], p["sgu_ln_g"],
      p["sgu_ln_b"], p["wcat_p"], p["sbias_p"], p["w_a2"], p["b_a"], p["gla_norm_g"], w_o_b,
      p["ln1_g"], p["ln1_b"])


def _expand_rows(x3):
    n, _, c = x3.shape
    return jnp.broadcast_to(x3, (n, DEC_SEQ, c)).reshape(n * DEC_SEQ, c)


def _sample_conv(x, past, cw):
    p0 = _expand_rows(past[:, 0:1, :])
    p1 = _expand_rows(past[:, 1:2, :])
    t_idx = lax.broadcasted_iota(jnp.int32, x.shape, 0) & (DEC_SEQ - 1)
    xm1 = jnp.where(t_idx >= 1, pltpu.roll(x, 1, 0), p1)
    xm2 = jnp.where(t_idx >= 2, pltpu.roll(x, 2, 0), jnp.where(t_idx == 1, p1, p0))
    return cw[0:1, :] * xm2 + cw[1:2, :] * xm1 + cw[2:3, :] * x


def _mixer_sample_kernel(*refs, tb, first):
    if first:
        (h_ref, mod_ref, lnig_ref, lnib_ref, past_ref, s0_ref, win_ref, cw_ref, sg_ref, sb_ref,
         wcat_ref, sbias_ref, wa2_ref, ba_ref, gn_ref, wo_ref, lng_ref, lnb_ref,
         hout_ref, ca_ref, cv_ref, gla_ref,
         z_ref, y_ref, qt_ref, kh_ref, eb_ref, oi_ref, os_ref) = refs
    else:
        (h_ref, mod_ref, past_ref, s0_ref, win_ref, cw_ref, sg_ref, sb_ref,
         wcat_ref, sbias_ref, wa2_ref, ba_ref, gn_ref, wo_ref, lng_ref, lnb_ref,
         hout_ref, ca_ref, cv_ref, gla_ref,
         z_ref, y_ref, qt_ref, kh_ref, eb_ref, oi_ref, os_ref) = refs
    rr = tb * DEC_SEQ
    h = h_ref[...].reshape(rr, D_MODEL)
    if first:
        h = _layer_norm(h, lnig_ref[...], lnib_ref[...])
    sh1 = _expand_rows(mod_ref[:, 0:1, :])
    sc1 = _expand_rows(mod_ref[:, 1:2, :])
    g1 = _expand_rows(mod_ref[:, 2:3, :])
    u = (h * (1.0 + sc1) + sh1).astype(BF16)
    z_ref[...] = _dot(u, win_ref[...])

    x = z_ref[:, O_AC:O_AC + D_A] * z_ref[:, O_AIN:O_AIN + D_A]
    conv = _sample_conv(x, past_ref[...], cw_ref[...])
    y_ref[:, 0:D_A] = (z_ref[:, O_AB:O_AB + D_A] * conv).astype(BF16)
    ca_ref[...] = x.reshape(tb, DEC_SEQ, D_A)[:, DEC_SEQ - 2:DEC_SEQ, :]

    wcat = wcat_ref[...].astype(BF16)
    sbias = sbias_ref[...]
    wa2 = wa2_ref[...]
    ba = ba_ref[...]
    gn = gn_ref[...]
    sg, sb = sg_ref[...], sb_ref[...]
    masks_b = _head_masks(D_B, H_B)
    masks_k = _head_masks(H_C * DK_C, H_C)
    row = lax.broadcasted_iota(jnp.int32, (CHUNK, CHUNK), 0)
    col = lax.broadcasted_iota(jnp.int32, (CHUNK, CHUNK), 1)
    same_seq = (row // DEC_SEQ) == (col // DEC_SEQ)
    causal = same_seq & (col <= row)
    tri = causal.astype(BF16)
    blk = same_seq.astype(BF16)

    for gi in range(rr // CHUNK):
        rows = pl.ds(gi * CHUNK, CHUNK)
        sv = _sgu_chunk(z_ref, rows, y_ref, rows, wcat, sbias, sg, sb, masks_b)
        cv_ref[gi * GROUP_SEQ:(gi + 1) * GROUP_SEQ] = sv.reshape(GROUP_SEQ, DEC_SEQ, D_B)
        lg = _gla_gate(z_ref, rows, wa2, ba)
        b = _dot_exact_lhs(tri, lg)
        b_tot = _dot_exact_lhs(blk, lg)
        q = z_ref[rows, O_Q:O_Q + H_C * DK_C]
        k = z_ref[rows, O_K:O_K + H_C * DK_C]
        qt = (q * (DK_C ** -0.5)) * jnp.exp(b)
        kt = (k * jnp.exp(-b)).astype(BF16)
        qt_ref[rows, :] = qt
        kh_ref[rows, :] = k * jnp.exp(b_tot - b)
        eb_ref[rows, :] = jnp.exp(b_tot)
        zq = jnp.zeros_like(qt)
        for hh in range(H_C):
            qm = jnp.where(masks_k[hh], qt, zq).astype(BF16)
            a = jnp.where(causal, _dot_nt(qm, kt), 0.0).astype(BF16)
            vb = z_ref[rows, O_V + hh * DV_C:O_V + (hh + 1) * DV_C].astype(BF16)
            oi_ref[rows, hh * DV_C:(hh + 1) * DV_C] = _dot(a, vb)

    ones16 = jnp.ones((2 * SUBLANES, DV_C), BF16)
    sub = lax.broadcasted_iota(jnp.int32, (SUBLANES, H_C * DK_C), 0)

    def seq_body(i, carry):
        r0 = pl.multiple_of(i * DEC_SEQ, DEC_SEQ)
        rows = pl.ds(r0, DEC_SEQ)
        s0 = s0_ref[i]
        qt = qt_ref[rows, :]
        kh = kh_ref[rows, :]
        zq = jnp.zeros_like(qt)
        q_exp = jnp.concatenate([jnp.where(m, qt, zq) for m in masks_k], axis=0).astype(BF16)
        k_exp = jnp.concatenate([jnp.where(m, kh, zq) for m in masks_k], axis=0).astype(BF16)
        o_int = _dot(q_exp, s0.astype(BF16))
        v = z_ref[rows, O_V:O_V + D_C]
        v_exp = jnp.concatenate([v[:, hh * DV_C:(hh + 1) * DV_C] for hh in range(H_C)],
                                axis=0).astype(BF16)
        eb = eb_ref[rows, :]
        hi, mid, lo = _split3(eb)
        e3 = jnp.where(sub == 0, hi.astype(F32),
                       jnp.where(sub == 1, mid.astype(F32),
                                 jnp.where(sub == 2, lo.astype(F32), jnp.zeros_like(eb))))
        e3 = jnp.concatenate([e3, jnp.zeros_like(eb)], axis=0).astype(BF16)
        decay = _dot_tn(e3, ones16)
        gla_ref[i] = decay * s0 + _dot_tn(k_exp, v_exp)
        for hh in range(H_C):
            os_ref[rows, hh * DV_C:(hh + 1) * DV_C] = o_int[hh * DEC_SEQ:(hh + 1) * DEC_SEQ, :]
        return carry

    lax.fori_loop(0, tb, seq_body, 0, unroll=SEQ_UNROLL)

    for gi in range(rr // CHUNK):
        rows = pl.ds(gi * CHUNK, CHUNK)
        for hh in range(H_C):
            cols = slice(hh * DV_C, (hh + 1) * DV_C)
            _gla_head_out(z_ref, rows, y_ref, rows, hh, oi_ref[rows, cols] + os_ref[rows, cols], gn)

    y = _dot(y_ref[...], wo_ref[...])
    hn = _layer_norm(ALPHA * h + g1 * y, lng_ref[...], lnb_ref[...])
    hout_ref[...] = hn.reshape(tb, DEC_SEQ, D_MODEL)


def _mixer_sample_call(l, h3, mod, past, s0, p, w_in_b, w_o_b):
    tb = TB_S
    ls = functools.partial(_layer_spec, l=l, n_grid=1)
    cs = functools.partial(_const_spec, n_grid=1)
    first = l == 0
    kern = functools.partial(_mixer_sample_kernel, tb=tb, first=first)
    ln_in_specs = [pl.BlockSpec((1, D_MODEL), lambda i: (0, 0))] * 2 if first else []
    ln_in_args = (p["ln_in_g"], p["ln_in_b"]) if first else ()
    rr = tb * DEC_SEQ
    return pl.pallas_call(
        kern,
        grid=(DEC_BATCH // tb,),
        in_specs=[
            pl.BlockSpec((tb, DEC_SEQ, D_MODEL), lambda i: (i, 0, 0)),
            pl.BlockSpec((None, tb, 6, D_MODEL), lambda i: (l, i, 0, 0)),
            *ln_in_specs,
            pl.BlockSpec((None, tb, 2, D_A), lambda i: (l, i, 0, 0)),
            pl.BlockSpec((None, tb, H_C * DK_C, DV_C), lambda i: (l, i, 0, 0)),
            cs((D_MODEL, N_IN_PAD)),
            ls((3, D_A)),
            ls((1, D_B)),
            ls((1, D_B)),
            ls((CHUNK, H_B * CHUNK)),
            ls((CHUNK, D_B)),
            ls((RANK_PAD, H_C * DK_C)),
            ls((1, H_C * DK_C)),
            ls((1, DV_C)),
            cs((D_MODEL, D_MODEL)),
            ls((1, D_MODEL)),
            ls((1, D_MODEL)),
        ],
        out_specs=[
            pl.BlockSpec((tb, DEC_SEQ, D_MODEL), lambda i: (i, 0, 0)),
            pl.BlockSpec((tb, 2, D_A), lambda i: (i, 0, 0)),
            pl.BlockSpec((tb, DEC_SEQ, D_B), lambda i: (i, 0, 0)),
            pl.BlockSpec((tb, H_C * DK_C, DV_C), lambda i: (i, 0, 0)),
        ],
        out_shape=[
            jax.ShapeDtypeStruct((DEC_BATCH, DEC_SEQ, D_MODEL), F32),
            jax.ShapeDtypeStruct((DEC_BATCH, 2, D_A), F32),
            jax.ShapeDtypeStruct((DEC_BATCH, DEC_SEQ, D_B), F32),
            jax.ShapeDtypeStruct((DEC_BATCH, H_C * DK_C, DV_C), F32),
        ],
        scratch_shapes=[
            pltpu.VMEM((rr, N_IN_PAD), F32),
            pltpu.VMEM((rr, D_MODEL), BF16),
            pltpu.VMEM((rr, H_C * DK_C), F32),
            pltpu.VMEM((rr, H_C * DK_C), F32),
            pltpu.VMEM((rr, H_C * DK_C), F32),
            pltpu.VMEM((rr, D_C), F32),
            pltpu.VMEM((rr, D_C), F32),
        ],
        compiler_params=pltpu.CompilerParams(
            dimension_semantics=("arbitrary",), vmem_limit_bytes=VMEM_LIMIT),
        name="mixer_sample",
    )(h3, mod, *ln_in_args, past, s0, w_in_b, p["conv_a_w"], p["sgu_ln_g"], p["sgu_ln_b"], p["wcat_s"],
      p["sbias_s"], p["w_a2"], p["b_a"], p["gla_norm_g"], w_o_b, p["ln1_g"], p["ln1_b"])


FFN_RB = 32
FFN_CB = 256
N_CB = D_FF // FFN_CB
FFN_AHEAD = 3
FFN_AHEAD_EXTRA = 4
FFN_SUB = 256


def _ffn_kernel(*refs, rows, sample, convert_next):
    if sample:
        (h_ref, mod_ref, past_ref, wup_ref, cw_ref, cb_ref, wdn_ref, lng_ref, lnb_ref,
         hout_ref, st_ref, u_ref, xp_ref, act_ref, acc_ref) = refs
        off = 0
    else:
        if convert_next:
            (h_ref, mod_ref, winf_ref, wof_ref, wup_ref, cw_ref, cb_ref, wdn_ref, lng_ref, lnb_ref,
             hout_ref, st_ref, winb_ref, wob_ref, u_ref, xp_ref, act_ref, acc_ref) = refs
            _to_bf16_padded(winf_ref, winb_ref)
            _to_bf16_padded(wof_ref, wob_ref)
        else:
            (h_ref, mod_ref, wup_ref, cw_ref, cb_ref, wdn_ref, lng_ref, lnb_ref,
             hout_ref, st_ref, u_ref, xp_ref, act_ref, acc_ref) = refs
        off = SUBLANES

        @pl.when(pl.program_id(1) == 0)
        def _():
            xp_ref[0:SUBLANES, :] = jnp.zeros((SUBLANES, 2 * D_FF), F32)

    def rows_of(r):
        if sample:
            seqs = slice(r // DEC_SEQ, (r + FFN_RB) // DEC_SEQ)
            return (h_ref[seqs].reshape(FFN_RB, D_MODEL),
                    [_expand_rows(mod_ref[seqs, k:k + 1, :]) for k in (3, 4, 5)])
        return h_ref[r:r + FFN_RB, :], [mod_ref[k:k + 1, :] for k in (3, 4, 5)]

    def modulate_pieces(s):
        def piece(r):
            hr, (sh2, sc2, _) = rows_of(r)
            u_ref[r:r + FFN_RB, :] = (hr * (1.0 + sc2) + sh2).astype(BF16)
        return [functools.partial(piece, r)
                for r in range(s * FFN_SUB, (s + 1) * FFN_SUB, FFN_RB)]

    n_sub = rows // FFN_SUB
    items = [(s, c) for s in range(n_sub) for c in range(N_CB)]

    def block_cols(c):
        return (slice(c * FFN_CB, (c + 1) * FFN_CB),
                slice(D_FF + c * FFN_CB, D_FF + (c + 1) * FFN_CB))

    def up_pieces(k):
        s, c = items[k]
        r0 = s * FFN_SUB

        def piece(cols):
            xp_ref[off + r0:off + r0 + FFN_SUB, cols] = _dot(u_ref[r0:r0 + FFN_SUB, :], wup_ref[:, cols])
        return [functools.partial(piece, cols) for cols in block_cols(c)]

    def down_pieces(k):
        s, c = items[k]
        r0 = s * FFN_SUB

        def piece(n):
            cols = slice(n * FFN_CB, (n + 1) * FFN_CB)
            d = _dot(act_ref[k % 2], wdn_ref[c * FFN_CB:(c + 1) * FFN_CB, cols])
            if c == 0:
                acc_ref[r0:r0 + FFN_SUB, cols] = d
            else:
                acc_ref[r0:r0 + FFN_SUB, cols] += d
        return [functools.partial(piece, n) for n in range(D_MODEL // FFN_CB)]

    def conv_rows(r, cols):
        if sample:
            seqs = slice(r // DEC_SEQ, (r + FFN_RB) // DEC_SEQ)
            x = xp_ref[r:r + FFN_RB, cols]
            st_ref[seqs, :, cols] = x.reshape(FFN_RB // DEC_SEQ, DEC_SEQ, FFN_CB)[:, DEC_SEQ - 2:, :]
            y = _sample_conv(x, past_ref[seqs, :, cols], cw_ref[:, cols])
        else:
            win = xp_ref[r:r + FFN_RB + SUBLANES, cols]
            y = (cw_ref[0:1, cols] * win[SUBLANES - 2:SUBLANES - 2 + FFN_RB]
                 + cw_ref[1:2, cols] * win[SUBLANES - 1:SUBLANES - 1 + FFN_RB]
                 + cw_ref[2:3, cols] * win[SUBLANES:SUBLANES + FFN_RB])
        return y + cb_ref[0:1, cols]

    def elementwise_pieces(k):
        s, c = items[k]
        ca, cg = block_cols(c)

        def piece(r):
            act_ref[k % 2, r - s * FFN_SUB:r - s * FFN_SUB + FFN_RB, :] = (
                _silu(conv_rows(r, cg)) * conv_rows(r, ca)).astype(BF16)
        return [functools.partial(piece, r)
                for r in range(s * FFN_SUB, (s + 1) * FFN_SUB, FFN_RB)]

    def finish_pieces(s):
        def piece(r):
            hr, (_, _, g2) = rows_of(r)
            hn = _layer_norm(ALPHA * hr + g2 * acc_ref[r:r + FFN_RB, :], lng_ref[...], lnb_ref[...])
            if sample:
                seqs = slice(r // DEC_SEQ, (r + FFN_RB) // DEC_SEQ)
                hout_ref[seqs] = hn.reshape(FFN_RB // DEC_SEQ, DEC_SEQ, D_MODEL)
            else:
                hout_ref[r:r + FFN_RB, :] = hn
        return [functools.partial(piece, r)
                for r in range(s * FFN_SUB, (s + 1) * FFN_SUB, FFN_RB)]

    ahead = FFN_AHEAD + FFN_AHEAD_EXTRA * (n_sub - 1)
    side = [[] for _ in items]
    for s in range(n_sub):
        if s + 1 < n_sub:
            early = modulate_pieces(s + 1)
            span = N_CB - ahead
            for i, pc in enumerate(early):
                side[s * N_CB + i * span // len(early)].append(pc)
            late = finish_pieces(s)
            for i, pc in enumerate(late):
                side[(s + 1) * N_CB + 1 + i * (N_CB - 1) // len(late)].append(pc)

    n_items = len(items)
    _interleave(modulate_pieces(0), [])
    for k in range(ahead):
        _interleave([], up_pieces(k))
    for k in range(n_items):
        mxu = up_pieces(k + ahead) if k + ahead < n_items else []
        if k >= 1:
            mxu = mxu + down_pieces(k - 1)
        _interleave(elementwise_pieces(k) + side[k], mxu)
    _interleave([], down_pieces(n_items - 1))
    _interleave(finish_pieces(n_sub - 1), [])

    if not sample:
        st_ref[...] = xp_ref[rows + SUBLANES - 2:rows + SUBLANES, :]
        xp_ref[0:SUBLANES, :] = xp_ref[rows:rows + SUBLANES, :]


def _ffn_scratch(rows, off):
    return [
        pltpu.VMEM((rows, D_MODEL), BF16),
        pltpu.VMEM((rows + off, 2 * D_FF), F32),
        pltpu.VMEM((2, FFN_SUB, FFN_CB), BF16),
        pltpu.VMEM((rows, D_MODEL), F32),
    ]


def _ffn_prompt_call(l, h, mod, p, w_up_b, w_dn_b):
    ts = TS_FFN
    n_tiles = SEQ // ts
    n_steps = BATCH * n_tiles
    convert_next = l + 1 < DEPTH
    ls = functools.partial(_layer_spec, l=l, n_grid=2)
    cs = functools.partial(_const_spec, n_grid=2)
    kern = functools.partial(_ffn_kernel, rows=ts, sample=False, convert_next=convert_next)
    slab = D_MODEL // n_steps
    nxt_specs, nxt_args, nxt_out_specs, nxt_out_shapes = [], (), [], []
    if convert_next:
        nxt_specs = [pl.BlockSpec((None, slab, N_IN), lambda b, j: (l + 1, b * n_tiles + j, 0)),
                     pl.BlockSpec((None, slab, D_MODEL), lambda b, j: (l + 1, b * n_tiles + j, 0))]
        nxt_args = (p["w_in_f32"], p["w_o_f32"])
        nxt_out_specs = [pl.BlockSpec((slab, N_IN_PAD), lambda b, j: (b * n_tiles + j, 0)),
                         pl.BlockSpec((slab, D_MODEL), lambda b, j: (b * n_tiles + j, 0))]
        nxt_out_shapes = [jax.ShapeDtypeStruct((D_MODEL, N_IN_PAD), BF16),
                          jax.ShapeDtypeStruct((D_MODEL, D_MODEL), BF16)]
    return pl.pallas_call(
        kern,
        grid=(BATCH, n_tiles),
        in_specs=[
            pl.BlockSpec((ts, D_MODEL), lambda b, j: (b * n_tiles + j, 0)),
            pl.BlockSpec((None, None, 6, D_MODEL), lambda b, j: (l, DEC_BATCH + b, 0, 0)),
            *nxt_specs,
            cs((D_MODEL, 2 * D_FF)),
            ls((3, 2 * D_FF)),
            ls((1, 2 * D_FF)),
            cs((D_FF, D_MODEL)),
            ls((1, D_MODEL)),
            ls((1, D_MODEL)),
        ],
        out_specs=[
            pl.BlockSpec((ts, D_MODEL), lambda b, j: (b * n_tiles + j, 0)),
            pl.BlockSpec((None, 2, 2 * D_FF), lambda b, j: (b, 0, 0)),
            *nxt_out_specs,
        ],
        out_shape=[
            jax.ShapeDtypeStruct((BATCH * SEQ, D_MODEL), F32),
            jax.ShapeDtypeStruct((BATCH, 2, 2 * D_FF), F32),
            *nxt_out_shapes,
        ],
        scratch_shapes=_ffn_scratch(ts, SUBLANES),
        compiler_params=pltpu.CompilerParams(
            dimension_semantics=("arbitrary", "arbitrary"), vmem_limit_bytes=VMEM_LIMIT),
        name="ffn_prompt",
    )(h, mod, *nxt_args, w_up_b, p["ffn_conv_w"], p["ffn_conv_b"], w_dn_b, p["ln2_g"], p["ln2_b"])


def _ffn_sample_call(l, h3, mod, past, p, w_up_b, w_dn_b):
    tb = TB_S
    rr = tb * DEC_SEQ
    ls = functools.partial(_layer_spec, l=l, n_grid=1)
    cs = functools.partial(_const_spec, n_grid=1)
    kern = functools.partial(_ffn_kernel, rows=rr, sample=True, convert_next=False)
    return pl.pallas_call(
        kern,
        grid=(DEC_BATCH // tb,),
        in_specs=[
            pl.BlockSpec((tb, DEC_SEQ, D_MODEL), lambda i: (i, 0, 0)),
            pl.BlockSpec((None, tb, 6, D_MODEL), lambda i: (l, i, 0, 0)),
            pl.BlockSpec((None, tb, 2, 2 * D_FF), lambda i: (l, i, 0, 0)),
            cs((D_MODEL, 2 * D_FF)),
            ls((3, 2 * D_FF)),
            ls((1, 2 * D_FF)),
            cs((D_FF, D_MODEL)),
            ls((1, D_MODEL)),
            ls((1, D_MODEL)),
        ],
        out_specs=[
            pl.BlockSpec((tb, DEC_SEQ, D_MODEL), lambda i: (i, 0, 0)),
            pl.BlockSpec((tb, 2, 2 * D_FF), lambda i: (i, 0, 0)),
        ],
        out_shape=[
            jax.ShapeDtypeStruct((DEC_BATCH, DEC_SEQ, D_MODEL), F32),
            jax.ShapeDtypeStruct((DEC_BATCH, 2, 2 * D_FF), F32),
        ],
        scratch_shapes=_ffn_scratch(rr, 0),
        compiler_params=pltpu.CompilerParams(
            dimension_semantics=("arbitrary",), vmem_limit_bytes=VMEM_LIMIT),
        name="ffn_sample",
    )(h3, mod, past, w_up_b, p["ffn_conv_w"], p["ffn_conv_b"], w_dn_b, p["ln2_g"], p["ln2_b"])


def _prep_params(w_in, conv_a_w, sgu_ln_g, sgu_ln_b, sgu_w, sgu_b, gla_w_a2, gla_b_a, gla_norm_g,
                 w_o, ln1_g, ln1_b, ffn_w_up, ffn_conv_w, ffn_conv_b, ffn_w_down, ln2_g, ln2_b):
    tril = jnp.tril(jnp.ones((CHUNK, CHUNK), bool))
    w_p = jnp.where(tril[None, None], sgu_w, 0.0)
    wcat_p = jnp.transpose(w_p, (0, 2, 1, 3)).reshape(DEPTH, CHUNK, H_B * CHUNK)
    tril8 = jnp.tril(jnp.ones((DEC_SEQ, DEC_SEQ), bool))
    w8 = jnp.where(tril8[None, None], sgu_w[:, :, :DEC_SEQ, :DEC_SEQ], 0.0)
    eye = jnp.eye(GROUP_SEQ, dtype=F32)
    w_blk = jnp.einsum("ij,lhts->lhitjs", eye, w8).reshape(DEPTH, H_B, CHUNK, CHUNK)
    wcat_s = jnp.transpose(w_blk, (0, 2, 1, 3)).reshape(DEPTH, CHUNK, H_B * CHUNK)
    sbias_p = jnp.repeat(jnp.transpose(sgu_b, (0, 2, 1)), DH_B, axis=2)
    sbias_s = jnp.tile(sbias_p[:, :DEC_SEQ, :], (1, GROUP_SEQ, 1))
    return {
        "w_in_f32": w_in, "w_o_f32": w_o, "ffn_w_up": ffn_w_up, "ffn_w_down": ffn_w_down,
        "conv_a_w": conv_a_w,
        "sgu_ln_g": sgu_ln_g.reshape(DEPTH, 1, D_B),
        "sgu_ln_b": sgu_ln_b.reshape(DEPTH, 1, D_B),
        "wcat_p": wcat_p, "wcat_s": wcat_s, "sbias_p": sbias_p, "sbias_s": sbias_s,
        "w_a2": jnp.pad(gla_w_a2, ((0, 0), (0, RANK_PAD - GATE_RANK), (0, 0))).astype(BF16),
        "b_a": gla_b_a.reshape(DEPTH, 1, H_C * DK_C),
        "gla_norm_g": gla_norm_g.reshape(DEPTH, 1, DV_C),
        "ln1_g": ln1_g.reshape(DEPTH, 1, D_MODEL), "ln1_b": ln1_b.reshape(DEPTH, 1, D_MODEL),
        "ffn_conv_w": ffn_conv_w,
        "ffn_conv_b": ffn_conv_b.reshape(DEPTH, 1, 2 * D_FF),
        "ln2_g": ln2_g.reshape(DEPTH, 1, D_MODEL), "ln2_b": ln2_b.reshape(DEPTH, 1, D_MODEL),
    }


def kernel(x_prompt, x_sample, c_prompt, c_sample, state_conv_a, state_gla, state_ffn_conv,
           ln_in_g, ln_in_b, w_ada, b_ada, w_in, conv_a_w, sgu_ln_g, sgu_ln_b, sgu_w, sgu_b,
           gla_w_a2, gla_b_a, gla_norm_g, w_o, ln1_g, ln1_b, ffn_w_up, ffn_conv_w, ffn_conv_b,
           ffn_w_down, ln2_g, ln2_b):
    p = _prep_params(w_in, conv_a_w, sgu_ln_g, sgu_ln_b, sgu_w, sgu_b, gla_w_a2, gla_b_a,
                     gla_norm_g, w_o, ln1_g, ln1_b, ffn_w_up, ffn_conv_w, ffn_conv_b,
                     ffn_w_down, ln2_g, ln2_b)
    mod = _ada_call(jnp.concatenate([c_sample, c_prompt], axis=0), w_ada, b_ada)
    mod_p = mod_s = mod
    s0_all = state_gla.reshape(DEPTH, DEC_BATCH, H_C * DK_C, DV_C)

    p["ln_in_g"] = ln_in_g.reshape(1, D_MODEL)
    p["ln_in_b"] = ln_in_b.reshape(1, D_MODEL)
    hp = x_prompt.reshape(BATCH * SEQ, D_MODEL)
    hs = x_sample

    outs_p = [[], [], [], []]
    outs_s = [[], [], [], []]
    w_in_b, w_o_b = _convert_layer0_call(w_in, w_o)
    for l in range(DEPTH):
        hp, ca, cv, gs, w_up_b, w_dn_b = _mixer_prompt_call(l, hp, mod_p, p, w_in_b, w_o_b)
        hs, ca_s_l, cv_s_l, gs_s_l = _mixer_sample_call(l, hs, mod_s, state_conv_a, s0_all, p,
                                                        w_in_b, w_o_b)
        ffn_out = _ffn_prompt_call(l, hp, mod_p, p, w_up_b, w_dn_b)
        hp, fs = ffn_out[0], ffn_out[1]
        if l + 1 < DEPTH:
            w_in_b, w_o_b = ffn_out[2], ffn_out[3]
        for acc, val in zip(outs_p, (ca, cv, gs, fs)):
            acc.append(val)
        hs, fs = _ffn_sample_call(l, hs, mod_s, state_ffn_conv, p, w_up_b, w_dn_b)
        for acc, val in zip(outs_s, (ca_s_l, cv_s_l, gs_s_l, fs)):
            acc.append(val)

    ca_p, cv_p, gla_p, ffn_p = [jnp.stack(v) for v in outs_p]
    ca_s, cv_s, gla_s, ffn_s = [jnp.stack(v) for v in outs_s]
    return (hp.reshape(BATCH, SEQ, D_MODEL), hs,
            ca_p, ca_s, cv_p, cv_s,
            gla_p.reshape(DEPTH, BATCH, H_C, DK_C, DV_C),
            gla_s.reshape(DEPTH, DEC_BATCH, H_C, DK_C, DV_C),
            ffn_p, ffn_s)
```

```python
import functools
import math

import jax
import jax.numpy as jnp
from jax import lax
from jax.experimental import pallas as pl
from jax.experimental.pallas import tpu as pltpu

D_MODEL = 1024
BATCH = 8
SEQ = 2048
DEPTH = 4
DEC_BATCH = 128
DEC_SEQ = 8
D_A = 256
D_B = 256
H_B = 4
DH_B = 64
D_C = 512
H_C = 4
DV_C = 128
DK_C = 64
GATE_RANK = 16
GATE_TAU = 16.0
CHUNK = 128
D_FF = 2816
LN_EPS = 1e-5
ALPHA = (2.0 * DEPTH) ** 0.25
N_IN = 2832
N_IN_PAD = 2944
RANK_PAD = 128

O_AIN, O_AB, O_AC, O_SU, O_SV, O_Q, O_K, O_V, O_G, O_R = (
    0, 256, 512, 768, 1024, 1280, 1536, 1792, 2304, 2816)

SUBLANES = 8
VMEM_LIMIT = 56 * 1024 * 1024

TS_MIX = 1024
MIX_RB = 256
MIX_CB = 256
NORM_RB = 32
TS_FFN = 256
TB_S = 32
GROUP_SEQ = CHUNK // DEC_SEQ
SEQ_UNROLL = 4

F32 = jnp.float32
BF16 = jnp.bfloat16


def _dot(a, b):
    return jnp.dot(a, b, preferred_element_type=F32)


def _dot_nt(a, b):
    return lax.dot_general(a, b, (((1,), (1,)), ((), ())), preferred_element_type=F32)


def _dot_tn(a, b):
    return lax.dot_general(a, b, (((0,), (0,)), ((), ())), preferred_element_type=F32)


def _layer_norm(x, g, b):
    mu = jnp.mean(x, axis=-1, keepdims=True)
    xc = x - mu
    var = jnp.mean(xc * xc, axis=-1, keepdims=True)
    return xc * lax.rsqrt(var + LN_EPS) * g + b


def _gelu(x):
    c = math.sqrt(2.0 / math.pi)
    return 0.5 * x * (1.0 + jnp.tanh(c * (x + 0.044715 * (x * x * x))))


def _silu(x):
    return x * jax.nn.sigmoid(x)


def _log_sigmoid(x):
    return jnp.minimum(x, 0.0) - jnp.log1p(jnp.exp(-jnp.abs(x)))


def _split3(x):
    hi = x.astype(BF16)
    r1 = x - hi.astype(F32)
    mid = r1.astype(BF16)
    lo = (r1 - mid.astype(F32)).astype(BF16)
    return hi, mid, lo


def _dot_exact_lhs(m_bf16, x):
    hi, mid, lo = _split3(x)
    return _dot(m_bf16, hi) + _dot(m_bf16, mid) + _dot(m_bf16, lo)


def _head_masks(width, n_heads):
    lane = lax.broadcasted_iota(jnp.int32, (1, width), 1)
    per = width // n_heads
    return [(lane >= h * per) & (lane < (h + 1) * per) for h in range(n_heads)]


def _interleave(valu_pieces, mxu_pieces):
    n_v, n_m = len(valu_pieces), len(mxu_pieces)
    done = 0
    for k, mp in enumerate(mxu_pieces):
        upto = (k + 1) * n_v // (n_m + 1)
        for vp in valu_pieces[done:upto]:
            vp()
        done = upto
        mp()
    for vp in valu_pieces[done:]:
        vp()


def _ada_kernel(c_ref, w_ref, b_ref, o_ref):
    cs = _silu(c_ref[...]).astype(BF16)
    m = _dot(cs, w_ref[...].astype(BF16)) + b_ref[...]
    for k in range(6):
        @pl.when(pl.program_id(1) == k)
        def _():
            o_ref[:, k, :] = m


def _ada_call(c_all, w_ada, b_ada):
    n_rows = c_all.shape[0]
    return pl.pallas_call(
        _ada_kernel,
        grid=(DEPTH, 6),
        in_specs=[
            pl.BlockSpec((n_rows, D_MODEL), lambda l, j: (0, 0)),
            pl.BlockSpec((None, D_MODEL, D_MODEL), lambda l, j: (l, 0, j)),
            pl.BlockSpec((None, 1, D_MODEL), lambda l, j: (l, 0, j)),
        ],
        out_specs=pl.BlockSpec((None, n_rows, 6, D_MODEL), lambda l, j: (l, 0, 0, 0)),
        out_shape=jax.ShapeDtypeStruct((DEPTH, n_rows, 6, D_MODEL), F32),
        compiler_params=pltpu.CompilerParams(
            dimension_semantics=("arbitrary", "arbitrary"), vmem_limit_bytes=VMEM_LIMIT),
        name="ada_mod",
    )(c_all, w_ada, b_ada.reshape(DEPTH, 1, 6 * D_MODEL))


LANES = 128
N_IN_SLABS = N_IN // LANES


def _convert_in_proj_slab(step, wt_ref, tail_ref, winb_ref):
    @pl.when(step < N_IN_SLABS)
    def _():
        winb_ref[...] = wt_ref[...].T.astype(BF16)

    @pl.when(step == N_IN_SLABS)
    def _():
        winb_ref[...] = tail_ref[...].T.astype(BF16)


def _convert_kernel(wt_ref, tail_ref, wof_ref, winb_ref, wob_ref):
    step = pl.program_id(0)
    _convert_in_proj_slab(step, wt_ref, tail_ref, winb_ref)

    @pl.when(step == 0)
    def _():
        wob_ref[...] = wof_ref[...].astype(BF16)


def _convert_layer0_call(w_in_t, w_in_tail, w_o):
    return pl.pallas_call(
        _convert_kernel,
        grid=(N_IN_SLABS + 1,),
        in_specs=[
            pl.BlockSpec((None, LANES, D_MODEL), lambda i: (0, jnp.minimum(i, N_IN_SLABS - 1), 0)),
            pl.BlockSpec((None, LANES, D_MODEL), lambda i: (0, 0, 0)),
            pl.BlockSpec((None, D_MODEL, D_MODEL), lambda i: (0, 0, 0)),
        ],
        out_specs=[
            pl.BlockSpec((D_MODEL, LANES), lambda i: (0, i)),
            pl.BlockSpec((D_MODEL, D_MODEL), lambda i: (0, 0)),
        ],
        out_shape=[
            jax.ShapeDtypeStruct((D_MODEL, N_IN_PAD), BF16),
            jax.ShapeDtypeStruct((D_MODEL, D_MODEL), BF16),
        ],
        compiler_params=pltpu.CompilerParams(dimension_semantics=("arbitrary",)),
        name="convert_layer0",
    )(w_in_t, w_in_tail, w_o)


def _sgu_chunk(z_ref, zrows, y_ref, rows, wcat, sbias, ln_g, ln_b, masks_b):
    su = _gelu(z_ref[zrows, O_SU:O_SU + D_B])
    sv = _layer_norm(_gelu(z_ref[zrows, O_SV:O_SV + D_B]), ln_g, ln_b)
    svb = sv.astype(BF16)
    zero = jnp.zeros_like(svb)
    svm = jnp.concatenate([jnp.where(m, svb, zero) for m in masks_b], axis=0)
    mixed = _dot(wcat, svm) + sbias
    y_ref[rows, D_A:D_A + D_B] = (su * mixed).astype(BF16)
    return sv


def _gla_gate(z_ref, rows, wa2, ba):
    cr = z_ref[rows, O_R:O_R + RANK_PAD].astype(BF16)
    return _log_sigmoid(_dot(cr, wa2) + ba) * (1.0 / GATE_TAU)


def _gla_head_out(z_ref, zrows, y_ref, rows, hh, o, gn):
    ms = jnp.mean(o * o, axis=-1, keepdims=True)
    on = o * lax.rsqrt(ms + LN_EPS) * gn
    g = z_ref[zrows, O_G + hh * DV_C:O_G + (hh + 1) * DV_C]
    y_ref[rows, D_A + D_B + hh * DV_C:D_A + D_B + (hh + 1) * DV_C] = (_silu(g) * on).astype(BF16)


def _const_spec(shape, n_grid):
    zeros = (0,) * len(shape)
    if n_grid == 1:
        return pl.BlockSpec(shape, lambda i: zeros, pipeline_mode=pl.Buffered(1))
    return pl.BlockSpec(shape, lambda b, j: zeros, pipeline_mode=pl.Buffered(1))


def _layer_spec(shape, l, n_grid):
    idx = (l,) + (0,) * len(shape)
    if n_grid == 1:
        return pl.BlockSpec((None,) + shape, lambda i: idx, pipeline_mode=pl.Buffered(1))
    return pl.BlockSpec((None,) + shape, lambda b, j: idx, pipeline_mode=pl.Buffered(1))


def _mixer_prompt_kernel(*refs, ts, n_tiles, first):
    if first:
        (x_ref, mod_ref, wupf_ref, wdnf_ref, lnig_ref, lnib_ref, win_ref, cw_ref, sg_ref, sb_ref,
         wcat_ref, sbias_ref, wa2_ref, ba_ref, gn_ref, wo_ref, lng_ref, lnb_ref,
         hout_ref, ca_ref, cv_ref, gla_ref, wupb_ref, wdnb_ref,
         u_ref, z_ref, xa_ref, y_ref, st_ref, yo_ref, h_ref) = refs
        for r in range(0, ts, NORM_RB):
            h_ref[r:r + NORM_RB, :] = _layer_norm(x_ref[r:r + NORM_RB, :], lnig_ref[...], lnib_ref[...])
    else:
        (h_ref, mod_ref, wupf_ref, wdnf_ref, win_ref, cw_ref, sg_ref, sb_ref,
         wcat_ref, sbias_ref, wa2_ref, ba_ref, gn_ref, wo_ref, lng_ref, lnb_ref,
         hout_ref, ca_ref, cv_ref, gla_ref, wupb_ref, wdnb_ref,
         u_ref, z_ref, xa_ref, y_ref, st_ref, yo_ref) = refs
    wupb_ref[...] = wupf_ref[...].astype(BF16)
    wdnb_ref[...] = wdnf_ref[...].astype(BF16)
    j = pl.program_id(1)

    @pl.when(j == 0)
    def _():
        xa_ref[0:SUBLANES, :] = jnp.zeros((SUBLANES, D_A), F32)
        st_ref[...] = jnp.zeros_like(st_ref)

    sh1, sc1, g1 = mod_ref[0:1, :], mod_ref[1:2, :], mod_ref[2:3, :]
    for r in range(0, ts, NORM_RB):
        u_ref[r:r + NORM_RB, :] = (h_ref[r:r + NORM_RB, :] * (1.0 + sc1) + sh1).astype(BF16)

    def z_rows(r0, n):
        p, local = divmod(r0, MIX_RB)
        base = (p % 2) * MIX_RB + local
        return slice(base, base + n)

    def in_proj_pieces(p):
        rows = slice(p * MIX_RB, (p + 1) * MIX_RB)
        zrows = z_rows(p * MIX_RB, MIX_RB)

        def piece(c0):
            cols = slice(c0, min(c0 + MIX_CB, N_IN_PAD))
            z_ref[zrows, cols] = _dot(u_ref[rows, :], win_ref[:, cols])
        return [functools.partial(piece, c0) for c0 in range(0, N_IN_PAD, MIX_CB)]

    def out_proj_pieces(p):
        rows = slice(p * MIX_RB, (p + 1) * MIX_RB)

        def piece(c0):
            cols = slice(c0, c0 + MIX_CB)
            yo_ref[rows, cols] = _dot(y_ref[rows, :], wo_ref[:, cols])
        return [functools.partial(piece, c0) for c0 in range(0, D_MODEL, MIX_CB)]

    def norm_pieces(p):
        def piece(r):
            rows = slice(r, r + NORM_RB)
            hout_ref[rows, :] = _layer_norm(ALPHA * h_ref[rows, :] + g1 * yo_ref[rows, :],
                                            lng_ref[...], lnb_ref[...])
        return [functools.partial(piece, r)
                for r in range(p * MIX_RB, (p + 1) * MIX_RB, NORM_RB)]

    cw = cw_ref[...]
    wcat = wcat_ref[...].astype(BF16)
    sbias = sbias_ref[...]
    wa2 = wa2_ref[...]
    ba = ba_ref[...]
    gn = gn_ref[...]
    sg, sb = sg_ref[...], sb_ref[...]
    masks_b = _head_masks(D_B, H_B)
    masks_k = _head_masks(H_C * DK_C, H_C)
    row = lax.broadcasted_iota(jnp.int32, (CHUNK, CHUNK), 0)
    col = lax.broadcasted_iota(jnp.int32, (CHUNK, CHUNK), 1)
    causal = col <= row
    tri = causal.astype(BF16)

    def mix_pieces(c):
        r0 = c * CHUNK
        rows = slice(r0, r0 + CHUNK)
        zrows = z_rows(r0, CHUNK)
        v = {}

        def conv_a():
            x = z_ref[zrows, O_AC:O_AC + D_A] * z_ref[zrows, O_AIN:O_AIN + D_A]
            xa_ref[r0 + SUBLANES:r0 + SUBLANES + CHUNK, :] = x
            win = xa_ref[r0:r0 + CHUNK + SUBLANES, :]
            conv = (cw[0:1, :] * win[SUBLANES - 2:SUBLANES - 2 + CHUNK]
                    + cw[1:2, :] * win[SUBLANES - 1:SUBLANES - 1 + CHUNK]
                    + cw[2:3, :] * x)
            y_ref[rows, 0:D_A] = (z_ref[zrows, O_AB:O_AB + D_A] * conv).astype(BF16)

        def sgu():
            cv_ref[...] = _sgu_chunk(z_ref, zrows, y_ref, rows, wcat, sbias, sg, sb, masks_b)

        def gla_prep():
            lg = _gla_gate(z_ref, zrows, wa2, ba)
            b = _dot_exact_lhs(tri, lg)
            b_last = b[CHUNK - 1:CHUNK, :]
            b_mid = b[CHUNK // 2 - 1:CHUNK // 2, :]
            q = z_ref[zrows, O_Q:O_Q + H_C * DK_C]
            k = z_ref[zrows, O_K:O_K + H_C * DK_C]
            v["qt"] = (q * (DK_C ** -0.5)) * jnp.exp(b - b_mid)
            v["kt"] = (k * jnp.exp(b_mid - b)).astype(BF16)
            v["kh"] = k * jnp.exp(b_last - b)
            st = st_ref[...]
            v["stb"] = (st * jnp.exp(b_mid)).astype(BF16)
            v["st_new"] = st * jnp.exp(b_last)

        def gla_head(hh):
            zq = jnp.zeros_like(v["qt"])
            qm = jnp.where(masks_k[hh], v["qt"], zq).astype(BF16)
            a = jnp.where(causal, _dot_nt(qm, v["kt"]), 0.0).astype(BF16)
            vb = z_ref[zrows, O_V + hh * DV_C:O_V + (hh + 1) * DV_C].astype(BF16)
            o = _dot(a, vb) + _dot_nt(qm, v["stb"])
            _gla_head_out(z_ref, zrows, y_ref, rows, hh, o, gn)
            khm = jnp.where(masks_k[hh], v["kh"], zq).astype(BF16)
            v["st_new"] = v["st_new"] + _dot_tn(vb, khm)

        def gla_state():
            st_ref[...] = v["st_new"]

        return ([conv_a, sgu, gla_prep] + [functools.partial(gla_head, hh) for hh in range(H_C)]
                + [gla_state])

    n_rb = ts // MIX_RB
    per = MIX_RB // CHUNK
    _interleave([], in_proj_pieces(0))
    for p in range(n_rb):
        valu = [pc for c in range(p * per, (p + 1) * per) for pc in mix_pieces(c)]
        if p >= 2:
            valu = valu + norm_pieces(p - 2)
        mxu = in_proj_pieces(p + 1) if p + 1 < n_rb else []
        if p >= 1:
            mxu = mxu + out_proj_pieces(p - 1)
        _interleave(valu, mxu)
    _interleave(norm_pieces(n_rb - 2) if n_rb >= 2 else [], out_proj_pieces(n_rb - 1))
    _interleave(norm_pieces(n_rb - 1), [])

    ca_ref[...] = xa_ref[ts + SUBLANES - 2:ts + SUBLANES, :]
    xa_ref[0:SUBLANES, :] = xa_ref[ts:ts + SUBLANES, :]

    @pl.when(j == n_tiles - 1)
    def _():
        gla_ref[...] = st_ref[...].T


def _mixer_prompt_call(l, h, mod, p, w_in_b, w_o_b):
    ts = TS_MIX
    n_tiles = SEQ // ts
    n_steps = BATCH * n_tiles
    first = l == 0
    ls = functools.partial(_layer_spec, l=l, n_grid=2)
    cs = functools.partial(_const_spec, n_grid=2)
    kern = functools.partial(_mixer_prompt_kernel, ts=ts, n_tiles=n_tiles, first=first)
    ln_in_specs = [pl.BlockSpec((1, D_MODEL), lambda b, j: (0, 0))] * 2 if first else []
    ln_in_args = (p["ln_in_g"], p["ln_in_b"]) if first else ()
    ln_in_scratch = [pltpu.VMEM((ts, D_MODEL), F32)] if first else []
    up_rows = D_MODEL // n_steps
    dn_rows = D_FF // n_steps
    return pl.pallas_call(
        kern,
        grid=(BATCH, n_tiles),
        in_specs=[
            pl.BlockSpec((ts, D_MODEL), lambda b, j: (b * n_tiles + j, 0)),
            pl.BlockSpec((None, None, 6, D_MODEL), lambda b, j: (l, DEC_BATCH + b, 0, 0)),
            pl.BlockSpec((None, up_rows, 2 * D_FF), lambda b, j: (l, b * n_tiles + j, 0)),
            pl.BlockSpec((None, dn_rows, D_MODEL), lambda b, j: (l, b * n_tiles + j, 0)),
            *ln_in_specs,
            cs((D_MODEL, N_IN_PAD)),
            ls((3, D_A)),
            ls((1, D_B)),
            ls((1, D_B)),
            ls((CHUNK, H_B * CHUNK)),
            ls((CHUNK, D_B)),
            ls((RANK_PAD, H_C * DK_C)),
            ls((1, H_C * DK_C)),
            ls((1, DV_C)),
            cs((D_MODEL, D_MODEL)),
            ls((1, D_MODEL)),
            ls((1, D_MODEL)),
        ],
        out_specs=[
            pl.BlockSpec((ts, D_MODEL), lambda b, j: (b * n_tiles + j, 0)),
            pl.BlockSpec((None, 2, D_A), lambda b, j: (b, 0, 0)),
            pl.BlockSpec((None, CHUNK, D_B), lambda b, j: (b, 0, 0)),
            pl.BlockSpec((None, H_C * DK_C, DV_C), lambda b, j: (b, 0, 0)),
            pl.BlockSpec((up_rows, 2 * D_FF), lambda b, j: (b * n_tiles + j, 0)),
            pl.BlockSpec((dn_rows, D_MODEL), lambda b, j: (b * n_tiles + j, 0)),
        ],
        out_shape=[
            jax.ShapeDtypeStruct((BATCH * SEQ, D_MODEL), F32),
            jax.ShapeDtypeStruct((BATCH, 2, D_A), F32),
            jax.ShapeDtypeStruct((BATCH, CHUNK, D_B), F32),
            jax.ShapeDtypeStruct((BATCH, H_C * DK_C, DV_C), F32),
            jax.ShapeDtypeStruct((D_MODEL, 2 * D_FF), BF16),
            jax.ShapeDtypeStruct((D_FF, D_MODEL), BF16),
        ],
        scratch_shapes=[
            pltpu.VMEM((ts, D_MODEL), BF16),
            pltpu.VMEM((2 * MIX_RB, N_IN_PAD), F32),
            pltpu.VMEM((ts + SUBLANES, D_A), F32),
            pltpu.VMEM((ts, D_MODEL), BF16),
            pltpu.VMEM((DV_C, H_C * DK_C), F32),
            pltpu.VMEM((ts, D_MODEL), F32),
            *ln_in_scratch,
        ],
        compiler_params=pltpu.CompilerParams(
            dimension_semantics=("arbitrary", "arbitrary"), vmem_limit_bytes=VMEM_LIMIT),
        name="mixer_prompt",
    )(h, mod, p["ffn_w_up"], p["ffn_w_down"], *ln_in_args, w_in_b, p["conv_a_w"], p["sgu_ln_g"],
      p["sgu_ln_b"], p["wcat_p"], p["sbias_p"], p["w_a2"], p["b_a"], p["gla_norm_g"], w_o_b,
      p["ln1_g"], p["ln1_b"])


def _expand_rows(x3):
    n, _, c = x3.shape
    return jnp.broadcast_to(x3, (n, DEC_SEQ, c)).reshape(n * DEC_SEQ, c)


def _sample_conv(x, past, cw):
    p0 = _expand_rows(past[:, 0:1, :])
    p1 = _expand_rows(past[:, 1:2, :])
    t_idx = lax.broadcasted_iota(jnp.int32, x.shape, 0) & (DEC_SEQ - 1)
    xm1 = jnp.where(t_idx >= 1, pltpu.roll(x, 1, 0), p1)
    xm2 = jnp.where(t_idx >= 2, pltpu.roll(x, 2, 0), jnp.where(t_idx == 1, p1, p0))
    return cw[0:1, :] * xm2 + cw[1:2, :] * xm1 + cw[2:3, :] * x


def _mixer_sample_kernel(*refs, tb, first):
    if first:
        (h_ref, mod_ref, lnig_ref, lnib_ref, past_ref, s0_ref, win_ref, cw_ref, sg_ref, sb_ref,
         wcat_ref, sbias_ref, wa2_ref, ba_ref, gn_ref, wo_ref, lng_ref, lnb_ref,
         hout_ref, ca_ref, cv_ref, gla_ref,
         z_ref, y_ref, qt_ref, kh_ref, eb_ref, oi_ref, os_ref) = refs
    else:
        (h_ref, mod_ref, past_ref, s0_ref, win_ref, cw_ref, sg_ref, sb_ref,
         wcat_ref, sbias_ref, wa2_ref, ba_ref, gn_ref, wo_ref, lng_ref, lnb_ref,
         hout_ref, ca_ref, cv_ref, gla_ref,
         z_ref, y_ref, qt_ref, kh_ref, eb_ref, oi_ref, os_ref) = refs
    rr = tb * DEC_SEQ
    h = h_ref[...].reshape(rr, D_MODEL)
    if first:
        h = _layer_norm(h, lnig_ref[...], lnib_ref[...])
    sh1 = _expand_rows(mod_ref[:, 0:1, :])
    sc1 = _expand_rows(mod_ref[:, 1:2, :])
    g1 = _expand_rows(mod_ref[:, 2:3, :])
    u = (h * (1.0 + sc1) + sh1).astype(BF16)
    z_ref[...] = _dot(u, win_ref[...])

    x = z_ref[:, O_AC:O_AC + D_A] * z_ref[:, O_AIN:O_AIN + D_A]
    conv = _sample_conv(x, past_ref[...], cw_ref[...])
    y_ref[:, 0:D_A] = (z_ref[:, O_AB:O_AB + D_A] * conv).astype(BF16)
    ca_ref[...] = x.reshape(tb, DEC_SEQ, D_A)[:, DEC_SEQ - 2:DEC_SEQ, :]

    wcat = wcat_ref[...].astype(BF16)
    sbias = sbias_ref[...]
    wa2 = wa2_ref[...]
    ba = ba_ref[...]
    gn = gn_ref[...]
    sg, sb = sg_ref[...], sb_ref[...]
    masks_b = _head_masks(D_B, H_B)
    masks_k = _head_masks(H_C * DK_C, H_C)
    row = lax.broadcasted_iota(jnp.int32, (CHUNK, CHUNK), 0)
    col = lax.broadcasted_iota(jnp.int32, (CHUNK, CHUNK), 1)
    same_seq = (row // DEC_SEQ) == (col // DEC_SEQ)
    causal = same_seq & (col <= row)
    tri = causal.astype(BF16)
    blk = same_seq.astype(BF16)

    for gi in range(rr // CHUNK):
        rows = pl.ds(gi * CHUNK, CHUNK)
        sv = _sgu_chunk(z_ref, rows, y_ref, rows, wcat, sbias, sg, sb, masks_b)
        cv_ref[gi * GROUP_SEQ:(gi + 1) * GROUP_SEQ] = sv.reshape(GROUP_SEQ, DEC_SEQ, D_B)
        lg = _gla_gate(z_ref, rows, wa2, ba)
        b = _dot_exact_lhs(tri, lg)
        b_tot = _dot_exact_lhs(blk, lg)
        q = z_ref[rows, O_Q:O_Q + H_C * DK_C]
        k = z_ref[rows, O_K:O_K + H_C * DK_C]
        qt = (q * (DK_C ** -0.5)) * jnp.exp(b)
        kt = (k * jnp.exp(-b)).astype(BF16)
        qt_ref[rows, :] = qt
        kh_ref[rows, :] = k * jnp.exp(b_tot - b)
        eb_ref[rows, :] = jnp.exp(b_tot)
        zq = jnp.zeros_like(qt)
        for hh in range(H_C):
            qm = jnp.where(masks_k[hh], qt, zq).astype(BF16)
            a = jnp.where(causal, _dot_nt(qm, kt), 0.0).astype(BF16)
            vb = z_ref[rows, O_V + hh * DV_C:O_V + (hh + 1) * DV_C].astype(BF16)
            oi_ref[rows, hh * DV_C:(hh + 1) * DV_C] = _dot(a, vb)

    ones16 = jnp.ones((2 * SUBLANES, DV_C), BF16)
    sub = lax.broadcasted_iota(jnp.int32, (SUBLANES, H_C * DK_C), 0)

    def seq_body(i, carry):
        r0 = pl.multiple_of(i * DEC_SEQ, DEC_SEQ)
        rows = pl.ds(r0, DEC_SEQ)
        s0 = s0_ref[i]
        qt = qt_ref[rows, :]
        kh = kh_ref[rows, :]
        zq = jnp.zeros_like(qt)
        q_exp = jnp.concatenate([jnp.where(m, qt, zq) for m in masks_k], axis=0).astype(BF16)
        k_exp = jnp.concatenate([jnp.where(m, kh, zq) for m in masks_k], axis=0).astype(BF16)
        o_int = _dot(q_exp, s0.astype(BF16))
        v = z_ref[rows, O_V:O_V + D_C]
        v_exp = jnp.concatenate([v[:, hh * DV_C:(hh + 1) * DV_C] for hh in range(H_C)],
                                axis=0).astype(BF16)
        eb = eb_ref[rows, :]
        hi, mid, lo = _split3(eb)
        e3 = jnp.where(sub == 0, hi.astype(F32),
                       jnp.where(sub == 1, mid.astype(F32),
                                 jnp.where(sub == 2, lo.astype(F32), jnp.zeros_like(eb))))
        e3 = jnp.concatenate([e3, jnp.zeros_like(eb)], axis=0).astype(BF16)
        decay = _dot_tn(e3, ones16)
        gla_ref[i] = decay * s0 + _dot_tn(k_exp, v_exp)
        for hh in range(H_C):
            os_ref[rows, hh * DV_C:(hh + 1) * DV_C] = o_int[hh * DEC_SEQ:(hh + 1) * DEC_SEQ, :]
        return carry

    lax.fori_loop(0, tb, seq_body, 0, unroll=SEQ_UNROLL)

    for gi in range(rr // CHUNK):
        rows = pl.ds(gi * CHUNK, CHUNK)
        for hh in range(H_C):
            cols = slice(hh * DV_C, (hh + 1) * DV_C)
            _gla_head_out(z_ref, rows, y_ref, rows, hh, oi_ref[rows, cols] + os_ref[rows, cols], gn)

    y = _dot(y_ref[...], wo_ref[...])
    hn = _layer_norm(ALPHA * h + g1 * y, lng_ref[...], lnb_ref[...])
    hout_ref[...] = hn.reshape(tb, DEC_SEQ, D_MODEL)


def _mixer_sample_call(l, h3, mod, past, s0, p, w_in_b, w_o_b):
    tb = TB_S
    ls = functools.partial(_layer_spec, l=l, n_grid=1)
    cs = functools.partial(_const_spec, n_grid=1)
    first = l == 0
    kern = functools.partial(_mixer_sample_kernel, tb=tb, first=first)
    ln_in_specs = [pl.BlockSpec((1, D_MODEL), lambda i: (0, 0))] * 2 if first else []
    ln_in_args = (p["ln_in_g"], p["ln_in_b"]) if first else ()
    rr = tb * DEC_SEQ
    return pl.pallas_call(
        kern,
        grid=(DEC_BATCH // tb,),
        in_specs=[
            pl.BlockSpec((tb, DEC_SEQ, D_MODEL), lambda i: (i, 0, 0)),
            pl.BlockSpec((None, tb, 6, D_MODEL), lambda i: (l, i, 0, 0)),
            *ln_in_specs,
            pl.BlockSpec((None, tb, 2, D_A), lambda i: (l, i, 0, 0)),
            pl.BlockSpec((None, tb, H_C * DK_C, DV_C), lambda i: (l, i, 0, 0)),
            cs((D_MODEL, N_IN_PAD)),
            ls((3, D_A)),
            ls((1, D_B)),
            ls((1, D_B)),
            ls((CHUNK, H_B * CHUNK)),
            ls((CHUNK, D_B)),
            ls((RANK_PAD, H_C * DK_C)),
            ls((1, H_C * DK_C)),
            ls((1, DV_C)),
            cs((D_MODEL, D_MODEL)),
            ls((1, D_MODEL)),
            ls((1, D_MODEL)),
        ],
        out_specs=[
            pl.BlockSpec((tb, DEC_SEQ, D_MODEL), lambda i: (i, 0, 0)),
            pl.BlockSpec((tb, 2, D_A), lambda i: (i, 0, 0)),
            pl.BlockSpec((tb, DEC_SEQ, D_B), lambda i: (i, 0, 0)),
            pl.BlockSpec((tb, H_C * DK_C, DV_C), lambda i: (i, 0, 0)),
        ],
        out_shape=[
            jax.ShapeDtypeStruct((DEC_BATCH, DEC_SEQ, D_MODEL), F32),
            jax.ShapeDtypeStruct((DEC_BATCH, 2, D_A), F32),
            jax.ShapeDtypeStruct((DEC_BATCH, DEC_SEQ, D_B), F32),
            jax.ShapeDtypeStruct((DEC_BATCH, H_C * DK_C, DV_C), F32),
        ],
        scratch_shapes=[
            pltpu.VMEM((rr, N_IN_PAD), F32),
            pltpu.VMEM((rr, D_MODEL), BF16),
            pltpu.VMEM((rr, H_C * DK_C), F32),
            pltpu.VMEM((rr, H_C * DK_C), F32),
            pltpu.VMEM((rr, H_C * DK_C), F32),
            pltpu.VMEM((rr, D_C), F32),
            pltpu.VMEM((rr, D_C), F32),
        ],
        compiler_params=pltpu.CompilerParams(
            dimension_semantics=("arbitrary",), vmem_limit_bytes=VMEM_LIMIT),
        name="mixer_sample",
    )(h3, mod, *ln_in_args, past, s0, w_in_b, p["conv_a_w"], p["sgu_ln_g"], p["sgu_ln_b"], p["wcat_s"],
      p["sbias_s"], p["w_a2"], p["b_a"], p["gla_norm_g"], w_o_b, p["ln1_g"], p["ln1_b"])


FFN_RB = 32
FFN_CB = 256
N_CB = D_FF // FFN_CB
FFN_AHEAD = 3
FFN_AHEAD_EXTRA = 4
FFN_SUB = 256


def _ffn_kernel(*refs, rows, sample, convert_next):
    if sample:
        (h_ref, mod_ref, past_ref, wup_ref, cw_ref, cb_ref, wdn_ref, lng_ref, lnb_ref,
         hout_ref, st_ref, u_ref, xp_ref, act_ref, acc_ref) = refs
        off = 0
    else:
        if convert_next:
            (h_ref, mod_ref, wt_ref, tail_ref, wof_ref, wup_ref, cw_ref, cb_ref, wdn_ref, lng_ref,
             lnb_ref, hout_ref, st_ref, winb_ref, wob_ref, u_ref, xp_ref, act_ref, acc_ref) = refs
            step = pl.program_id(0) * pl.num_programs(1) + pl.program_id(1)
            _convert_in_proj_slab(step, wt_ref, tail_ref, winb_ref)
            wob_ref[...] = wof_ref[...].astype(BF16)
        else:
            (h_ref, mod_ref, wup_ref, cw_ref, cb_ref, wdn_ref, lng_ref, lnb_ref,
             hout_ref, st_ref, u_ref, xp_ref, act_ref, acc_ref) = refs
        off = SUBLANES

        @pl.when(pl.program_id(1) == 0)
        def _():
            xp_ref[0:SUBLANES, :] = jnp.zeros((SUBLANES, 2 * D_FF), F32)

    def rows_of(r):
        if sample:
            seqs = slice(r // DEC_SEQ, (r + FFN_RB) // DEC_SEQ)
            return (h_ref[seqs].reshape(FFN_RB, D_MODEL),
                    [_expand_rows(mod_ref[seqs, k:k + 1, :]) for k in (3, 4, 5)])
        return h_ref[r:r + FFN_RB, :], [mod_ref[k:k + 1, :] for k in (3, 4, 5)]

    def modulate_pieces(s):
        def piece(r):
            hr, (sh2, sc2, _) = rows_of(r)
            u_ref[r:r + FFN_RB, :] = (hr * (1.0 + sc2) + sh2).astype(BF16)
        return [functools.partial(piece, r)
                for r in range(s * FFN_SUB, (s + 1) * FFN_SUB, FFN_RB)]

    n_sub = rows // FFN_SUB
    items = [(s, c) for s in range(n_sub) for c in range(N_CB)]

    def block_cols(c):
        return (slice(c * FFN_CB, (c + 1) * FFN_CB),
                slice(D_FF + c * FFN_CB, D_FF + (c + 1) * FFN_CB))

    def up_pieces(k):
        s, c = items[k]
        r0 = s * FFN_SUB

        def piece(cols):
            xp_ref[off + r0:off + r0 + FFN_SUB, cols] = _dot(u_ref[r0:r0 + FFN_SUB, :], wup_ref[:, cols])
        return [functools.partial(piece, cols) for cols in block_cols(c)]

    def down_pieces(k):
        s, c = items[k]
        r0 = s * FFN_SUB

        def piece(n):
            cols = slice(n * FFN_CB, (n + 1) * FFN_CB)
            d = _dot(act_ref[k % 2], wdn_ref[c * FFN_CB:(c + 1) * FFN_CB, cols])
            if c == 0:
                acc_ref[r0:r0 + FFN_SUB, cols] = d
            else:
                acc_ref[r0:r0 + FFN_SUB, cols] += d
        return [functools.partial(piece, n) for n in range(D_MODEL // FFN_CB)]

    def conv_rows(r, cols):
        if sample:
            seqs = slice(r // DEC_SEQ, (r + FFN_RB) // DEC_SEQ)
            x = xp_ref[r:r + FFN_RB, cols]
            st_ref[seqs, :, cols] = x.reshape(FFN_RB // DEC_SEQ, DEC_SEQ, FFN_CB)[:, DEC_SEQ - 2:, :]
            y = _sample_conv(x, past_ref[seqs, :, cols], cw_ref[:, cols])
        else:
            win = xp_ref[r:r + FFN_RB + SUBLANES, cols]
            y = (cw_ref[0:1, cols] * win[SUBLANES - 2:SUBLANES - 2 + FFN_RB]
                 + cw_ref[1:2, cols] * win[SUBLANES - 1:SUBLANES - 1 + FFN_RB]
                 + cw_ref[2:3, cols] * win[SUBLANES:SUBLANES + FFN_RB])
        return y + cb_ref[0:1, cols]

    def elementwise_pieces(k):
        s, c = items[k]
        ca, cg = block_cols(c)

        def piece(r):
            act_ref[k % 2, r - s * FFN_SUB:r - s * FFN_SUB + FFN_RB, :] = (
                _silu(conv_rows(r, cg)) * conv_rows(r, ca)).astype(BF16)
        return [functools.partial(piece, r)
                for r in range(s * FFN_SUB, (s + 1) * FFN_SUB, FFN_RB)]

    def finish_pieces(s):
        def piece(r):
            hr, (_, _, g2) = rows_of(r)
            hn = _layer_norm(ALPHA * hr + g2 * acc_ref[r:r + FFN_RB, :], lng_ref[...], lnb_ref[...])
            if sample:
                seqs = slice(r // DEC_SEQ, (r + FFN_RB) // DEC_SEQ)
                hout_ref[seqs] = hn.reshape(FFN_RB // DEC_SEQ, DEC_SEQ, D_MODEL)
            else:
                hout_ref[r:r + FFN_RB, :] = hn
        return [functools.partial(piece, r)
                for r in range(s * FFN_SUB, (s + 1) * FFN_SUB, FFN_RB)]

    ahead = FFN_AHEAD + FFN_AHEAD_EXTRA * (n_sub - 1)
    side = [[] for _ in items]
    for s in range(n_sub):
        if s + 1 < n_sub:
            early = modulate_pieces(s + 1)
            span = N_CB - ahead
            for i, pc in enumerate(early):
                side[s * N_CB + i * span // len(early)].append(pc)
            late = finish_pieces(s)
            for i, pc in enumerate(late):
                side[(s + 1) * N_CB + 1 + i * (N_CB - 1) // len(late)].append(pc)

    n_items = len(items)
    _interleave(modulate_pieces(0), [])
    for k in range(ahead):
        _interleave([], up_pieces(k))
    for k in range(n_items):
        mxu = up_pieces(k + ahead) if k + ahead < n_items else []
        if k >= 1:
            mxu = mxu + down_pieces(k - 1)
        _interleave(elementwise_pieces(k) + side[k], mxu)
    _interleave([], down_pieces(n_items - 1))
    _interleave(finish_pieces(n_sub - 1), [])

    if not sample:
        st_ref[...] = xp_ref[rows + SUBLANES - 2:rows + SUBLANES, :]
        xp_ref[0:SUBLANES, :] = xp_ref[rows:rows + SUBLANES, :]


def _ffn_scratch(rows, off):
    return [
        pltpu.VMEM((rows, D_MODEL), BF16),
        pltpu.VMEM((rows + off, 2 * D_FF), F32),
        pltpu.VMEM((2, FFN_SUB, FFN_CB), BF16),
        pltpu.VMEM((rows, D_MODEL), F32),
    ]


def _ffn_prompt_call(l, h, mod, p, w_up_b, w_dn_b):
    ts = TS_FFN
    n_tiles = SEQ // ts
    n_steps = BATCH * n_tiles
    convert_next = l + 1 < DEPTH
    ls = functools.partial(_layer_spec, l=l, n_grid=2)
    cs = functools.partial(_const_spec, n_grid=2)
    kern = functools.partial(_ffn_kernel, rows=ts, sample=False, convert_next=convert_next)
    slab = D_MODEL // n_steps
    nxt_specs, nxt_args, nxt_out_specs, nxt_out_shapes = [], (), [], []
    if convert_next:
        nxt_specs = [
            pl.BlockSpec((None, LANES, D_MODEL),
                         lambda b, j: (l + 1, jnp.minimum(b * n_tiles + j, N_IN_SLABS - 1), 0)),
            pl.BlockSpec((None, LANES, D_MODEL), lambda b, j: (l + 1, 0, 0)),
            pl.BlockSpec((None, slab, D_MODEL), lambda b, j: (l + 1, b * n_tiles + j, 0))]
        nxt_args = (p["w_in_t"], p["w_in_tail"], p["w_o_f32"])
        nxt_out_specs = [
            pl.BlockSpec((D_MODEL, LANES), lambda b, j: (0, jnp.minimum(b * n_tiles + j, N_IN_SLABS))),
            pl.BlockSpec((slab, D_MODEL), lambda b, j: (b * n_tiles + j, 0))]
        nxt_out_shapes = [jax.ShapeDtypeStruct((D_MODEL, N_IN_PAD), BF16),
                          jax.ShapeDtypeStruct((D_MODEL, D_MODEL), BF16)]
    return pl.pallas_call(
        kern,
        grid=(BATCH, n_tiles),
        in_specs=[
            pl.BlockSpec((ts, D_MODEL), lambda b, j: (b * n_tiles + j, 0)),
            pl.BlockSpec((None, None, 6, D_MODEL), lambda b, j: (l, DEC_BATCH + b, 0, 0)),
            *nxt_specs,
            cs((D_MODEL, 2 * D_FF)),
            ls((3, 2 * D_FF)),
            ls((1, 2 * D_FF)),
            cs((D_FF, D_MODEL)),
            ls((1, D_MODEL)),
            ls((1, D_MODEL)),
        ],
        out_specs=[
            pl.BlockSpec((ts, D_MODEL), lambda b, j: (b * n_tiles + j, 0)),
            pl.BlockSpec((None, 2, 2 * D_FF), lambda b, j: (b, 0, 0)),
            *nxt_out_specs,
        ],
        out_shape=[
            jax.ShapeDtypeStruct((BATCH * SEQ, D_MODEL), F32),
            jax.ShapeDtypeStruct((BATCH, 2, 2 * D_FF), F32),
            *nxt_out_shapes,
        ],
        scratch_shapes=_ffn_scratch(ts, SUBLANES),
        compiler_params=pltpu.CompilerParams(
            dimension_semantics=("arbitrary", "arbitrary"), vmem_limit_bytes=VMEM_LIMIT),
        name="ffn_prompt",
    )(h, mod, *nxt_args, w_up_b, p["ffn_conv_w"], p["ffn_conv_b"], w_dn_b, p["ln2_g"], p["ln2_b"])


def _ffn_sample_call(l, h3, mod, past, p, w_up_b, w_dn_b):
    tb = TB_S
    rr = tb * DEC_SEQ
    ls = functools.partial(_layer_spec, l=l, n_grid=1)
    cs = functools.partial(_const_spec, n_grid=1)
    kern = functools.partial(_ffn_kernel, rows=rr, sample=True, convert_next=False)
    return pl.pallas_call(
        kern,
        grid=(DEC_BATCH // tb,),
        in_specs=[
            pl.BlockSpec((tb, DEC_SEQ, D_MODEL), lambda i: (i, 0, 0)),
            pl.BlockSpec((None, tb, 6, D_MODEL), lambda i: (l, i, 0, 0)),
            pl.BlockSpec((None, tb, 2, 2 * D_FF), lambda i: (l, i, 0, 0)),
            cs((D_MODEL, 2 * D_FF)),
            ls((3, 2 * D_FF)),
            ls((1, 2 * D_FF)),
            cs((D_FF, D_MODEL)),
            ls((1, D_MODEL)),
            ls((1, D_MODEL)),
        ],
        out_specs=[
            pl.BlockSpec((tb, DEC_SEQ, D_MODEL), lambda i: (i, 0, 0)),
            pl.BlockSpec((tb, 2, 2 * D_FF), lambda i: (i, 0, 0)),
        ],
        out_shape=[
            jax.ShapeDtypeStruct((DEC_BATCH, DEC_SEQ, D_MODEL), F32),
            jax.ShapeDtypeStruct((DEC_BATCH, 2, 2 * D_FF), F32),
        ],
        scratch_shapes=_ffn_scratch(rr, 0),
        compiler_params=pltpu.CompilerParams(
            dimension_semantics=("arbitrary",), vmem_limit_bytes=VMEM_LIMIT),
        name="ffn_sample",
    )(h3, mod, past, w_up_b, p["ffn_conv_w"], p["ffn_conv_b"], w_dn_b, p["ln2_g"], p["ln2_b"])


def _prep_params(w_in, conv_a_w, sgu_ln_g, sgu_ln_b, sgu_w, sgu_b, gla_w_a2, gla_b_a, gla_norm_g,
                 w_o, ln1_g, ln1_b, ffn_w_up, ffn_conv_w, ffn_conv_b, ffn_w_down, ln2_g, ln2_b):
    tril = jnp.tril(jnp.ones((CHUNK, CHUNK), bool))
    w_p = jnp.where(tril[None, None], sgu_w, 0.0)
    wcat_p = jnp.transpose(w_p, (0, 2, 1, 3)).reshape(DEPTH, CHUNK, H_B * CHUNK)
    tril8 = jnp.tril(jnp.ones((DEC_SEQ, DEC_SEQ), bool))
    w8 = jnp.where(tril8[None, None], sgu_w[:, :, :DEC_SEQ, :DEC_SEQ], 0.0)
    eye = jnp.eye(GROUP_SEQ, dtype=F32)
    w_blk = jnp.einsum("ij,lhts->lhitjs", eye, w8).reshape(DEPTH, H_B, CHUNK, CHUNK)
    wcat_s = jnp.transpose(w_blk, (0, 2, 1, 3)).reshape(DEPTH, CHUNK, H_B * CHUNK)
    sbias_p = jnp.repeat(jnp.transpose(sgu_b, (0, 2, 1)), DH_B, axis=2)
    sbias_s = jnp.tile(sbias_p[:, :DEC_SEQ, :], (1, GROUP_SEQ, 1))
    w_in_t = jnp.swapaxes(w_in, 1, 2)
    return {
        "w_in_t": w_in_t,
        "w_in_tail": jnp.pad(w_in_t[:, N_IN_SLABS * LANES:, :],
                             ((0, 0), (0, N_IN_PAD - N_IN), (0, 0))),
        "w_o_f32": w_o, "ffn_w_up": ffn_w_up, "ffn_w_down": ffn_w_down,
        "conv_a_w": conv_a_w,
        "sgu_ln_g": sgu_ln_g.reshape(DEPTH, 1, D_B),
        "sgu_ln_b": sgu_ln_b.reshape(DEPTH, 1, D_B),
        "wcat_p": wcat_p, "wcat_s": wcat_s, "sbias_p": sbias_p, "sbias_s": sbias_s,
        "w_a2": jnp.pad(gla_w_a2, ((0, 0), (0, RANK_PAD - GATE_RANK), (0, 0))).astype(BF16),
        "b_a": gla_b_a.reshape(DEPTH, 1, H_C * DK_C),
        "gla_norm_g": gla_norm_g.reshape(DEPTH, 1, DV_C),
        "ln1_g": ln1_g.reshape(DEPTH, 1, D_MODEL), "ln1_b": ln1_b.reshape(DEPTH, 1, D_MODEL),
        "ffn_conv_w": ffn_conv_w,
        "ffn_conv_b": ffn_conv_b.reshape(DEPTH, 1, 2 * D_FF),
        "ln2_g": ln2_g.reshape(DEPTH, 1, D_MODEL), "ln2_b": ln2_b.reshape(DEPTH, 1, D_MODEL),
    }


def kernel(x_prompt, x_sample, c_prompt, c_sample, state_conv_a, state_gla, state_ffn_conv,
           ln_in_g, ln_in_b, w_ada, b_ada, w_in, conv_a_w, sgu_ln_g, sgu_ln_b, sgu_w, sgu_b,
           gla_w_a2, gla_b_a, gla_norm_g, w_o, ln1_g, ln1_b, ffn_w_up, ffn_conv_w, ffn_conv_b,
           ffn_w_down, ln2_g, ln2_b):
    p = _prep_params(w_in, conv_a_w, sgu_ln_g, sgu_ln_b, sgu_w, sgu_b, gla_w_a2, gla_b_a,
                     gla_norm_g, w_o, ln1_g, ln1_b, ffn_w_up, ffn_conv_w, ffn_conv_b,
                     ffn_w_down, ln2_g, ln2_b)
    mod = _ada_call(jnp.concatenate([c_sample, c_prompt], axis=0), w_ada, b_ada)
    mod_p = mod_s = mod
    s0_all = state_gla.reshape(DEPTH, DEC_BATCH, H_C * DK_C, DV_C)

    p["ln_in_g"] = ln_in_g.reshape(1, D_MODEL)
    p["ln_in_b"] = ln_in_b.reshape(1, D_MODEL)
    hp = x_prompt.reshape(BATCH * SEQ, D_MODEL)
    hs = x_sample

    outs_p = [[], [], [], []]
    outs_s = [[], [], [], []]
    w_in_b, w_o_b = _convert_layer0_call(p["w_in_t"], p["w_in_tail"], w_o)
    for l in range(DEPTH):
        hp, ca, cv, gs, w_up_b, w_dn_b = _mixer_prompt_call(l, hp, mod_p, p, w_in_b, w_o_b)
        hs, ca_s_l, cv_s_l, gs_s_l = _mixer_sample_call(l, hs, mod_s, state_conv_a, s0_all, p,
                                                        w_in_b, w_o_b)
        ffn_out = _ffn_prompt_call(l, hp, mod_p, p, w_up_b, w_dn_b)
        hp, fs = ffn_out[0], ffn_out[1]
        if l + 1 < DEPTH:
            w_in_b, w_o_b = ffn_out[2], ffn_out[3]
        for acc, val in zip(outs_p, (ca, cv, gs, fs)):
            acc.append(val)
        hs, fs = _ffn_sample_call(l, hs, mod_s, state_ffn_conv, p, w_up_b, w_dn_b)
        for acc, val in zip(outs_s, (ca_s_l, cv_s_l, gs_s_l, fs)):
            acc.append(val)

    ca_p, cv_p, gla_p, ffn_p = [jnp.stack(v) for v in outs_p]
    ca_s, cv_s, gla_s, ffn_s = [jnp.stack(v) for v in outs_s]
    return (hp.reshape(BATCH, SEQ, D_MODEL), hs,
            ca_p, ca_s, cv_p, cv_s,
            gla_p.reshape(DEPTH, BATCH, H_C, DK_C, DV_C),
            gla_s.reshape(DEPTH, DEC_BATCH, H_C, DK_C, DV_C),
            ffn_p, ffn_s)
```

```python
import functools
import math

import jax
import jax.numpy as jnp
from jax import lax
from jax.experimental import pallas as pl
from jax.experimental.pallas import tpu as pltpu

D_MODEL = 1024
BATCH = 8
SEQ = 2048
DEPTH = 4
DEC_BATCH = 128
DEC_SEQ = 8
D_A = 256
D_B = 256
H_B = 4
DH_B = 64
D_C = 512
H_C = 4
DV_C = 128
DK_C = 64
GATE_RANK = 16
GATE_TAU = 16.0
CHUNK = 128
D_FF = 2816
LN_EPS = 1e-5
ALPHA = (2.0 * DEPTH) ** 0.25
N_IN = 2832
N_IN_PAD = 2944
RANK_PAD = 128

O_AIN, O_AB, O_AC, O_SU, O_SV, O_Q, O_K, O_V, O_G, O_R = (
    0, 256, 512, 768, 1024, 1280, 1536, 1792, 2304, 2816)

SUBLANES = 8
VMEM_LIMIT = 56 * 1024 * 1024

TS_MIX = 1024
MIX_RB = 256
MIX_CB = 256
NORM_RB = 32
TS_FFN = 256
TB_S = 32
GROUP_SEQ = CHUNK // DEC_SEQ
SEQ_UNROLL = 4

F32 = jnp.float32
BF16 = jnp.bfloat16


def _dot(a, b):
    return jnp.dot(a, b, preferred_element_type=F32)


def _dot_nt(a, b):
    return lax.dot_general(a, b, (((1,), (1,)), ((), ())), preferred_element_type=F32)


def _dot_tn(a, b):
    return lax.dot_general(a, b, (((0,), (0,)), ((), ())), preferred_element_type=F32)


def _layer_norm(x, g, b):
    mu = jnp.mean(x, axis=-1, keepdims=True)
    xc = x - mu
    var = jnp.mean(xc * xc, axis=-1, keepdims=True)
    return xc * lax.rsqrt(var + LN_EPS) * g + b


def _gelu(x):
    c = math.sqrt(2.0 / math.pi)
    return 0.5 * x * (1.0 + jnp.tanh(c * (x + 0.044715 * (x * x * x))))


def _silu(x):
    return x * jax.nn.sigmoid(x)


def _log_sigmoid(x):
    return jnp.minimum(x, 0.0) - jnp.log1p(jnp.exp(-jnp.abs(x)))


def _split3(x):
    hi = x.astype(BF16)
    r1 = x - hi.astype(F32)
    mid = r1.astype(BF16)
    lo = (r1 - mid.astype(F32)).astype(BF16)
    return hi, mid, lo


def _dot_exact_lhs(m_bf16, x):
    hi, mid, lo = _split3(x)
    return _dot(m_bf16, hi) + _dot(m_bf16, mid) + _dot(m_bf16, lo)


def _head_masks(width, n_heads):
    lane = lax.broadcasted_iota(jnp.int32, (1, width), 1)
    per = width // n_heads
    return [(lane >= h * per) & (lane < (h + 1) * per) for h in range(n_heads)]


def _interleave(valu_pieces, mxu_pieces):
    n_v, n_m = len(valu_pieces), len(mxu_pieces)
    done = 0
    for k, mp in enumerate(mxu_pieces):
        upto = (k + 1) * n_v // (n_m + 1)
        for vp in valu_pieces[done:upto]:
            vp()
        done = upto
        mp()
    for vp in valu_pieces[done:]:
        vp()


def _ada_kernel(c_ref, w_ref, b_ref, o_ref):
    cs = _silu(c_ref[...]).astype(BF16)
    m = _dot(cs, w_ref[...].astype(BF16)) + b_ref[...]
    for k in range(6):
        @pl.when(pl.program_id(1) == k)
        def _():
            o_ref[:, k, :] = m


def _ada_call(c_all, w_ada, b_ada):
    n_rows = c_all.shape[0]
    return pl.pallas_call(
        _ada_kernel,
        grid=(DEPTH, 6),
        in_specs=[
            pl.BlockSpec((n_rows, D_MODEL), lambda l, j: (0, 0)),
            pl.BlockSpec((None, D_MODEL, D_MODEL), lambda l, j: (l, 0, j)),
            pl.BlockSpec((None, 1, D_MODEL), lambda l, j: (l, 0, j)),
        ],
        out_specs=pl.BlockSpec((None, n_rows, 6, D_MODEL), lambda l, j: (l, 0, 0, 0)),
        out_shape=jax.ShapeDtypeStruct((DEPTH, n_rows, 6, D_MODEL), F32),
        compiler_params=pltpu.CompilerParams(
            dimension_semantics=("arbitrary", "arbitrary"), vmem_limit_bytes=VMEM_LIMIT),
        name="ada_mod",
    )(c_all, w_ada, b_ada.reshape(DEPTH, 1, 6 * D_MODEL))


LANES = 128
N_IN_SLABS = N_IN // LANES


def _convert_in_proj_slab(step, wt_ref, tail_ref, winb_ref):
    @pl.when(step < N_IN_SLABS)
    def _():
        winb_ref[...] = wt_ref[...].T.astype(BF16)

    @pl.when(step == N_IN_SLABS)
    def _():
        winb_ref[...] = tail_ref[...].T.astype(BF16)


def _convert_kernel(wt_ref, tail_ref, wof_ref, winb_ref, wob_ref):
    step = pl.program_id(0)
    _convert_in_proj_slab(step, wt_ref, tail_ref, winb_ref)

    @pl.when(step == 0)
    def _():
        wob_ref[...] = wof_ref[...].astype(BF16)


def _convert_layer0_call(w_in_t, w_in_tail, w_o):
    return pl.pallas_call(
        _convert_kernel,
        grid=(N_IN_SLABS + 1,),
        in_specs=[
            pl.BlockSpec((None, LANES, D_MODEL), lambda i: (0, jnp.minimum(i, N_IN_SLABS - 1), 0)),
            pl.BlockSpec((None, LANES, D_MODEL), lambda i: (0, 0, 0)),
            pl.BlockSpec((None, D_MODEL, D_MODEL), lambda i: (0, 0, 0)),
        ],
        out_specs=[
            pl.BlockSpec((D_MODEL, LANES), lambda i: (0, i)),
            pl.BlockSpec((D_MODEL, D_MODEL), lambda i: (0, 0)),
        ],
        out_shape=[
            jax.ShapeDtypeStruct((D_MODEL, N_IN_PAD), BF16),
            jax.ShapeDtypeStruct((D_MODEL, D_MODEL), BF16),
        ],
        compiler_params=pltpu.CompilerParams(dimension_semantics=("arbitrary",)),
        name="convert_layer0",
    )(w_in_t, w_in_tail, w_o)


def _sgu_chunk(z_ref, zrows, y_ref, rows, wcat, sbias, ln_g, ln_b, masks_b):
    su = _gelu(z_ref[zrows, O_SU:O_SU + D_B])
    sv = _layer_norm(_gelu(z_ref[zrows, O_SV:O_SV + D_B]), ln_g, ln_b)
    svb = sv.astype(BF16)
    zero = jnp.zeros_like(svb)
    svm = jnp.concatenate([jnp.where(m, svb, zero) for m in masks_b], axis=0)
    mixed = _dot(wcat, svm) + sbias
    y_ref[rows, D_A:D_A + D_B] = (su * mixed).astype(BF16)
    return sv


def _gla_gate(z_ref, rows, wa2, ba):
    cr = z_ref[rows, O_R:O_R + RANK_PAD].astype(BF16)
    return _log_sigmoid(_dot(cr, wa2) + ba) * (1.0 / GATE_TAU)


def _gla_head_out(z_ref, zrows, y_ref, rows, hh, o, gn):
    ms = jnp.mean(o * o, axis=-1, keepdims=True)
    on = o * lax.rsqrt(ms + LN_EPS) * gn
    g = z_ref[zrows, O_G + hh * DV_C:O_G + (hh + 1) * DV_C]
    y_ref[rows, D_A + D_B + hh * DV_C:D_A + D_B + (hh + 1) * DV_C] = (_silu(g) * on).astype(BF16)


def _const_spec(shape, n_grid):
    zeros = (0,) * len(shape)
    if n_grid == 1:
        return pl.BlockSpec(shape, lambda i: zeros, pipeline_mode=pl.Buffered(1))
    return pl.BlockSpec(shape, lambda b, j: zeros, pipeline_mode=pl.Buffered(1))


def _layer_spec(shape, l, n_grid):
    idx = (l,) + (0,) * len(shape)
    if n_grid == 1:
        return pl.BlockSpec((None,) + shape, lambda i: idx, pipeline_mode=pl.Buffered(1))
    return pl.BlockSpec((None,) + shape, lambda b, j: idx, pipeline_mode=pl.Buffered(1))


def _mixer_prompt_kernel(*refs, ts, n_tiles, first):
    if first:
        (x_ref, mod_ref, wupf_ref, wdnf_ref, lnig_ref, lnib_ref, win_ref, cw_ref, sg_ref, sb_ref,
         wcat_ref, sbias_ref, wa2_ref, ba_ref, gn_ref, wo_ref, lng_ref, lnb_ref,
         hout_ref, ca_ref, cv_ref, gla_ref, wupb_ref, wdnb_ref,
         u_ref, z_ref, xa_ref, y_ref, st_ref, yo_ref, h_ref) = refs
        for r in range(0, ts, NORM_RB):
            h_ref[r:r + NORM_RB, :] = _layer_norm(x_ref[r:r + NORM_RB, :], lnig_ref[...], lnib_ref[...])
    else:
        (h_ref, mod_ref, wupf_ref, wdnf_ref, win_ref, cw_ref, sg_ref, sb_ref,
         wcat_ref, sbias_ref, wa2_ref, ba_ref, gn_ref, wo_ref, lng_ref, lnb_ref,
         hout_ref, ca_ref, cv_ref, gla_ref, wupb_ref, wdnb_ref,
         u_ref, z_ref, xa_ref, y_ref, st_ref, yo_ref) = refs
    wupb_ref[...] = wupf_ref[...].astype(BF16)
    wdnb_ref[...] = wdnf_ref[...].astype(BF16)
    j = pl.program_id(1)

    @pl.when(j == 0)
    def _():
        xa_ref[0:SUBLANES, :] = jnp.zeros((SUBLANES, D_A), F32)
        st_ref[...] = jnp.zeros_like(st_ref)

    sh1, sc1, g1 = mod_ref[0:1, :], mod_ref[1:2, :], mod_ref[2:3, :]
    for r in range(0, ts, NORM_RB):
        u_ref[r:r + NORM_RB, :] = (h_ref[r:r + NORM_RB, :] * (1.0 + sc1) + sh1).astype(BF16)

    def z_rows(r0, n):
        p, local = divmod(r0, MIX_RB)
        base = (p % 2) * MIX_RB + local
        return slice(base, base + n)

    def in_proj_pieces(p):
        rows = slice(p * MIX_RB, (p + 1) * MIX_RB)
        zrows = z_rows(p * MIX_RB, MIX_RB)

        def piece(c0):
            cols = slice(c0, min(c0 + MIX_CB, N_IN_PAD))
            z_ref[zrows, cols] = _dot(u_ref[rows, :], win_ref[:, cols])
        return [functools.partial(piece, c0) for c0 in range(0, N_IN_PAD, MIX_CB)]

    def out_proj_pieces(p):
        rows = slice(p * MIX_RB, (p + 1) * MIX_RB)

        def piece(c0):
            cols = slice(c0, c0 + MIX_CB)
            yo_ref[rows, cols] = _dot(y_ref[rows, :], wo_ref[:, cols])
        return [functools.partial(piece, c0) for c0 in range(0, D_MODEL, MIX_CB)]

    def norm_pieces(p):
        def piece(r):
            rows = slice(r, r + NORM_RB)
            hout_ref[rows, :] = _layer_norm(ALPHA * h_ref[rows, :] + g1 * yo_ref[rows, :],
                                            lng_ref[...], lnb_ref[...])
        return [functools.partial(piece, r)
                for r in range(p * MIX_RB, (p + 1) * MIX_RB, NORM_RB)]

    cw = cw_ref[...]
    wcat = wcat_ref[...].astype(BF16)
    sbias = sbias_ref[...]
    wa2 = wa2_ref[...]
    ba = ba_ref[...]
    gn = gn_ref[...]
    sg, sb = sg_ref[...], sb_ref[...]
    masks_b = _head_masks(D_B, H_B)
    masks_k = _head_masks(H_C * DK_C, H_C)
    row = lax.broadcasted_iota(jnp.int32, (CHUNK, CHUNK), 0)
    col = lax.broadcasted_iota(jnp.int32, (CHUNK, CHUNK), 1)
    causal = col <= row
    tri = causal.astype(BF16)

    def mix_pieces(c):
        r0 = c * CHUNK
        rows = slice(r0, r0 + CHUNK)
        zrows = z_rows(r0, CHUNK)
        v = {}

        def conv_a():
            x = z_ref[zrows, O_AC:O_AC + D_A] * z_ref[zrows, O_AIN:O_AIN + D_A]
            xa_ref[r0 + SUBLANES:r0 + SUBLANES + CHUNK, :] = x
            win = xa_ref[r0:r0 + CHUNK + SUBLANES, :]
            conv = (cw[0:1, :] * win[SUBLANES - 2:SUBLANES - 2 + CHUNK]
                    + cw[1:2, :] * win[SUBLANES - 1:SUBLANES - 1 + CHUNK]
                    + cw[2:3, :] * x)
            y_ref[rows, 0:D_A] = (z_ref[zrows, O_AB:O_AB + D_A] * conv).astype(BF16)

        def sgu():
            cv_ref[...] = _sgu_chunk(z_ref, zrows, y_ref, rows, wcat, sbias, sg, sb, masks_b)

        def gla_prep():
            lg = _gla_gate(z_ref, zrows, wa2, ba)
            b = _dot_exact_lhs(tri, lg)
            b_last = b[CHUNK - 1:CHUNK, :]
            b_mid = b[CHUNK // 2 - 1:CHUNK // 2, :]
            q = z_ref[zrows, O_Q:O_Q + H_C * DK_C]
            k = z_ref[zrows, O_K:O_K + H_C * DK_C]
            v["qt"] = (q * (DK_C ** -0.5)) * jnp.exp(b - b_mid)
            v["kt"] = (k * jnp.exp(b_mid - b)).astype(BF16)
            v["kh"] = k * jnp.exp(b_last - b)
            st = st_ref[...]
            v["stb"] = (st * jnp.exp(b_mid)).astype(BF16)
            v["st_new"] = st * jnp.exp(b_last)

        def gla_head(hh):
            zq = jnp.zeros_like(v["qt"])
            qm = jnp.where(masks_k[hh], v["qt"], zq).astype(BF16)
            if "a_all" not in v:
                zk = jnp.zeros_like(v["kt"])
                kblk = jnp.concatenate([jnp.where(m, v["kt"], zk) for m in masks_k], axis=0)
                v["a_all"] = _dot_nt(v["qt"].astype(BF16), kblk)
            a = jnp.where(causal, v["a_all"][:, hh * CHUNK:(hh + 1) * CHUNK], 0.0).astype(BF16)
            vb = z_ref[zrows, O_V + hh * DV_C:O_V + (hh + 1) * DV_C].astype(BF16)
            o = _dot(a, vb) + _dot_nt(qm, v["stb"])
            _gla_head_out(z_ref, zrows, y_ref, rows, hh, o, gn)
            khm = jnp.where(masks_k[hh], v["kh"], zq).astype(BF16)
            v["st_new"] = v["st_new"] + _dot_tn(vb, khm)

        def gla_state():
            st_ref[...] = v["st_new"]

        return ([conv_a, sgu, gla_prep] + [functools.partial(gla_head, hh) for hh in range(H_C)]
                + [gla_state])

    n_rb = ts // MIX_RB
    per = MIX_RB // CHUNK
    _interleave([], in_proj_pieces(0))
    for p in range(n_rb):
        valu = [pc for c in range(p * per, (p + 1) * per) for pc in mix_pieces(c)]
        if p >= 2:
            valu = valu + norm_pieces(p - 2)
        mxu = in_proj_pieces(p + 1) if p + 1 < n_rb else []
        if p >= 1:
            mxu = mxu + out_proj_pieces(p - 1)
        _interleave(valu, mxu)
    _interleave(norm_pieces(n_rb - 2) if n_rb >= 2 else [], out_proj_pieces(n_rb - 1))
    _interleave(norm_pieces(n_rb - 1), [])

    ca_ref[...] = xa_ref[ts + SUBLANES - 2:ts + SUBLANES, :]
    xa_ref[0:SUBLANES, :] = xa_ref[ts:ts + SUBLANES, :]

    @pl.when(j == n_tiles - 1)
    def _():
        gla_ref[...] = st_ref[...].T


def _mixer_prompt_call(l, h, mod, p, w_in_b, w_o_b):
    ts = TS_MIX
    n_tiles = SEQ // ts
    n_steps = BATCH * n_tiles
    first = l == 0
    ls = functools.partial(_layer_spec, l=l, n_grid=2)
    cs = functools.partial(_const_spec, n_grid=2)
    kern = functools.partial(_mixer_prompt_kernel, ts=ts, n_tiles=n_tiles, first=first)
    ln_in_specs = [pl.BlockSpec((1, D_MODEL), lambda b, j: (0, 0))] * 2 if first else []
    ln_in_args = (p["ln_in_g"], p["ln_in_b"]) if first else ()
    ln_in_scratch = [pltpu.VMEM((ts, D_MODEL), F32)] if first else []
    up_rows = D_MODEL // n_steps
    dn_rows = D_FF // n_steps
    return pl.pallas_call(
        kern,
        grid=(BATCH, n_tiles),
        in_specs=[
            pl.BlockSpec((ts, D_MODEL), lambda b, j: (b * n_tiles + j, 0)),
            pl.BlockSpec((None, None, 6, D_MODEL), lambda b, j: (l, DEC_BATCH + b, 0, 0)),
            pl.BlockSpec((None, up_rows, 2 * D_FF), lambda b, j: (l, b * n_tiles + j, 0)),
            pl.BlockSpec((None, dn_rows, D_MODEL), lambda b, j: (l, b * n_tiles + j, 0)),
            *ln_in_specs,
            cs((D_MODEL, N_IN_PAD)),
            ls((3, D_A)),
            ls((1, D_B)),
            ls((1, D_B)),
            ls((CHUNK, H_B * CHUNK)),
            ls((CHUNK, D_B)),
            ls((RANK_PAD, H_C * DK_C)),
            ls((1, H_C * DK_C)),
            ls((1, DV_C)),
            cs((D_MODEL, D_MODEL)),
            ls((1, D_MODEL)),
            ls((1, D_MODEL)),
        ],
        out_specs=[
            pl.BlockSpec((ts, D_MODEL), lambda b, j: (b * n_tiles + j, 0)),
            pl.BlockSpec((None, 2, D_A), lambda b, j: (b, 0, 0)),
            pl.BlockSpec((None, CHUNK, D_B), lambda b, j: (b, 0, 0)),
            pl.BlockSpec((None, H_C * DK_C, DV_C), lambda b, j: (b, 0, 0)),
            pl.BlockSpec((up_rows, 2 * D_FF), lambda b, j: (b * n_tiles + j, 0)),
            pl.BlockSpec((dn_rows, D_MODEL), lambda b, j: (b * n_tiles + j, 0)),
        ],
        out_shape=[
            jax.ShapeDtypeStruct((BATCH * SEQ, D_MODEL), F32),
            jax.ShapeDtypeStruct((BATCH, 2, D_A), F32),
            jax.ShapeDtypeStruct((BATCH, CHUNK, D_B), F32),
            jax.ShapeDtypeStruct((BATCH, H_C * DK_C, DV_C), F32),
            jax.ShapeDtypeStruct((D_MODEL, 2 * D_FF), BF16),
            jax.ShapeDtypeStruct((D_FF, D_MODEL), BF16),
        ],
        scratch_shapes=[
            pltpu.VMEM((ts, D_MODEL), BF16),
            pltpu.VMEM((2 * MIX_RB, N_IN_PAD), F32),
            pltpu.VMEM((ts + SUBLANES, D_A), F32),
            pltpu.VMEM((ts, D_MODEL), BF16),
            pltpu.VMEM((DV_C, H_C * DK_C), F32),
            pltpu.VMEM((ts, D_MODEL), F32),
            *ln_in_scratch,
        ],
        compiler_params=pltpu.CompilerParams(
            dimension_semantics=("arbitrary", "arbitrary"), vmem_limit_bytes=VMEM_LIMIT),
        name="mixer_prompt",
    )(h, mod, p["ffn_w_up"], p["ffn_w_down"], *ln_in_args, w_in_b, p["conv_a_w"], p["sgu_ln_g"],
      p["sgu_ln_b"], p["wcat_p"], p["sbias_p"], p["w_a2"], p["b_a"], p["gla_norm_g"], w_o_b,
      p["ln1_g"], p["ln1_b"])


def _expand_rows(x3):
    n, _, c = x3.shape
    return jnp.broadcast_to(x3, (n, DEC_SEQ, c)).reshape(n * DEC_SEQ, c)


def _sample_conv(x, past, cw):
    p0 = _expand_rows(past[:, 0:1, :])
    p1 = _expand_rows(past[:, 1:2, :])
    t_idx = lax.broadcasted_iota(jnp.int32, x.shape, 0) & (DEC_SEQ - 1)
    xm1 = jnp.where(t_idx >= 1, pltpu.roll(x, 1, 0), p1)
    xm2 = jnp.where(t_idx >= 2, pltpu.roll(x, 2, 0), jnp.where(t_idx == 1, p1, p0))
    return cw[0:1, :] * xm2 + cw[1:2, :] * xm1 + cw[2:3, :] * x


def _mixer_sample_kernel(*refs, tb, first):
    if first:
        (h_ref, mod_ref, lnig_ref, lnib_ref, past_ref, s0_ref, win_ref, cw_ref, sg_ref, sb_ref,
         wcat_ref, sbias_ref, wa2_ref, ba_ref, gn_ref, wo_ref, lng_ref, lnb_ref,
         hout_ref, ca_ref, cv_ref, gla_ref,
         z_ref, y_ref, qt_ref, kh_ref, eb_ref, oi_ref, os_ref) = refs
    else:
        (h_ref, mod_ref, past_ref, s0_ref, win_ref, cw_ref, sg_ref, sb_ref,
         wcat_ref, sbias_ref, wa2_ref, ba_ref, gn_ref, wo_ref, lng_ref, lnb_ref,
         hout_ref, ca_ref, cv_ref, gla_ref,
         z_ref, y_ref, qt_ref, kh_ref, eb_ref, oi_ref, os_ref) = refs
    rr = tb * DEC_SEQ
    h = h_ref[...].reshape(rr, D_MODEL)
    if first:
        h = _layer_norm(h, lnig_ref[...], lnib_ref[...])
    sh1 = _expand_rows(mod_ref[:, 0:1, :])
    sc1 = _expand_rows(mod_ref[:, 1:2, :])
    g1 = _expand_rows(mod_ref[:, 2:3, :])
    u = (h * (1.0 + sc1) + sh1).astype(BF16)
    z_ref[...] = _dot(u, win_ref[...])

    x = z_ref[:, O_AC:O_AC + D_A] * z_ref[:, O_AIN:O_AIN + D_A]
    conv = _sample_conv(x, past_ref[...], cw_ref[...])
    y_ref[:, 0:D_A] = (z_ref[:, O_AB:O_AB + D_A] * conv).astype(BF16)
    ca_ref[...] = x.reshape(tb, DEC_SEQ, D_A)[:, DEC_SEQ - 2:DEC_SEQ, :]

    wcat = wcat_ref[...].astype(BF16)
    sbias = sbias_ref[...]
    wa2 = wa2_ref[...]
    ba = ba_ref[...]
    gn = gn_ref[...]
    sg, sb = sg_ref[...], sb_ref[...]
    masks_b = _head_masks(D_B, H_B)
    masks_k = _head_masks(H_C * DK_C, H_C)
    row = lax.broadcasted_iota(jnp.int32, (CHUNK, CHUNK), 0)
    col = lax.broadcasted_iota(jnp.int32, (CHUNK, CHUNK), 1)
    same_seq = (row // DEC_SEQ) == (col // DEC_SEQ)
    causal = same_seq & (col <= row)
    tri = causal.astype(BF16)
    blk = same_seq.astype(BF16)

    for gi in range(rr // CHUNK):
        rows = pl.ds(gi * CHUNK, CHUNK)
        sv = _sgu_chunk(z_ref, rows, y_ref, rows, wcat, sbias, sg, sb, masks_b)
        cv_ref[gi * GROUP_SEQ:(gi + 1) * GROUP_SEQ] = sv.reshape(GROUP_SEQ, DEC_SEQ, D_B)
        lg = _gla_gate(z_ref, rows, wa2, ba)
        b = _dot_exact_lhs(tri, lg)
        b_tot = _dot_exact_lhs(blk, lg)
        q = z_ref[rows, O_Q:O_Q + H_C * DK_C]
        k = z_ref[rows, O_K:O_K + H_C * DK_C]
        qt = (q * (DK_C ** -0.5)) * jnp.exp(b)
        kt = (k * jnp.exp(-b)).astype(BF16)
        qt_ref[rows, :] = qt
        kh_ref[rows, :] = k * jnp.exp(b_tot - b)
        eb_ref[rows, :] = jnp.exp(b_tot)
        zq = jnp.zeros_like(qt)
        for hh in range(H_C):
            qm = jnp.where(masks_k[hh], qt, zq).astype(BF16)
            a = jnp.where(causal, _dot_nt(qm, kt), 0.0).astype(BF16)
            vb = z_ref[rows, O_V + hh * DV_C:O_V + (hh + 1) * DV_C].astype(BF16)
            oi_ref[rows, hh * DV_C:(hh + 1) * DV_C] = _dot(a, vb)

    ones16 = jnp.ones((2 * SUBLANES, DV_C), BF16)
    sub = lax.broadcasted_iota(jnp.int32, (SUBLANES, H_C * DK_C), 0)

    def seq_body(i, carry):
        r0 = pl.multiple_of(i * DEC_SEQ, DEC_SEQ)
        rows = pl.ds(r0, DEC_SEQ)
        s0 = s0_ref[i]
        qt = qt_ref[rows, :]
        kh = kh_ref[rows, :]
        zq = jnp.zeros_like(qt)
        q_exp = jnp.concatenate([jnp.where(m, qt, zq) for m in masks_k], axis=0).astype(BF16)
        k_exp = jnp.concatenate([jnp.where(m, kh, zq) for m in masks_k], axis=0).astype(BF16)
        o_int = _dot(q_exp, s0.astype(BF16))
        v = z_ref[rows, O_V:O_V + D_C]
        v_exp = jnp.concatenate([v[:, hh * DV_C:(hh + 1) * DV_C] for hh in range(H_C)],
                                axis=0).astype(BF16)
        eb = eb_ref[rows, :]
        hi, mid, lo = _split3(eb)
        e3 = jnp.where(sub == 0, hi.astype(F32),
                       jnp.where(sub == 1, mid.astype(F32),
                                 jnp.where(sub == 2, lo.astype(F32), jnp.zeros_like(eb))))
        e3 = jnp.concatenate([e3, jnp.zeros_like(eb)], axis=0).astype(BF16)
        decay = _dot_tn(e3, ones16)
        gla_ref[i] = decay * s0 + _dot_tn(k_exp, v_exp)
        for hh in range(H_C):
            os_ref[rows, hh * DV_C:(hh + 1) * DV_C] = o_int[hh * DEC_SEQ:(hh + 1) * DEC_SEQ, :]
        return carry

    lax.fori_loop(0, tb, seq_body, 0, unroll=SEQ_UNROLL)

    for gi in range(rr // CHUNK):
        rows = pl.ds(gi * CHUNK, CHUNK)
        for hh in range(H_C):
            cols = slice(hh * DV_C, (hh + 1) * DV_C)
            _gla_head_out(z_ref, rows, y_ref, rows, hh, oi_ref[rows, cols] + os_ref[rows, cols], gn)

    y = _dot(y_ref[...], wo_ref[...])
    hn = _layer_norm(ALPHA * h + g1 * y, lng_ref[...], lnb_ref[...])
    hout_ref[...] = hn.reshape(tb, DEC_SEQ, D_MODEL)


def _mixer_sample_call(l, h3, mod, past, s0, p, w_in_b, w_o_b):
    tb = TB_S
    ls = functools.partial(_layer_spec, l=l, n_grid=1)
    cs = functools.partial(_const_spec, n_grid=1)
    first = l == 0
    kern = functools.partial(_mixer_sample_kernel, tb=tb, first=first)
    ln_in_specs = [pl.BlockSpec((1, D_MODEL), lambda i: (0, 0))] * 2 if first else []
    ln_in_args = (p["ln_in_g"], p["ln_in_b"]) if first else ()
    rr = tb * DEC_SEQ
    return pl.pallas_call(
        kern,
        grid=(DEC_BATCH // tb,),
        in_specs=[
            pl.BlockSpec((tb, DEC_SEQ, D_MODEL), lambda i: (i, 0, 0)),
            pl.BlockSpec((None, tb, 6, D_MODEL), lambda i: (l, i, 0, 0)),
            *ln_in_specs,
            pl.BlockSpec((None, tb, 2, D_A), lambda i: (l, i, 0, 0)),
            pl.BlockSpec((None, tb, H_C * DK_C, DV_C), lambda i: (l, i, 0, 0)),
            cs((D_MODEL, N_IN_PAD)),
            ls((3, D_A)),
            ls((1, D_B)),
            ls((1, D_B)),
            ls((CHUNK, H_B * CHUNK)),
            ls((CHUNK, D_B)),
            ls((RANK_PAD, H_C * DK_C)),
            ls((1, H_C * DK_C)),
            ls((1, DV_C)),
            cs((D_MODEL, D_MODEL)),
            ls((1, D_MODEL)),
            ls((1, D_MODEL)),
        ],
        out_specs=[
            pl.BlockSpec((tb, DEC_SEQ, D_MODEL), lambda i: (i, 0, 0)),
            pl.BlockSpec((tb, 2, D_A), lambda i: (i, 0, 0)),
            pl.BlockSpec((tb, DEC_SEQ, D_B), lambda i: (i, 0, 0)),
            pl.BlockSpec((tb, H_C * DK_C, DV_C), lambda i: (i, 0, 0)),
        ],
        out_shape=[
            jax.ShapeDtypeStruct((DEC_BATCH, DEC_SEQ, D_MODEL), F32),
            jax.ShapeDtypeStruct((DEC_BATCH, 2, D_A), F32),
            jax.ShapeDtypeStruct((DEC_BATCH, DEC_SEQ, D_B), F32),
            jax.ShapeDtypeStruct((DEC_BATCH, H_C * DK_C, DV_C), F32),
        ],
        scratch_shapes=[
            pltpu.VMEM((rr, N_IN_PAD), F32),
            pltpu.VMEM((rr, D_MODEL), BF16),
            pltpu.VMEM((rr, H_C * DK_C), F32),
            pltpu.VMEM((rr, H_C * DK_C), F32),
            pltpu.VMEM((rr, H_C * DK_C), F32),
            pltpu.VMEM((rr, D_C), F32),
            pltpu.VMEM((rr, D_C), F32),
        ],
        compiler_params=pltpu.CompilerParams(
            dimension_semantics=("arbitrary",), vmem_limit_bytes=VMEM_LIMIT),
        name="mixer_sample",
    )(h3, mod, *ln_in_args, past, s0, w_in_b, p["conv_a_w"], p["sgu_ln_g"], p["sgu_ln_b"], p["wcat_s"],
      p["sbias_s"], p["w_a2"], p["b_a"], p["gla_norm_g"], w_o_b, p["ln1_g"], p["ln1_b"])


FFN_RB = 32
FFN_CB = 256
N_CB = D_FF // FFN_CB
FFN_AHEAD = 3
FFN_AHEAD_EXTRA = 4
FFN_SUB = 256


def _ffn_kernel(*refs, rows, sample, convert_next):
    if sample:
        (h_ref, mod_ref, past_ref, wup_ref, cw_ref, cb_ref, wdn_ref, lng_ref, lnb_ref,
         hout_ref, st_ref, u_ref, xp_ref, act_ref, acc_ref) = refs
        off = 0
    else:
        if convert_next:
            (h_ref, mod_ref, wt_ref, tail_ref, wof_ref, wup_ref, cw_ref, cb_ref, wdn_ref, lng_ref,
             lnb_ref, hout_ref, st_ref, winb_ref, wob_ref, u_ref, xp_ref, act_ref, acc_ref) = refs
            step = pl.program_id(0) * pl.num_programs(1) + pl.program_id(1)
            _convert_in_proj_slab(step, wt_ref, tail_ref, winb_ref)
            wob_ref[...] = wof_ref[...].astype(BF16)
        else:
            (h_ref, mod_ref, wup_ref, cw_ref, cb_ref, wdn_ref, lng_ref, lnb_ref,
             hout_ref, st_ref, u_ref, xp_ref, act_ref, acc_ref) = refs
        off = SUBLANES

        @pl.when(pl.program_id(1) == 0)
        def _():
            xp_ref[0:SUBLANES, :] = jnp.zeros((SUBLANES, 2 * D_FF), F32)

    def rows_of(r):
        if sample:
            seqs = slice(r // DEC_SEQ, (r + FFN_RB) // DEC_SEQ)
            return (h_ref[seqs].reshape(FFN_RB, D_MODEL),
                    [_expand_rows(mod_ref[seqs, k:k + 1, :]) for k in (3, 4, 5)])
        return h_ref[r:r + FFN_RB, :], [mod_ref[k:k + 1, :] for k in (3, 4, 5)]

    def modulate_pieces(s):
        def piece(r):
            hr, (sh2, sc2, _) = rows_of(r)
            u_ref[r:r + FFN_RB, :] = (hr * (1.0 + sc2) + sh2).astype(BF16)
        return [functools.partial(piece, r)
                for r in range(s * FFN_SUB, (s + 1) * FFN_SUB, FFN_RB)]

    n_sub = rows // FFN_SUB
    items = [(s, c) for s in range(n_sub) for c in range(N_CB)]

    def block_cols(c):
        return (slice(c * FFN_CB, (c + 1) * FFN_CB),
                slice(D_FF + c * FFN_CB, D_FF + (c + 1) * FFN_CB))

    def up_pieces(k):
        s, c = items[k]
        r0 = s * FFN_SUB

        def piece(cols):
            xp_ref[off + r0:off + r0 + FFN_SUB, cols] = _dot(u_ref[r0:r0 + FFN_SUB, :], wup_ref[:, cols])
        return [functools.partial(piece, cols) for cols in block_cols(c)]

    def down_pieces(k):
        s, c = items[k]
        r0 = s * FFN_SUB

        def piece(n):
            cols = slice(n * FFN_CB, (n + 1) * FFN_CB)
            d = _dot(act_ref[k % 2], wdn_ref[c * FFN_CB:(c + 1) * FFN_CB, cols])
            if c == 0:
                acc_ref[r0:r0 + FFN_SUB, cols] = d
            else:
                acc_ref[r0:r0 + FFN_SUB, cols] += d
        return [functools.partial(piece, n) for n in range(D_MODEL // FFN_CB)]

    def conv_rows(r, cols):
        if sample:
            seqs = slice(r // DEC_SEQ, (r + FFN_RB) // DEC_SEQ)
            x = xp_ref[r:r + FFN_RB, cols]
            st_ref[seqs, :, cols] = x.reshape(FFN_RB // DEC_SEQ, DEC_SEQ, FFN_CB)[:, DEC_SEQ - 2:, :]
            y = _sample_conv(x, past_ref[seqs, :, cols], cw_ref[:, cols])
        else:
            win = xp_ref[r:r + FFN_RB + SUBLANES, cols]
            y = (cw_ref[0:1, cols] * win[SUBLANES - 2:SUBLANES - 2 + FFN_RB]
                 + cw_ref[1:2, cols] * win[SUBLANES - 1:SUBLANES - 1 + FFN_RB]
                 + cw_ref[2:3, cols] * win[SUBLANES:SUBLANES + FFN_RB])
        return y + cb_ref[0:1, cols]

    def elementwise_pieces(k):
        s, c = items[k]
        ca, cg = block_cols(c)

        def piece(r):
            act_ref[k % 2, r - s * FFN_SUB:r - s * FFN_SUB + FFN_RB, :] = (
                _silu(conv_rows(r, cg)) * conv_rows(r, ca)).astype(BF16)
        return [functools.partial(piece, r)
                for r in range(s * FFN_SUB, (s + 1) * FFN_SUB, FFN_RB)]

    def finish_pieces(s):
        def piece(r):
            hr, (_, _, g2) = rows_of(r)
            hn = _layer_norm(ALPHA * hr + g2 * acc_ref[r:r + FFN_RB, :], lng_ref[...], lnb_ref[...])
            if sample:
                seqs = slice(r // DEC_SEQ, (r + FFN_RB) // DEC_SEQ)
                hout_ref[seqs] = hn.reshape(FFN_RB // DEC_SEQ, DEC_SEQ, D_MODEL)
            else:
                hout_ref[r:r + FFN_RB, :] = hn
        return [functools.partial(piece, r)
                for r in range(s * FFN_SUB, (s + 1) * FFN_SUB, FFN_RB)]

    ahead = FFN_AHEAD + FFN_AHEAD_EXTRA * (n_sub - 1)
    side = [[] for _ in items]
    for s in range(n_sub):
        if s + 1 < n_sub:
            early = modulate_pieces(s + 1)
            span = N_CB - ahead
            for i, pc in enumerate(early):
                side[s * N_CB + i * span // len(early)].append(pc)
            late = finish_pieces(s)
            for i, pc in enumerate(late):
                side[(s + 1) * N_CB + 1 + i * (N_CB - 1) // len(late)].append(pc)

    n_items = len(items)
    _interleave(modulate_pieces(0), [])
    for k in range(ahead):
        _interleave([], up_pieces(k))
    for k in range(n_items):
        mxu = up_pieces(k + ahead) if k + ahead < n_items else []
        if k >= 1:
            mxu = mxu + down_pieces(k - 1)
        _interleave(elementwise_pieces(k) + side[k], mxu)
    _interleave([], down_pieces(n_items - 1))
    _interleave(finish_pieces(n_sub - 1), [])

    if not sample:
        st_ref[...] = xp_ref[rows + SUBLANES - 2:rows + SUBLANES, :]
        xp_ref[0:SUBLANES, :] = xp_ref[rows:rows + SUBLANES, :]


def _ffn_scratch(rows, off):
    return [
        pltpu.VMEM((rows, D_MODEL), BF16),
        pltpu.VMEM((rows + off, 2 * D_FF), F32),
        pltpu.VMEM((2, FFN_SUB, FFN_CB), BF16),
        pltpu.VMEM((rows, D_MODEL), F32),
    ]


def _ffn_prompt_call(l, h, mod, p, w_up_b, w_dn_b):
    ts = TS_FFN
    n_tiles = SEQ // ts
    n_steps = BATCH * n_tiles
    convert_next = l + 1 < DEPTH
    ls = functools.partial(_layer_spec, l=l, n_grid=2)
    cs = functools.partial(_const_spec, n_grid=2)
    kern = functools.partial(_ffn_kernel, rows=ts, sample=False, convert_next=convert_next)
    slab = D_MODEL // n_steps
    nxt_specs, nxt_args, nxt_out_specs, nxt_out_shapes = [], (), [], []
    if convert_next:
        nxt_specs = [
            pl.BlockSpec((None, LANES, D_MODEL),
                         lambda b, j: (l + 1, jnp.minimum(b * n_tiles + j, N_IN_SLABS - 1), 0)),
            pl.BlockSpec((None, LANES, D_MODEL), lambda b, j: (l + 1, 0, 0)),
            pl.BlockSpec((None, slab, D_MODEL), lambda b, j: (l + 1, b * n_tiles + j, 0))]
        nxt_args = (p["w_in_t"], p["w_in_tail"], p["w_o_f32"])
        nxt_out_specs = [
            pl.BlockSpec((D_MODEL, LANES), lambda b, j: (0, jnp.minimum(b * n_tiles + j, N_IN_SLABS))),
            pl.BlockSpec((slab, D_MODEL), lambda b, j: (b * n_tiles + j, 0))]
        nxt_out_shapes = [jax.ShapeDtypeStruct((D_MODEL, N_IN_PAD), BF16),
                          jax.ShapeDtypeStruct((D_MODEL, D_MODEL), BF16)]
    return pl.pallas_call(
        kern,
        grid=(BATCH, n_tiles),
        in_specs=[
            pl.BlockSpec((ts, D_MODEL), lambda b, j: (b * n_tiles + j, 0)),
            pl.BlockSpec((None, None, 6, D_MODEL), lambda b, j: (l, DEC_BATCH + b, 0, 0)),
            *nxt_specs,
            cs((D_MODEL, 2 * D_FF)),
            ls((3, 2 * D_FF)),
            ls((1, 2 * D_FF)),
            cs((D_FF, D_MODEL)),
            ls((1, D_MODEL)),
            ls((1, D_MODEL)),
        ],
        out_specs=[
            pl.BlockSpec((ts, D_MODEL), lambda b, j: (b * n_tiles + j, 0)),
            pl.BlockSpec((None, 2, 2 * D_FF), lambda b, j: (b, 0, 0)),
            *nxt_out_specs,
        ],
        out_shape=[
            jax.ShapeDtypeStruct((BATCH * SEQ, D_MODEL), F32),
            jax.ShapeDtypeStruct((BATCH, 2, 2 * D_FF), F32),
            *nxt_out_shapes,
        ],
        scratch_shapes=_ffn_scratch(ts, SUBLANES),
        compiler_params=pltpu.CompilerParams(
            dimension_semantics=("arbitrary", "arbitrary"), vmem_limit_bytes=VMEM_LIMIT),
        name="ffn_prompt",
    )(h, mod, *nxt_args, w_up_b, p["ffn_conv_w"], p["ffn_conv_b"], w_dn_b, p["ln2_g"], p["ln2_b"])


def _ffn_sample_call(l, h3, mod, past, p, w_up_b, w_dn_b):
    tb = TB_S
    rr = tb * DEC_SEQ
    ls = functools.partial(_layer_spec, l=l, n_grid=1)
    cs = functools.partial(_const_spec, n_grid=1)
    kern = functools.partial(_ffn_kernel, rows=rr, sample=True, convert_next=False)
    return pl.pallas_call(
        kern,
        grid=(DEC_BATCH // tb,),
        in_specs=[
            pl.BlockSpec((tb, DEC_SEQ, D_MODEL), lambda i: (i, 0, 0)),
            pl.BlockSpec((None, tb, 6, D_MODEL), lambda i: (l, i, 0, 0)),
            pl.BlockSpec((None, tb, 2, 2 * D_FF), lambda i: (l, i, 0, 0)),
            cs((D_MODEL, 2 * D_FF)),
            ls((3, 2 * D_FF)),
            ls((1, 2 * D_FF)),
            cs((D_FF, D_MODEL)),
            ls((1, D_MODEL)),
            ls((1, D_MODEL)),
        ],
        out_specs=[
            pl.BlockSpec((tb, DEC_SEQ, D_MODEL), lambda i: (i, 0, 0)),
            pl.BlockSpec((tb, 2, 2 * D_FF), lambda i: (i, 0, 0)),
        ],
        out_shape=[
            jax.ShapeDtypeStruct((DEC_BATCH, DEC_SEQ, D_MODEL), F32),
            jax.ShapeDtypeStruct((DEC_BATCH, 2, 2 * D_FF), F32),
        ],
        scratch_shapes=_ffn_scratch(rr, 0),
        compiler_params=pltpu.CompilerParams(
            dimension_semantics=("arbitrary",), vmem_limit_bytes=VMEM_LIMIT),
        name="ffn_sample",
    )(h3, mod, past, w_up_b, p["ffn_conv_w"], p["ffn_conv_b"], w_dn_b, p["ln2_g"], p["ln2_b"])


def _prep_params(w_in, conv_a_w, sgu_ln_g, sgu_ln_b, sgu_w, sgu_b, gla_w_a2, gla_b_a, gla_norm_g,
                 w_o, ln1_g, ln1_b, ffn_w_up, ffn_conv_w, ffn_conv_b, ffn_w_down, ln2_g, ln2_b):
    tril = jnp.tril(jnp.ones((CHUNK, CHUNK), bool))
    w_p = jnp.where(tril[None, None], sgu_w, 0.0)
    wcat_p = jnp.transpose(w_p, (0, 2, 1, 3)).reshape(DEPTH, CHUNK, H_B * CHUNK)
    tril8 = jnp.tril(jnp.ones((DEC_SEQ, DEC_SEQ), bool))
    w8 = jnp.where(tril8[None, None], sgu_w[:, :, :DEC_SEQ, :DEC_SEQ], 0.0)
    eye = jnp.eye(GROUP_SEQ, dtype=F32)
    w_blk = jnp.einsum("ij,lhts->lhitjs", eye, w8).reshape(DEPTH, H_B, CHUNK, CHUNK)
    wcat_s = jnp.transpose(w_blk, (0, 2, 1, 3)).reshape(DEPTH, CHUNK, H_B * CHUNK)
    sbias_p = jnp.repeat(jnp.transpose(sgu_b, (0, 2, 1)), DH_B, axis=2)
    sbias_s = jnp.tile(sbias_p[:, :DEC_SEQ, :], (1, GROUP_SEQ, 1))
    w_in_t = jnp.swapaxes(w_in, 1, 2)
    return {
        "w_in_t": w_in_t,
        "w_in_tail": jnp.pad(w_in_t[:, N_IN_SLABS * LANES:, :],
                             ((0, 0), (0, N_IN_PAD - N_IN), (0, 0))),
        "w_o_f32": w_o, "ffn_w_up": ffn_w_up, "ffn_w_down": ffn_w_down,
        "conv_a_w": conv_a_w,
        "sgu_ln_g": sgu_ln_g.reshape(DEPTH, 1, D_B),
        "sgu_ln_b": sgu_ln_b.reshape(DEPTH, 1, D_B),
        "wcat_p": wcat_p, "wcat_s": wcat_s, "sbias_p": sbias_p, "sbias_s": sbias_s,
        "w_a2": jnp.pad(gla_w_a2, ((0, 0), (0, RANK_PAD - GATE_RANK), (0, 0))).astype(BF16),
        "b_a": gla_b_a.reshape(DEPTH, 1, H_C * DK_C),
        "gla_norm_g": gla_norm_g.reshape(DEPTH, 1, DV_C),
        "ln1_g": ln1_g.reshape(DEPTH, 1, D_MODEL), "ln1_b": ln1_b.reshape(DEPTH, 1, D_MODEL),
        "ffn_conv_w": ffn_conv_w,
        "ffn_conv_b": ffn_conv_b.reshape(DEPTH, 1, 2 * D_FF),
        "ln2_g": ln2_g.reshape(DEPTH, 1, D_MODEL), "ln2_b": ln2_b.reshape(DEPTH, 1, D_MODEL),
    }


def kernel(x_prompt, x_sample, c_prompt, c_sample, state_conv_a, state_gla, state_ffn_conv,
           ln_in_g, ln_in_b, w_ada, b_ada, w_in, conv_a_w, sgu_ln_g, sgu_ln_b, sgu_w, sgu_b,
           gla_w_a2, gla_b_a, gla_norm_g, w_o, ln1_g, ln1_b, ffn_w_up, ffn_conv_w, ffn_conv_b,
           ffn_w_down, ln2_g, ln2_b):
    p = _prep_params(w_in, conv_a_w, sgu_ln_g, sgu_ln_b, sgu_w, sgu_b, gla_w_a2, gla_b_a,
                     gla_norm_g, w_o, ln1_g, ln1_b, ffn_w_up, ffn_conv_w, ffn_conv_b,
                     ffn_w_down, ln2_g, ln2_b)
    mod = _ada_call(jnp.concatenate([c_sample, c_prompt], axis=0), w_ada, b_ada)
    mod_p = mod_s = mod
    s0_all = state_gla.reshape(DEPTH, DEC_BATCH, H_C * DK_C, DV_C)

    p["ln_in_g"] = ln_in_g.reshape(1, D_MODEL)
    p["ln_in_b"] = ln_in_b.reshape(1, D_MODEL)
    hp = x_prompt.reshape(BATCH * SEQ, D_MODEL)
    hs = x_sample

    outs_p = [[], [], [], []]
    outs_s = [[], [], [], []]
    w_in_b, w_o_b = _convert_layer0_call(p["w_in_t"], p["w_in_tail"], w_o)
    for l in range(DEPTH):
        hp, ca, cv, gs, w_up_b, w_dn_b = _mixer_prompt_call(l, hp, mod_p, p, w_in_b, w_o_b)
        hs, ca_s_l, cv_s_l, gs_s_l = _mixer_sample_call(l, hs, mod_s, state_conv_a, s0_all, p,
                                                        w_in_b, w_o_b)
        ffn_out = _ffn_prompt_call(l, hp, mod_p, p, w_up_b, w_dn_b)
        hp, fs = ffn_out[0], ffn_out[1]
        if l + 1 < DEPTH:
            w_in_b, w_o_b = ffn_out[2], ffn_out[3]
        for acc, val in zip(outs_p, (ca, cv, gs, fs)):
            acc.append(val)
        hs, fs = _ffn_sample_call(l, hs, mod_s, state_ffn_conv, p, w_up_b, w_dn_b)
        for acc, val in zip(outs_s, (ca_s_l, cv_s_l, gs_s_l, fs)):
            acc.append(val)

    ca_p, cv_p, gla_p, ffn_p = [jnp.stack(v) for v in outs_p]
    ca_s, cv_s, gla_s, ffn_s = [jnp.stack(v) for v in outs_s]
    return (hp.reshape(BATCH, SEQ, D_MODEL), hs,
            ca_p, ca_s, cv_p, cv_s,
            gla_p.reshape(DEPTH, BATCH, H_C, DK_C, DV_C),
            gla_s.reshape(DEPTH, DEC_BATCH, H_C, DK_C, DV_C),
            ffn_p, ffn_s)
```
